```python
import jax, jax.numpy as jnp
from jax import lax
import numpy as np

D_MODEL = 1024
BATCH = 32
SEQ = 256
DEPTH = 4
DEC_BATCH = 8
DEC_SEQ = 1024
PAST_LEN = 512

GRID_W = 64
D_MIX = D_MODEL
H_A = 4
D_A = D_MIX // 2
DH_A = D_A // H_A
H_B = 4
D_B = D_MIX // 4
DH_B = D_B // H_B
H_C = 4
D_C = D_MIX // 4
DV_C = D_C // H_C
DK_C = DV_C // 2
GLA_RANK = 16
GLA_TAU = 16.0
MLSTM_CHUNK = 64
RET_CHUNK = 64
GLA_CHUNK = 32
ROPE_BASE = 10000.0
EPS = 1e-6
IN_WIDTHS = (D_A, D_A, D_A, D_A, D_A, 4 * H_A, D_B, D_B, D_B, D_B, H_C * DK_C, H_C * DK_C, D_C, D_C, 2 * GLA_RANK)
D_IN = 5 * D_A + 4 * H_A + 4 * D_B + 2 * H_C * DK_C + 2 * D_C + 2 * GLA_RANK

kernel_name = 'hybrid_mlstm_retention_gla_diffusion_step'


def _rmsnorm(x, g):
    x = x.astype(jnp.float32)
    return x * lax.rsqrt(jnp.mean(x * x, axis=-1, keepdims=True) + EPS) * g.astype(jnp.float32)


def _head_rms(y):
    return y * lax.rsqrt(jnp.mean(y * y, axis=-1, keepdims=True) + EPS)


def _chunk(t, c):
    return t.reshape((t.shape[0], t.shape[1] // c, c) + t.shape[2:])


def _grid_rotary(L):
    rows = L // GRID_W
    r = jnp.repeat(jnp.arange(rows, dtype=jnp.float32), GRID_W)
    col = jnp.tile(jnp.arange(GRID_W, dtype=jnp.float32), rows)
    n_f = DH_B // 4
    freqs = ROPE_BASE ** (-jnp.arange(n_f, dtype=jnp.float32) / n_f)
    ang = jnp.concatenate([r[:, None] * freqs, col[:, None] * freqs], axis=-1)
    return jnp.cos(ang)[None, :, None, :], jnp.sin(ang)[None, :, None, :]


def _rotate(t, cos, sin):
    t1, t2 = jnp.split(t, 2, axis=-1)
    return jnp.concatenate([t1 * cos - t2 * sin, t1 * sin + t2 * cos], axis=-1)


def _mlstm(q, k, v, i_pre, f_pre, state0):
    f32 = jnp.float32
    C0, n0, m0 = (s.astype(f32) for s in state0)
    Cs = MLSTM_CHUNK
    qc, kc, vc = _chunk(q, Cs), _chunk(k, Cs), _chunk(v, Cs)
    ic = _chunk(i_pre, Cs)
    F = jnp.cumsum(jax.nn.log_sigmoid(_chunk(f_pre, Cs)), axis=2)
    F_last = F[:, :, -1]
    g = F_last[:, :, None] - F + ic
    gm = jnp.max(g, axis=2)
    wg = jnp.exp(g - gm[:, :, None])
    U_C = jnp.einsum('bnjh,bnjhd,bnjhe->bnhde', wg, kc, vc)
    U_n = jnp.einsum('bnjh,bnjhd->bnhd', wg, kc)

    def step(carry, inp):
        Cp, npv, mp = carry
        fl, gmx, uc, un = inp
        m_new = jnp.maximum(fl + mp, gmx)
        a = jnp.exp(fl + mp - m_new)
        b = jnp.exp(gmx - m_new)
        new = (a[..., None, None] * Cp + b[..., None, None] * uc, a[..., None] * npv + b[..., None] * un, m_new)
        return new, carry

    mv = lambda t: jnp.moveaxis(t, 1, 0)
    (C_last, n_last, m_last), (C_prev, n_prev, m_prev) = lax.scan(
        step, (C0, n0, m0), (mv(F_last), mv(gm), mv(U_C), mv(U_n)))
    C_prev, n_prev, m_prev = mv(C_prev), mv(n_prev), mv(m_prev)
    a_in = F + m_prev[:, :, None]
    causal = jnp.tril(jnp.ones((Cs, Cs), dtype=bool))
    D = F[:, :, :, None] - F[:, :, None, :] + ic[:, :, None, :]
    D = jnp.where(causal[:, :, None], D, -jnp.inf)
    m_i = jnp.maximum(a_in, jnp.max(D, axis=3))
    wD = jnp.exp(D - m_i[:, :, :, None])
    wa = jnp.exp(a_in - m_i)
    qk = jnp.einsum('bnihd,bnjhd->bnijh', qc, kc) * wD
    num = jnp.einsum('bnijh,bnjhe->bnihe', qk, vc) + wa[..., None] * jnp.einsum('bnihd,bnhde->bnihe', qc, C_prev)
    den = jnp.sum(qk, axis=3) + wa * jnp.einsum('bnihd,bnhd->bnih', qc, n_prev)
    h = num / jnp.maximum(jnp.abs(den), jnp.exp(-m_i))[..., None]
    return h.reshape(v.shape), (C_last, n_last, m_last)


def _retention(q, k, v, log_gamma, S0):
    f32 = jnp.float32
    S0 = S0.astype(f32)
    Cs = RET_CHUNK
    qc, kc, vc = _chunk(q, Cs), _chunk(k, Cs), _chunk(v, Cs)
    pos = jnp.arange(Cs, dtype=f32)
    rel = pos[:, None] - pos[None, :]
    decay = jnp.where((rel >= 0)[..., None], jnp.exp(jnp.maximum(rel, 0.0)[..., None] * log_gamma), 0.0)
    scores = jnp.einsum('bnihd,bnjhd->bnhij', qc, kc) * jnp.transpose(decay, (2, 0, 1))
    intra = jnp.einsum('bnhij,bnjhe->bnihe', scores, vc)
    w_end = jnp.exp((Cs - 1 - pos)[:, None] * log_gamma)
    U = jnp.einsum('bnjhd,jh,bnjhe->bnhde', kc, w_end, vc)
    g_chunk = jnp.exp(Cs * log_gamma)[:, None, None]

    def step(S, U_n):
        return g_chunk * S + U_n, S

    S_last, S_prev = lax.scan(step, S0, jnp.moveaxis(U, 1, 0))
    S_prev = jnp.moveaxis(S_prev, 0, 1)
    w_q = jnp.exp((pos + 1.0)[:, None] * log_gamma)
    inter = jnp.einsum('bnihd,ih,bnhde->bnihe', qc, w_q, S_prev)
    return (intra + inter).reshape(v.shape), S_last


def _gla(q, k, v, log_a, S0):
    f32 = jnp.float32
    S0 = S0.astype(f32)
    Cs = GLA_CHUNK
    qc, kc, vc = _chunk(q, Cs), _chunk(k, Cs), _chunk(v, Cs)
    b = jnp.cumsum(_chunk(log_a, Cs), axis=2)
    causal = jnp.tril(jnp.ones((Cs, Cs), dtype=bool))
    diff = b[:, :, :, None] - b[:, :, None, :]
    diff = jnp.where(causal[None, None, :, :, None, None], diff, -jnp.inf)
    scores = jnp.einsum('bnihd,bnjhd,bnijhd->bnhij', qc, kc, jnp.exp(diff))
    intra = jnp.einsum('bnhij,bnjhe->bnihe', scores, vc)
    b_last = b[:, :, -1]
    U = jnp.einsum('bnjhd,bnjhe->bnhde', kc * jnp.exp(b_last[:, :, None] - b), vc)

    def step(S, inp):
        gl, U_n = inp
        return jnp.exp(gl)[..., None] * S + U_n, S

    S_last, S_prev = lax.scan(step, S0, (jnp.moveaxis(b_last, 1, 0), jnp.moveaxis(U, 1, 0)))
    S_prev = jnp.moveaxis(S_prev, 0, 1)
    inter = jnp.einsum('bnihd,bnhde->bnihe', qc * jnp.exp(b), S_prev)
    return (intra + inter).reshape(v.shape), S_last


def _mixer(h, st_f, st_b, rot, w_in, gate_b, ret_logit, gla_w2, gla_b2, hn_g, w_out):
    f32 = jnp.float32
    B, L, _ = h.shape
    proj = jnp.einsum('bld,de->ble', h, w_in.astype(f32))
    (aq, ak, av, ao, az, ag, bq, bk, bv, bz, cq, ck, cv, cz, clr) = jnp.split(
        proj, np.cumsum(IN_WIDTHS)[:-1].tolist(), axis=-1)
    flip = lambda t: jnp.flip(t, axis=1)

    aq = aq.reshape(B, L, H_A, DH_A)
    ak = ak.reshape(B, L, H_A, DH_A) * DH_A ** -0.5
    av = av.reshape(B, L, H_A, DH_A)
    ag = (ag + gate_b.astype(f32)).reshape(B, L, 4, H_A)
    ha_f, sa_f = _mlstm(aq, ak, av, ag[:, :, 0], ag[:, :, 1], st_f[0:3])
    ha_b, sa_b = _mlstm(flip(aq), flip(ak), flip(av), flip(ag[:, :, 2]), flip(ag[:, :, 3]), st_b[0:3])
    ya = jax.nn.sigmoid(ao).reshape(B, L, H_A, DH_A) * (ha_f + flip(ha_b))

    bq = bq.reshape(B, L, H_B, DH_B)
    bk = bk.reshape(B, L, H_B, DH_B)
    bv = bv.reshape(B, L, H_B, DH_B)
    if rot is not None:
        bq = _rotate(bq, rot[0], rot[1])
        bk = _rotate(bk, rot[0], rot[1])
    bk = bk * DH_B ** -0.5
    log_gamma = jax.nn.log_sigmoid(ret_logit.astype(f32))
    hb_f, sb_f = _retention(bq, bk, bv, log_gamma[0], st_f[3])
    hb_b, sb_b = _retention(flip(bq), flip(bk), flip(bv), log_gamma[1], st_b[3])
    yb = hb_f + flip(hb_b)

    cq = cq.reshape(B, L, H_C, DK_C)
    ck = ck.reshape(B, L, H_C, DK_C) * DK_C ** -0.5
    cv = cv.reshape(B, L, H_C, DV_C)
    clr = clr.reshape(B, L, 2, GLA_RANK)
    log_a = jax.nn.log_sigmoid(jnp.einsum('bldr,drk->bldk', clr, gla_w2.astype(f32)) + gla_b2.astype(f32)) / GLA_TAU
    log_a = log_a.reshape(B, L, 2, H_C, DK_C)
    hc_f, sc_f = _gla(cq, ck, cv, log_a[:, :, 0], st_f[4])
    hc_b, sc_b = _gla(flip(cq), flip(ck), flip(cv), flip(log_a[:, :, 1]), st_b[4])
    yc = hc_f + flip(hc_b)

    y = jnp.concatenate([_head_rms(ya).reshape(B, L, D_A), _head_rms(yb).reshape(B, L, D_B),
                         _head_rms(yc).reshape(B, L, D_C)], axis=-1)
    y = y * hn_g.astype(f32) * jax.nn.silu(jnp.concatenate([az, bz, cz], axis=-1))
    out = jnp.einsum('ble,ed->bld', y, w_out.astype(f32))
    return out, (sa_f[0], sa_f[1], sa_f[2], sb_f, sc_f), (sa_b[0], sa_b[1], sa_b[2], sb_b, sc_b)


def _layer(x, mod, st_f, st_b, rot, g, w_in, gate_b, ret_logit, gla_w2, gla_b2, hn_g, w_out):
    shift, scale, gate = jnp.split(mod[:, None, :], 3, axis=-1)
    h = _rmsnorm(x, g) * (1.0 + scale) + shift
    out, sf, sb = _mixer(h, st_f, st_b, rot, w_in, gate_b, ret_logit, gla_w2, gla_b2, hn_g, w_out)
    return (x.astype(jnp.float32) + gate * out).astype(x.dtype), sf, sb


def _modulation(cvec, w, b):
    return jnp.einsum('bd,de->be', jax.nn.silu(cvec.astype(jnp.float32)), w.astype(jnp.float32)) + b.astype(jnp.float32)


def _zero_state(B):
    z = lambda *s: jnp.zeros((B,) + s, jnp.float32)
    return (z(H_A, DH_A, DH_A), z(H_A, DH_A), z(H_A), z(H_B, DH_B, DH_B), z(H_C, DK_C, DV_C))


def setup_inputs(seed: int = 0) -> dict:
    key = jax.random.key(seed)
    ks = jax.random.split(key, 24)
    f32 = jnp.float32
    nrm = lambda k, shape, s: s * jax.random.normal(k, shape, f32)
    x_prompt = nrm(ks[0], (BATCH, SEQ, D_MODEL), 1.0)
    x_sample = nrm(ks[1], (DEC_BATCH, DEC_SEQ, D_MODEL), 1.0)
    state_mlstm_C = nrm(ks[2], (DEC_BATCH, DEPTH, 2, H_A, DH_A, DH_A), 0.5)
    state_mlstm_n = nrm(ks[3], (DEC_BATCH, DEPTH, 2, H_A, DH_A), 0.5)
    state_mlstm_m = nrm(ks[4], (DEC_BATCH, DEPTH, 2, H_A), 1.0)
    state_ret = nrm(ks[5], (DEC_BATCH, DEPTH, 2, H_B, DH_B, DH_B), 1.0)
    state_gla = nrm(ks[6], (DEC_BATCH, DEPTH, 2, H_C, DK_C, DV_C), 1.0)
    c = nrm(ks[7], (DEC_BATCH, D_MODEL), 1.0)
    c_ctx = nrm(ks[8], (D_MODEL,), 1.0)
    norm_g = 1.0 + nrm(ks[9], (DEPTH, D_MODEL), 0.02)
    w_ada = nrm(ks[10], (DEPTH, D_MODEL, 3 * D_MODEL), 0.5 * D_MODEL ** -0.5)
    b_ada = nrm(ks[11], (DEPTH, 3 * D_MODEL), 0.02)
    w_in = nrm(ks[12], (DEPTH, D_MODEL, D_IN), D_MODEL ** -0.5)
    i_bias = nrm(ks[13], (DEPTH, 2, 1, H_A), 0.1)
    f_bias = jnp.linspace(3.0, 6.0, H_A, dtype=f32) + nrm(ks[14], (DEPTH, 2, 1, H_A), 0.1)
    mlstm_gate_b = jnp.concatenate([i_bias, f_bias], axis=2).reshape(DEPTH, 4 * H_A)
    e = 5.0 + jnp.arange(H_B, dtype=f32)
    ret_decay_logit = jnp.log(2.0 ** e - 1.0) + nrm(ks[15], (DEPTH, 2, H_B), 0.05)
    gla_w2 = nrm(ks[16], (DEPTH, 2, GLA_RANK, H_C * DK_C), GLA_RANK ** -0.5)
    gla_b2 = 1.0 + nrm(ks[17], (DEPTH, 2, H_C * DK_C), 0.1)
    headnorm_g = 1.0 + nrm(ks[18], (DEPTH, D_MIX), 0.02)
    w_out = nrm(ks[19], (DEPTH, D_MIX, D_MODEL), D_MIX ** -0.5)
    final_g = 1.0 + nrm(ks[20], (D_MODEL,), 0.02)
    return {'x_prompt': x_prompt, 'x_sample': x_sample, 'state_mlstm_C': state_mlstm_C,
            'state_mlstm_n': state_mlstm_n, 'state_mlstm_m': state_mlstm_m, 'state_ret': state_ret,
            'state_gla': state_gla, 'c': c, 'c_ctx': c_ctx, 'norm_g': norm_g, 'w_ada': w_ada, 'b_ada': b_ada,
            'w_in': w_in, 'mlstm_gate_b': mlstm_gate_b, 'ret_decay_logit': ret_decay_logit, 'gla_w2': gla_w2,
            'gla_b2': gla_b2, 'headnorm_g': headnorm_g, 'w_out': w_out, 'final_g': final_g}


def reference(x_prompt, x_sample, state_mlstm_C, state_mlstm_n, state_mlstm_m, state_ret, state_gla, c, c_ctx,
              norm_g, w_ada, b_ada, w_in, mlstm_gate_b, ret_decay_logit, gla_w2, gla_b2, headnorm_g, w_out, final_g):
    Bp = x_prompt.shape[0]
    Ls = x_sample.shape[1]
    rot = _grid_rotary(Ls)
    caches = (state_mlstm_C, state_mlstm_n, state_mlstm_m, state_ret, state_gla)
    xp, xs = x_prompt, x_sample
    ctx_states = []
    for l in range(DEPTH):
        lp = (norm_g[l], w_in[l], mlstm_gate_b[l], ret_decay_logit[l], gla_w2[l], gla_b2[l], headnorm_g[l], w_out[l])
        mod_ctx = _modulation(c_ctx[None, :], w_ada[l], b_ada[l])
        mod_lat = _modulation(c, w_ada[l], b_ada[l])
        xp, sf, sb = _layer(xp, mod_ctx, _zero_state(Bp), _zero_state(Bp), None, *lp)
        ctx_states.append([jnp.stack([a, b], axis=1) for a, b in zip(sf, sb)])
        cache_f = [s[:, l, 0] for s in caches]
        cache_b = [s[:, l, 1] for s in caches]
        xs, _, _ = _layer(xs, mod_lat, cache_f, cache_b, rot, *lp)
    new_mlstm_C = jnp.stack([s[0] for s in ctx_states], axis=1)
    new_mlstm_n = jnp.stack([s[1] for s in ctx_states], axis=1)
    new_mlstm_m = jnp.stack([s[2] for s in ctx_states], axis=1)
    new_ret = jnp.stack([s[3] for s in ctx_states], axis=1)
    new_gla = jnp.stack([s[4] for s in ctx_states], axis=1)
    y_prompt = _rmsnorm(xp, final_g).astype(x_prompt.dtype)
    y_sample = _rmsnorm(xs, final_g).astype(x_sample.dtype)
    return (y_prompt, y_sample, new_mlstm_C, new_mlstm_n, new_mlstm_m, new_ret, new_gla)
```

```python
import functools

import jax
import jax.numpy as jnp
import numpy as np
from jax import lax
from jax.experimental import pallas as pl
from jax.experimental.pallas import tpu as pltpu

F32 = jnp.float32
BF16 = jnp.bfloat16

D_MODEL = 1024
H_A, DH_A = 4, 128
H_B, DH_B = 4, 64
H_C, DK_C, DV_C = 4, 32, 64
D_A, D_B, D_C = H_A * DH_A, H_B * DH_B, H_C * DV_C
GLA_RANK = 16
GLA_TAU = 16.0
GRID_W = 64
ROPE_BASE = 10000.0
EPS = 1e-6
IN_WIDTHS = (D_A, D_A, D_A, D_A, D_A, 4 * H_A, D_B, D_B, D_B, D_B, H_C * DK_C, H_C * DK_C, D_C, D_C, 2 * GLA_RANK)

LANES = 128
CH = 256
TOK = 1024
NCH = TOK // CH
VMEM_LIMIT = 60 * 1024 * 1024
NEG = -1e30

OFF_A, W_A = 0, 4 * D_A
OFF_B, W_B = OFF_A + W_A, 3 * D_B
OFF_C, W_C = OFF_B + W_B, 2 * H_C * DK_C + D_C
OFF_Z, W_Z = OFF_C + W_C, D_MODEL
OFF_S, W_S = OFF_Z + W_Z, LANES
D_INR = OFF_S + W_S
COL_FF, COL_FB = 4, 12


def _dot(a, b):
    return jnp.dot(a, b, preferred_element_type=F32)


def _dot_nt(a, b):
    return lax.dot_general(a, b, (((1,), (1,)), ((), ())), preferred_element_type=F32)


def _bf(x):
    return x.astype(BF16)


def _split3(x):
    hi = _bf(x)
    r1 = x - hi.astype(F32)
    mid = _bf(r1)
    lo = _bf(r1 - mid.astype(F32))
    return hi, mid, lo


def _exact_dot_l(m_bf, x):
    hi, mid, lo = _split3(x)
    return (_dot(m_bf, lo) + _dot(m_bf, mid)) + _dot(m_bf, hi)


def _exact_dot_r(x, m_bf):
    hi, mid, lo = _split3(x)
    return (_dot(lo, m_bf) + _dot(mid, m_bf)) + _dot(hi, m_bf)


def _log_sigmoid(x):
    return jnp.minimum(x, 0.0) - jnp.log(1.0 + jnp.exp(-jnp.abs(x)))


def _sigmoid(x):
    return 1.0 / (1.0 + jnp.exp(-x))


def _row_to_col(row, n):
    eye = lax.broadcasted_iota(jnp.int32, (n, n), 0) == lax.broadcasted_iota(jnp.int32, (n, n), 1)
    return jnp.sum(jnp.where(eye, row, 0.0), axis=1, keepdims=True)


def _lane_block_mask(width, block, h):
    lane = lax.broadcasted_iota(jnp.int32, (1, width), 1)
    return ((lane >= h * block) & (lane < (h + 1) * block)).astype(F32)


def _log2(n):
    assert n & (n - 1) == 0
    return n.bit_length() - 1


def _block_diag_mask(rows, cols, rblock, cblock):
    r = lax.broadcasted_iota(jnp.int32, (rows, cols), 0) >> _log2(rblock)
    c = lax.broadcasted_iota(jnp.int32, (rows, cols), 1) >> _log2(cblock)
    return (r == c).astype(F32)


def _causal_masks():
    ri = lax.broadcasted_iota(jnp.int32, (CH, CH), 0)
    cj = lax.broadcasted_iota(jnp.int32, (CH, CH), 1)
    return cj <= ri, cj >= ri


def _bwd_gate_lanes():
    lane = lax.broadcasted_iota(jnp.int32, (1, LANES), 1)
    return (lane >= COL_FB) & (lane < COL_FB + H_A)


def _modulation_kernel(c_ref, w_ref, b_ref, o_ref):
    cv = c_ref[...]
    s = cv * _sigmoid(cv)
    o_ref[...] = _dot(_bf(s), _bf(w_ref[...])) + b_ref[...]


def _modulation(cstack, w_ada, b_ada):
    depth, d, d3 = w_ada.shape
    rows = cstack.shape[0]
    tn = 512
    return pl.pallas_call(
        _modulation_kernel,
        out_shape=jax.ShapeDtypeStruct((depth, rows, d3), F32),
        grid=(depth, d3 // tn),
        in_specs=[
            pl.BlockSpec((rows, d), lambda l, j: (0, 0)),
            pl.BlockSpec((None, d, tn), lambda l, j: (l, 0, j)),
            pl.BlockSpec((None, 1, tn), lambda l, j: (l, 0, j)),
        ],
        out_specs=pl.BlockSpec((None, rows, tn), lambda l, j: (l, 0, j)),
        name="adaln_modulation",
    )(cstack, w_ada, b_ada.reshape(depth, 1, d3))


def _layer_kernel(*refs, chained, rotary, final):
    it = iter(refs)
    x_ref, mod_ref, ng_ref, win_ref, gb_ref, lg_ref, w2_ref, b2_ref, hng_ref, wout_ref = (next(it) for _ in range(10))
    fg_ref = next(it) if final else None
    if rotary:
        cos_ref, sin_ref = next(it), next(it)
    if chained:
        sC_ref, sn_ref, sm_ref_in, sR_ref, sG_ref = (next(it) for _ in range(5))
    out_ref = next(it)
    if not chained:
        oC_ref, on_ref, om_ref, oR_ref, oG_ref = (next(it) for _ in range(5))
    (h_ref, pj_ref, sm_ref, la_ref, fc_ref, bf_ref, bb_ref, tot_ref, y_ref,
     gm_ref, min_ref, dec_ref, rw_ref) = (next(it) for _ in range(13))
    if chained:
        uC_ref, un_ref, uR_ref, uG_ref = (next(it) for _ in range(4))
    else:
        uC_ref, un_ref, uR_ref, uG_ref = oC_ref, on_ref, oR_ref, oG_ref

    seqs, seqlen, _ = x_ref.shape

    def x_rows(ref, c):
        r0 = pl.multiple_of(c * CH, CH)
        return ref.at[r0 >> _log2(seqlen), pl.ds(pl.multiple_of(r0 & (seqlen - 1), CH), CH), :]

    def norm_body(c, carry):
        rows = pl.ds(pl.multiple_of(c * CH, CH), CH)
        x = x_rows(x_ref, c)[...]
        shift, scale = mod_ref[:, 0:D_MODEL], mod_ref[:, D_MODEL:2 * D_MODEL]
        xn = x * lax.rsqrt(jnp.mean(x * x, axis=-1, keepdims=True) + EPS) * ng_ref[...]
        h_ref[rows, :] = _bf(xn * (1.0 + scale) + shift)
        return carry

    lax.fori_loop(0, NCH, norm_body, 0)

    sm_ref[...] = _dot(h_ref[...], win_ref[:, OFF_S:OFF_S + W_S])

    def la_body(c, carry):
        rows = pl.ds(pl.multiple_of(c * CH, CH), CH)
        la_ref[rows, :] = _log_sigmoid(_dot(_bf(sm_ref[rows, :]), w2_ref[...]) + b2_ref[...]) * (1.0 / GLA_TAU)
        return carry

    lax.fori_loop(0, NCH, la_body, 0)

    def prefix_body(c, carry):
        rows = pl.ds(pl.multiple_of(c * CH, CH), CH)
        bwd_cols = _bwd_gate_lanes()
        tri = _bf(_causal_masks()[0].astype(F32))
        ls = _log_sigmoid(sm_ref[rows, :] + gb_ref[...])
        la = la_ref[rows, :]
        xcat = jnp.concatenate([ls, la], axis=1)
        ps = _exact_dot_l(tri, xcat)
        tot = ps[CH - 1:CH, :]
        fc_ref[rows, :] = jnp.where(bwd_cols, tot[:, 0:LANES] - ps[:, 0:LANES] + ls, ps[:, 0:LANES])
        bf_ref[rows, :] = ps[:, LANES:2 * LANES]
        bb_ref[rows, :] = tot[:, 2 * LANES:] - ps[:, 2 * LANES:] + la[:, LANES:]
        tot_ref[c] = tot
        return carry

    lax.fori_loop(0, NCH, prefix_body, 0)

    pj_ref[:, 0:W_A] = _dot(h_ref[...], win_ref[:, OFF_A:OFF_A + W_A])
    AQ, AK, AV, AO = 0, D_A, 2 * D_A, 3 * D_A
    k_scale = DH_A ** -0.5

    def mlstm_state_body(c, carry):
        rows = pl.ds(pl.multiple_of(c * CH, CH), CH)
        g_pre = sm_ref[rows, :] + gb_ref[...]
        rw = pltpu.roll(g_pre, 4, 1) - fc_ref[rows, :]
        rw_ref[rows, :] = rw
        totg = tot_ref[c][:, 0:LANES]
        g = totg + rw
        gm = jnp.max(g, axis=0, keepdims=True)
        gm_ref[c] = gm
        wg = jnp.exp(g - gm)
        for hh in range(H_A):
            k_h = pj_ref[rows, AK + hh * DH_A:AK + (hh + 1) * DH_A] * k_scale
            v_h = pj_ref[rows, AV + hh * DH_A:AV + (hh + 1) * DH_A]
            wf = wg[:, COL_FF + hh:COL_FF + hh + 1]
            wb = wg[:, COL_FB + hh:COL_FB + hh + 1]
            rhs = _bf(jnp.concatenate([wf * v_h, wb * v_h], axis=1))
            u = _dot(_bf(k_h.T), rhs)
            uC_ref[c, 0, hh] = u[:, 0:DH_A]
            uC_ref[c, 1, hh] = u[:, DH_A:]
            un_ref[c, 0, hh:hh + 1, :] = jnp.sum(wf * k_h, axis=0, keepdims=True)
            un_ref[c, 1, hh:hh + 1, :] = jnp.sum(wb * k_h, axis=0, keepdims=True)
        return carry

    lax.fori_loop(0, NCH, mlstm_state_body, 0)

    def mlstm_step(m_prev, c, d, col):
        totg = tot_ref[c][:, 0:LANES]
        gm = gm_ref[c]
        m_new = jnp.maximum(totg + m_prev, gm)
        return m_new, jnp.exp(totg + m_prev - m_new), jnp.exp(gm - m_new)

    if chained:
        for d, order, col in ((0, range(NCH), COL_FF), (1, range(NCH - 1, -1, -1), COL_FB)):
            m_run = sm_ref_in[...]
            c_run = [sC_ref[d, hh] for hh in range(H_A)]
            n_run = [sn_ref[d, hh:hh + 1, :] for hh in range(H_A)]
            for idx, c in enumerate(order):
                last = idx == NCH - 1
                if not last:
                    m_new, a_row, b_row = mlstm_step(m_run, c, d, col)
                for hh in range(H_A):
                    u_c, u_n = uC_ref[c, d, hh], un_ref[c, d, hh:hh + 1, :]
                    uC_ref[c, d, hh] = c_run[hh]
                    un_ref[c, d, hh:hh + 1, :] = n_run[hh]
                    if not last:
                        a_s = a_row[:, col + hh:col + hh + 1]
                        b_s = b_row[:, col + hh:col + hh + 1]
                        c_run[hh] = a_s * c_run[hh] + b_s * u_c
                        n_run[hh] = a_s * n_run[hh] + b_s * u_n
                min_ref[c] = m_run if d == 0 else jnp.where(_bwd_gate_lanes(), m_run, min_ref[c])
                if not last:
                    m_run = m_new
    else:
        zero_row = jnp.zeros((1, LANES), F32)
        for c in range(NCH):
            m_new, _, b_row = mlstm_step(zero_row, c, 0, 0)
            for d, col in ((0, COL_FF), (1, COL_FB)):
                for hh in range(H_A):
                    b_s = b_row[:, col + hh:col + hh + 1]
                    oC_ref[c, d, hh] = b_s * uC_ref[c, d, hh]
                    on_ref[c, d, hh:hh + 1, :] = b_s * un_ref[c, d, hh:hh + 1, :]
            om_ref[c] = m_new
            min_ref[c] = zero_row

    def mlstm_out_body(c, carry):
        rows = pl.ds(pl.multiple_of(c * CH, CH), CH)
        lower, upper = _causal_masks()
        fcum = fc_ref[rows, :]
        rwt = rw_ref[rows, :].T
        ain_all = fcum + min_ref[c]
        for hh in range(H_A):
            q_h = pj_ref[rows, AQ + hh * DH_A:AQ + (hh + 1) * DH_A]
            k_h = pj_ref[rows, AK + hh * DH_A:AK + (hh + 1) * DH_A] * k_scale
            v_h = pj_ref[rows, AV + hh * DH_A:AV + (hh + 1) * DH_A]
            o_h = pj_ref[rows, AO + hh * DH_A:AO + (hh + 1) * DH_A]
            s = _dot_nt(_bf(q_h), _bf(k_h))
            p = None
            q_sc = []
            for d, col, msk in ((0, COL_FF + hh, lower), (1, COL_FB + hh, upper)):
                dm = jnp.where(msk, fcum[:, col:col + 1] + rwt[col:col + 1, :], NEG)
                ain = ain_all[:, col:col + 1]
                mi = jnp.maximum(ain, jnp.max(dm, axis=1, keepdims=True))
                qk = s * jnp.exp(dm - mi)
                den = jnp.sum(qk, axis=1, keepdims=True)
                if chained:
                    wa = jnp.exp(ain - mi)
                    den = den + wa * jnp.sum(q_h * un_ref[c, d, hh:hh + 1, :], axis=1, keepdims=True)
                r = 1.0 / jnp.maximum(jnp.abs(den), jnp.exp(-mi))
                p = qk * r if p is None else p + qk * r
                if chained:
                    q_sc.append(q_h * (wa * r))
            hs = _dot(_bf(p), _bf(v_h))
            if chained:
                c_in = jnp.concatenate([uC_ref[c, 0, hh], uC_ref[c, 1, hh]], axis=0)
                hs = hs + _dot(_bf(jnp.concatenate(q_sc, axis=1)), _bf(c_in))
            y_ref[rows, hh * DH_A:(hh + 1) * DH_A] = _sigmoid(o_h) * hs
        return carry

    lax.fori_loop(0, NCH, mlstm_out_body, 0)

    pj_ref[:, 0:W_B] = _dot(h_ref[...], win_ref[:, OFF_B:OFF_B + W_B])
    BQ, BK, BV = 0, D_B, 2 * D_B
    if rotary:
        def rotary_body(c, carry):
            rows = pl.ds(pl.multiple_of(c * CH, CH), CH)
            first_half = (lax.broadcasted_iota(jnp.int32, (1, D_B), 1) & (DH_B - 1)) < DH_B // 2
            for off in (BQ, BK):
                t = pj_ref[rows, off:off + D_B]
                partner = jnp.where(first_half, pltpu.roll(t, D_B - DH_B // 2, 1), pltpu.roll(t, DH_B // 2, 1))
                pj_ref[rows, off:off + D_B] = t * cos_ref[rows, :] + partner * sin_ref[rows, :]
            return carry

        lax.fori_loop(0, NCH, rotary_body, 0)

    lgam = _log_sigmoid(lg_ref[...])
    lg_f, lg_b = lgam[0:1, :], lgam[1:2, :]
    pos = lax.broadcasted_iota(jnp.int32, (CH, 1), 0).astype(F32)
    lower, upper = _causal_masks()
    rel = (lax.broadcasted_iota(jnp.int32, (CH, CH), 0) - lax.broadcasted_iota(jnp.int32, (CH, CH), 1)).astype(F32)
    for hh in range(H_B):
        lf = lg_f[:, hh * DH_B:hh * DH_B + 1]
        lb = lg_b[:, hh * DH_B:hh * DH_B + 1]
        dec_ref[hh] = (jnp.where(lower, jnp.exp(jnp.maximum(rel, 0.0) * lf), 0.0)
                       + jnp.where(upper, jnp.exp(jnp.maximum(-rel, 0.0) * lb), 0.0))
    bd_b = _block_diag_mask(D_B, D_B, DH_B, DH_B)
    hm_b = [_lane_block_mask(D_B, DH_B, hh) for hh in range(H_B)]
    ret_scale = DH_B ** -0.5

    def ret_state_body(c, carry):
        rows = pl.ds(pl.multiple_of(c * CH, CH), CH)
        k = pj_ref[rows, BK:BK + D_B] * ret_scale
        v = pj_ref[rows, BV:BV + D_B]
        kf = k * jnp.exp((CH - 1.0 - pos) * lg_f)
        kb = k * jnp.exp(pos * lg_b)
        u = _dot(_bf(jnp.concatenate([kf, kb], axis=1).T), _bf(v))
        uR_ref[c, 0] = u[0:D_B, :] * bd_b
        uR_ref[c, 1] = u[D_B:, :] * bd_b
        return carry

    lax.fori_loop(0, NCH, ret_state_body, 0)

    if chained:
        for d, order, lg_row in ((0, range(NCH), lg_f), (1, range(NCH - 1, -1, -1), lg_b)):
            g_col = _row_to_col(jnp.exp(float(CH) * lg_row), D_B)
            s_run = sR_ref[d]
            for idx, c in enumerate(order):
                u = uR_ref[c, d]
                uR_ref[c, d] = s_run
                if idx != NCH - 1:
                    s_run = g_col * s_run + u

    def ret_out_body(c, carry):
        rows = pl.ds(pl.multiple_of(c * CH, CH), CH)
        q = pj_ref[rows, BQ:BQ + D_B]
        k = pj_ref[rows, BK:BK + D_B] * ret_scale
        v = pj_ref[rows, BV:BV + D_B]
        kb16 = _bf(k)
        ps = [_bf(_dot_nt(_bf(q * hm_b[hh]), kb16) * dec_ref[hh]) for hh in range(H_B)]
        vst = jnp.concatenate([_bf(v * hm_b[hh]) for hh in range(H_B)], axis=0)
        yb = _dot(jnp.concatenate(ps, axis=1), vst)
        if chained:
            qf = q * jnp.exp((pos + 1.0) * lg_f)
            qb = q * jnp.exp((float(CH) - pos) * lg_b)
            s_in = jnp.concatenate([uR_ref[c, 0], uR_ref[c, 1]], axis=0)
            yb = yb + _dot(_bf(jnp.concatenate([qf, qb], axis=1)), _bf(s_in))
        y_ref[rows, D_A:D_A + D_B] = yb
        return carry

    lax.fori_loop(0, NCH, ret_out_body, 0)

    pj_ref[:, 0:W_C] = _dot(h_ref[...], win_ref[:, OFF_C:OFF_C + W_C])
    KC = H_C * DK_C
    CQ, CK, CV = 0, KC, 2 * KC
    gla_scale = DK_C ** -0.5
    bd_c = _block_diag_mask(KC, D_C, DK_C, DV_C)
    hm_ck = [_lane_block_mask(KC, DK_C, hh) for hh in range(H_C)]
    hm_cv = [_lane_block_mask(D_C, DV_C, hh) for hh in range(H_C)]

    def gla_state_body(c, carry):
        rows = pl.ds(pl.multiple_of(c * CH, CH), CH)
        k = pj_ref[rows, CK:CK + KC] * gla_scale
        v = pj_ref[rows, CV:CV + D_C]
        tot = tot_ref[c]
        khf = k * jnp.exp(tot[:, LANES:2 * LANES] - bf_ref[rows, :])
        khb = k * jnp.exp(tot[:, 2 * LANES:] - bb_ref[rows, :])
        u = _dot(_bf(jnp.concatenate([khf, khb], axis=1).T), _bf(v))
        uG_ref[c, 0] = u[0:KC, :] * bd_c
        uG_ref[c, 1] = u[KC:, :] * bd_c
        return carry

    lax.fori_loop(0, NCH, gla_state_body, 0)

    if chained:
        for d, order in ((0, range(NCH)), (1, range(NCH - 1, -1, -1))):
            s_run = sG_ref[d]
            for idx, c in enumerate(order):
                u = uG_ref[c, d]
                uG_ref[c, d] = s_run
                if idx != NCH - 1:
                    tot_row = tot_ref[c][:, (1 + d) * LANES:(2 + d) * LANES]
                    s_run = _row_to_col(jnp.exp(tot_row), KC) * s_run + u

    def gla_out_body(c, carry):
        rows = pl.ds(pl.multiple_of(c * CH, CH), CH)
        q = pj_ref[rows, CQ:CQ + KC]
        k = pj_ref[rows, CK:CK + KC] * gla_scale
        v = pj_ref[rows, CV:CV + D_C]
        b_f, b_b = bf_ref[rows, :], bb_ref[rows, :]
        lower, upper = _causal_masks()
        qf, qb = q * jnp.exp(b_f), q * jnp.exp(b_b)
        kf16, kb16 = _bf(k * jnp.exp(-b_f)), _bf(k * jnp.exp(-b_b))
        ps = []
        for hh in range(H_C):
            s_f = _dot_nt(_bf(qf * hm_ck[hh]), kf16)
            s_b = _dot_nt(_bf(qb * hm_ck[hh]), kb16)
            ps.append(_bf(jnp.where(lower, s_f, 0.0) + jnp.where(upper, s_b, 0.0)))
        vst = jnp.concatenate([_bf(v * hm_cv[hh]) for hh in range(H_C)], axis=0)
        yc = _dot(jnp.concatenate(ps, axis=1), vst)
        if chained:
            s_in = jnp.concatenate([uG_ref[c, 0], uG_ref[c, 1]], axis=0)
            yc = yc + _dot(_bf(jnp.concatenate([qf, qb], axis=1)), _bf(s_in))
        y_ref[rows, D_A + D_B:D_MODEL] = yc
        return carry

    lax.fori_loop(0, NCH, gla_out_body, 0)

    pj_ref[:, 0:W_Z] = _dot(h_ref[...], win_ref[:, OFF_Z:OFF_Z + W_Z])
    def gate_body(c, carry):
        rows = pl.ds(pl.multiple_of(c * CH, CH), CH)
        parts = []
        for hh in range(H_A):
            ya = y_ref[rows, hh * DH_A:(hh + 1) * DH_A]
            parts.append(ya * lax.rsqrt(jnp.mean(ya * ya, axis=1, keepdims=True) + EPS))
        ybc = y_ref[rows, D_A:D_MODEL]
        seg = _bf(_block_diag_mask(D_B + D_C, D_B + D_C, DH_B, DH_B) * (1.0 / DH_B))
        parts.append(ybc * lax.rsqrt(_exact_dot_r(ybc * ybc, seg) + EPS))
        z = pj_ref[rows, 0:W_Z]
        yg = jnp.concatenate(parts, axis=1) * hng_ref[...] * (z * _sigmoid(z))
        h_ref[rows, :] = _bf(yg)
        return carry

    lax.fori_loop(0, NCH, gate_body, 0)
    pj_ref[:, 0:D_MODEL] = _dot(h_ref[...], wout_ref[...])

    def residual_body(c, carry):
        rows = pl.ds(pl.multiple_of(c * CH, CH), CH)
        xo = x_rows(x_ref, c)[...] + mod_ref[:, 2 * D_MODEL:] * pj_ref[rows, 0:D_MODEL]
        if final:
            xo = xo * lax.rsqrt(jnp.mean(xo * xo, axis=-1, keepdims=True) + EPS) * fg_ref[...]
        x_rows(out_ref, c)[...] = xo
        return carry

    lax.fori_loop(0, NCH, residual_body, 0)


def _layer_call(l, x, mod, norm_g, w_in_r, gate_b, lg_rows, w2_full, b2_full, hn_g, w_out_b, final_g,
                rot=None, states=None, final=False):
    batch, seqlen, d = x.shape
    chained = states is not None
    seqs = TOK // seqlen
    steps = batch // seqs
    assert seqs * seqlen == TOK and steps * seqs == batch and seqlen % CH == 0
    assert chained == (seqs == 1)
    assert chained or seqlen == CH
    per_batch_mod = mod.shape[0] > 1

    def const(shape):
        nd = len(shape)
        return pl.BlockSpec(shape, lambda i, _nd=nd: (0,) * _nd)

    def layer_block(shape):
        nd = len(shape)
        return pl.BlockSpec((None,) + shape, lambda i, _nd=nd: (l,) + (0,) * _nd)

    in_specs = [
        pl.BlockSpec((seqs, seqlen, d), lambda i: (i, 0, 0)),
        pl.BlockSpec((None, 1, 3 * d), (lambda i: (i, 0, 0)) if per_batch_mod else (lambda i: (0, 0, 0))),
        layer_block((1, d)),
        pl.BlockSpec((None, d, D_INR), lambda i: (l, 0, 0), pipeline_mode=pl.Buffered(1)),
        layer_block((1, LANES)),
        layer_block((2, D_B)),
        layer_block((LANES, 2 * LANES)),
        layer_block((1, 2 * LANES)),
        layer_block((1, d)),
        pl.BlockSpec((None, d, d), lambda i: (l, 0, 0), pipeline_mode=pl.Buffered(1)),
    ]
    args = [x, mod, norm_g.reshape(-1, 1, d), w_in_r, gate_b, lg_rows, w2_full, b2_full,
            hn_g.reshape(-1, 1, d), w_out_b]
    if final:
        in_specs.append(const((1, d)))
        args.append(final_g.reshape(1, d))
    if rot is not None:
        in_specs += [pl.BlockSpec((seqlen, D_B), lambda i: (0, 0), pipeline_mode=pl.Buffered(1))] * 2
        args += list(rot)
    if chained:
        s_c, s_n, s_m, s_r, s_g = states
        in_specs += [
            pl.BlockSpec((None, None, 2, H_A, DH_A, DH_A), lambda i: (i, l, 0, 0, 0, 0)),
            pl.BlockSpec((None, None, 2, H_A, DH_A), lambda i: (i, l, 0, 0, 0)),
            pl.BlockSpec((None, None, 1, LANES), lambda i: (i, l, 0, 0)),
            pl.BlockSpec((None, None, 2, D_B, D_B), lambda i: (i, l, 0, 0, 0)),
            pl.BlockSpec((None, None, 2, H_C * DK_C, D_C), lambda i: (i, l, 0, 0, 0)),
        ]
        args += [s_c, s_n, s_m, s_r, s_g]

    out_shape = [jax.ShapeDtypeStruct(x.shape, x.dtype)]
    out_specs = [pl.BlockSpec((seqs, seqlen, d), lambda i: (i, 0, 0))]
    if not chained:
        out_shape += [
            jax.ShapeDtypeStruct((batch, 2, H_A, DH_A, DH_A), F32),
            jax.ShapeDtypeStruct((batch, 2, H_A, DH_A), F32),
            jax.ShapeDtypeStruct((batch, 1, LANES), F32),
            jax.ShapeDtypeStruct((batch, 2, D_B, D_B), F32),
            jax.ShapeDtypeStruct((batch, 2, H_C * DK_C, D_C), F32),
        ]
        out_specs += [
            pl.BlockSpec((seqs, 2, H_A, DH_A, DH_A), lambda i: (i, 0, 0, 0, 0)),
            pl.BlockSpec((seqs, 2, H_A, DH_A), lambda i: (i, 0, 0, 0)),
            pl.BlockSpec((seqs, 1, LANES), lambda i: (i, 0, 0)),
            pl.BlockSpec((seqs, 2, D_B, D_B), lambda i: (i, 0, 0, 0)),
            pl.BlockSpec((seqs, 2, H_C * DK_C, D_C), lambda i: (i, 0, 0, 0)),
        ]

    scratch = [
        pltpu.VMEM((TOK, d), BF16),
        pltpu.VMEM((TOK, W_A), F32),
        pltpu.VMEM((TOK, LANES), F32),
        pltpu.VMEM((TOK, 2 * LANES), F32),
        pltpu.VMEM((TOK, LANES), F32),
        pltpu.VMEM((TOK, LANES), F32),
        pltpu.VMEM((TOK, LANES), F32),
        pltpu.VMEM((NCH, 1, 3 * LANES), F32),
        pltpu.VMEM((TOK, d), F32),
        pltpu.VMEM((NCH, 1, LANES), F32),
        pltpu.VMEM((NCH, 1, LANES), F32),
        pltpu.VMEM((H_B, CH, CH), F32),
        pltpu.VMEM((TOK, LANES), F32),
    ]
    if chained:
        scratch += [
            pltpu.VMEM((NCH, 2, H_A, DH_A, DH_A), F32),
            pltpu.VMEM((NCH, 2, H_A, DH_A), F32),
            pltpu.VMEM((NCH, 2, D_B, D_B), F32),
            pltpu.VMEM((NCH, 2, H_C * DK_C, D_C), F32),
        ]
    outs = pl.pallas_call(
        functools.partial(_layer_kernel, chained=chained, rotary=rot is not None, final=final),
        out_shape=out_shape,
        grid=(steps,),
        in_specs=in_specs,
        out_specs=out_specs,
        scratch_shapes=scratch,
        compiler_params=pltpu.CompilerParams(dimension_semantics=("arbitrary",), vmem_limit_bytes=VMEM_LIMIT),
        name=("latent" if chained else "context") + f"_layer{l}",
    )(*args)
    return outs


def _rotary_tables(seqlen):
    rows = seqlen // GRID_W
    r = jnp.repeat(jnp.arange(rows, dtype=F32), GRID_W)
    col = jnp.tile(jnp.arange(GRID_W, dtype=F32), rows)
    n_f = DH_B // 4
    freqs = ROPE_BASE ** (-jnp.arange(n_f, dtype=F32) / n_f)
    ang = jnp.concatenate([r[:, None] * freqs, col[:, None] * freqs], axis=-1)
    cos, sin = jnp.cos(ang), jnp.sin(ang)
    cos_l = jnp.tile(jnp.concatenate([cos, cos], axis=-1), (1, H_B))
    sin_l = jnp.tile(jnp.concatenate([-sin, sin], axis=-1), (1, H_B))
    return cos_l, sin_l


def _gate_lanes(m):
    z = jnp.zeros(m.shape[:-2] + (LANES,), m.dtype)
    z = z.at[..., COL_FF:COL_FF + H_A].set(m[..., 0, :]).at[..., COL_FB:COL_FB + H_A].set(m[..., 1, :])
    return z[..., None, :]


def _block_diag_heads(s):
    hn, a, b = s.shape[-3:]
    eye = jnp.eye(hn, dtype=s.dtype)
    out = s[..., :, :, None, :] * eye[:, None, :, None]
    return out.reshape(s.shape[:-3] + (hn * a, hn * b))


def _diag_blocks(s, hn):
    a, b = s.shape[-2] // hn, s.shape[-1] // hn
    s = s.reshape(s.shape[:-2] + (hn, a, hn, b))
    return jnp.stack([s[..., i, :, i, :] for i in range(hn)], axis=-3)


def kernel(x_prompt, x_sample, state_mlstm_C, state_mlstm_n, state_mlstm_m, state_ret, state_gla, c, c_ctx,
           norm_g, w_ada, b_ada, w_in, mlstm_gate_b, ret_decay_logit, gla_w2, gla_b2, headnorm_g, w_out, final_g):
    depth = w_in.shape[0]
    dec_batch = c.shape[0]

    parts = jnp.split(w_in, np.cumsum(IN_WIDTHS)[:-1].tolist(), axis=-1)
    (aq, ak, av, ao, az, ag, bq, bk, bv, bz, cq, ck, cv, cz, clr) = parts
    pad = jnp.zeros(w_in.shape[:2] + (W_S - ag.shape[-1] - clr.shape[-1],), w_in.dtype)
    w_in_r = jnp.concatenate([aq, ak, av, ao, bq, bk, bv, cq, ck, cv, az, bz, cz, ag, clr, pad], axis=-1).astype(BF16)
    w_out_b = w_out.astype(BF16)
    n_g = mlstm_gate_b.shape[-1]
    gate_b = jnp.pad(mlstm_gate_b, ((0, 0), (0, LANES - n_g)))[:, None, :]
    lg_rows = jnp.repeat(ret_decay_logit, DH_B, axis=-1)
    w2_full = jnp.zeros((depth, LANES, 2 * LANES), F32)
    w2_full = w2_full.at[:, n_g:n_g + GLA_RANK, 0:LANES].set(gla_w2[:, 0])
    w2_full = w2_full.at[:, n_g + GLA_RANK:n_g + 2 * GLA_RANK, LANES:].set(gla_w2[:, 1]).astype(BF16)
    b2_full = gla_b2.reshape(depth, 1, 2 * LANES)

    cstack = jnp.zeros((16, D_MODEL), F32).at[0:dec_batch].set(c).at[dec_batch].set(c_ctx)
    mods = _modulation(cstack, w_ada, b_ada)

    rot = _rotary_tables(x_sample.shape[1])
    s_m = _gate_lanes(state_mlstm_m)
    s_r = _block_diag_heads(state_ret)
    s_g = _block_diag_heads(state_gla)

    xp, xs = x_prompt, x_sample
    ctx = []
    for l in range(depth):
        final = l == depth - 1
        common = (norm_g, w_in_r, gate_b, lg_rows, w2_full, b2_full, headnorm_g, w_out_b, final_g)
        outs = _layer_call(l, xp, mods[l, dec_batch:dec_batch + 1][:, None, :], *common, final=final)
        xp = outs[0]
        ctx.append(outs[1:])
        xs = _layer_call(l, xs, mods[l, 0:dec_batch][:, None, :], *common, rot=rot,
                         states=(state_mlstm_C, state_mlstm_n, s_m, s_r, s_g), final=final)[0]

    new_c = jnp.stack([s[0] for s in ctx], axis=1)
    new_n = jnp.stack([s[1] for s in ctx], axis=1)
    m_l = jnp.stack([s[2][:, 0] for s in ctx], axis=1)
    new_m = jnp.stack([m_l[..., COL_FF:COL_FF + H_A], m_l[..., COL_FB:COL_FB + H_A]], axis=2)
    new_r = jnp.stack([_diag_blocks(s[3], H_B) for s in ctx], axis=1)
    new_g = jnp.stack([_diag_blocks(s[4], H_C) for s in ctx], axis=1)
    return (xp, xs, new_c, new_n, new_m, new_r, new_g)
```

```python
import functools

import jax
import jax.numpy as jnp
import numpy as np
from jax import lax
from jax.experimental import pallas as pl
from jax.experimental.pallas import tpu as pltpu

F32 = jnp.float32
BF16 = jnp.bfloat16

D_MODEL = 1024
H_A, DH_A = 4, 128
H_B, DH_B = 4, 64
H_C, DK_C, DV_C = 4, 32, 64
D_A, D_B, D_C = H_A * DH_A, H_B * DH_B, H_C * DV_C
GLA_RANK = 16
GLA_TAU = 16.0
GRID_W = 64
ROPE_BASE = 10000.0
EPS = 1e-6
IN_WIDTHS = (D_A, D_A, D_A, D_A, D_A, 4 * H_A, D_B, D_B, D_B, D_B, H_C * DK_C, H_C * DK_C, D_C, D_C, 2 * GLA_RANK)

LANES = 128
CH = 256
TOK = 1024
NCH = TOK // CH
VMEM_LIMIT = 60 * 1024 * 1024
NEG = -1e30

OFF_A, W_A = 0, 4 * D_A
OFF_B, W_B = OFF_A + W_A, 3 * D_B
OFF_C, W_C = OFF_B + W_B, 2 * H_C * DK_C + D_C
OFF_Z, W_Z = OFF_C + W_C, D_MODEL
OFF_S, W_S = OFF_Z + W_Z, LANES
D_INR = OFF_S + W_S
COL_FF, COL_FB = 4, 12


def _dot(a, b):
    return jnp.dot(a, b, preferred_element_type=F32)


def _dot_nt(a, b):
    return lax.dot_general(a, b, (((1,), (1,)), ((), ())), preferred_element_type=F32)


def _bf(x):
    return x.astype(BF16)


def _split3(x):
    hi = _bf(x)
    r1 = x - hi.astype(F32)
    mid = _bf(r1)
    lo = _bf(r1 - mid.astype(F32))
    return hi, mid, lo


def _exact_dot_l(m_bf, x):
    hi, mid, lo = _split3(x)
    return (_dot(m_bf, lo) + _dot(m_bf, mid)) + _dot(m_bf, hi)


def _exact_dot_r(x, m_bf):
    hi, mid, lo = _split3(x)
    return (_dot(lo, m_bf) + _dot(mid, m_bf)) + _dot(hi, m_bf)


def _log_sigmoid(x):
    return jnp.minimum(x, 0.0) - jnp.log(1.0 + jnp.exp(-jnp.abs(x)))


def _sigmoid(x):
    return 1.0 / (1.0 + jnp.exp(-x))


def _row_to_col(row, n):
    eye = lax.broadcasted_iota(jnp.int32, (n, n), 0) == lax.broadcasted_iota(jnp.int32, (n, n), 1)
    return jnp.sum(jnp.where(eye, row, 0.0), axis=1, keepdims=True)


def _lane_block_mask(width, block, h):
    lane = lax.broadcasted_iota(jnp.int32, (1, width), 1)
    return ((lane >= h * block) & (lane < (h + 1) * block)).astype(F32)


def _log2(n):
    assert n & (n - 1) == 0
    return n.bit_length() - 1


def _block_diag_mask(rows, cols, rblock, cblock):
    r = lax.broadcasted_iota(jnp.int32, (rows, cols), 0) >> _log2(rblock)
    c = lax.broadcasted_iota(jnp.int32, (rows, cols), 1) >> _log2(cblock)
    return (r == c).astype(F32)


def _causal_masks():
    ri = lax.broadcasted_iota(jnp.int32, (CH, CH), 0)
    cj = lax.broadcasted_iota(jnp.int32, (CH, CH), 1)
    return cj <= ri, cj >= ri


def _bwd_gate_lanes():
    lane = lax.broadcasted_iota(jnp.int32, (1, LANES), 1)
    return (lane >= COL_FB) & (lane < COL_FB + H_A)


def _pieces(x):
    return jnp.concatenate(_split3(x), axis=1)


def _cum_max(x, reverse):
    n = x.shape[0]
    row = lax.broadcasted_iota(jnp.int32, x.shape, 0)
    s = 1
    while s < n:
        if reverse:
            shifted = jnp.where(row < n - s, pltpu.roll(x, n - s, 0), NEG)
        else:
            shifted = jnp.where(row >= s, pltpu.roll(x, s, 0), NEG)
        x = jnp.maximum(x, shifted)
        s *= 2
    return x


GATE_BLOCK = 16
N_PIECES = 3


def _gate_constants():
    half = N_PIECES * GATE_BLOCK
    cols = [COL_FF + h for h in range(H_A)] + [COL_FB + h for h in range(H_A)]
    sp = np.zeros((2 * N_PIECES * LANES, LANES), np.float32)
    for part in range(2):
        for p in range(N_PIECES):
            for k in cols:
                sp[(part * N_PIECES + p) * LANES + k, part * half + p * GATE_BLOCK + k] = 1.0
    lane = np.arange(LANES)
    sel = np.zeros((2 * H_A, LANES, 2 * DH_A), np.float32)
    gl = np.zeros((16, LANES), np.float32)
    for hd, k in enumerate(cols):
        mine = (lane % GATE_BLOCK == k) & (lane < 2 * half)
        gl[hd] = mine
        sel[hd, mine & (lane < half), 0:DH_A] = 1.0
        sel[hd, mine & (lane >= half), DH_A:] = 1.0
    valid = np.isin(lane % GATE_BLOCK, cols)
    gl[2 * H_A] = valid & (lane < half)
    gl[2 * H_A + 1] = valid & (lane >= half) & (lane < 2 * half)
    return jnp.asarray(sp, BF16), jnp.asarray(sel, BF16), jnp.asarray(gl, F32)


def _modulation_kernel(c_ref, w_ref, b_ref, o_ref):
    cv = c_ref[...]
    s = cv * _sigmoid(cv)
    o_ref[...] = _dot(_bf(s), _bf(w_ref[...])) + b_ref[...]


def _modulation(cstack, w_ada, b_ada):
    depth, d, d3 = w_ada.shape
    rows = cstack.shape[0]
    tn = 512
    return pl.pallas_call(
        _modulation_kernel,
        out_shape=jax.ShapeDtypeStruct((depth, rows, d3), F32),
        grid=(depth, d3 // tn),
        in_specs=[
            pl.BlockSpec((rows, d), lambda l, j: (0, 0)),
            pl.BlockSpec((None, d, tn), lambda l, j: (l, 0, j)),
            pl.BlockSpec((None, 1, tn), lambda l, j: (l, 0, j)),
        ],
        out_specs=pl.BlockSpec((None, rows, tn), lambda l, j: (l, 0, j)),
        name="adaln_modulation",
    )(cstack, w_ada, b_ada.reshape(depth, 1, d3))


def _layer_kernel(*refs, chained, rotary, final):
    it = iter(refs)
    x_ref, mod_ref, ng_ref, win_ref, gb_ref, lg_ref, w2_ref, b2_ref, hng_ref, wout_ref = (next(it) for _ in range(10))
    sp_ref, sel_ref, gl_ref = (next(it) for _ in range(3))
    fg_ref = next(it) if final else None
    if rotary:
        cos_ref, sin_ref = next(it), next(it)
    if chained:
        sC_ref, sn_ref, sm_ref_in, sR_ref, sG_ref = (next(it) for _ in range(5))
    out_ref = next(it)
    if not chained:
        oC_ref, on_ref, om_ref, oR_ref, oG_ref = (next(it) for _ in range(5))
    (h_ref, pj_ref, sm_ref, fc_ref, bf_ref, bb_ref, tot_ref, y_ref,
     gm_ref, min_ref, cm_ref, bop_ref) = (next(it) for _ in range(12))
    la_ref = pj_ref.at[:, 0:2 * LANES]
    dec_ref = pj_ref.at[:, W_B:W_B + CH]
    if chained:
        uCN_ref, uR_ref, uG_ref = (next(it) for _ in range(3))
    else:
        uR_ref, uG_ref = oR_ref, oG_ref

    seqs, seqlen, _ = x_ref.shape

    def x_rows(ref, c):
        r0 = pl.multiple_of(c * CH, CH)
        return ref.at[r0 >> _log2(seqlen), pl.ds(pl.multiple_of(r0 & (seqlen - 1), CH), CH), :]

    def norm_body(c, carry):
        rows = pl.ds(pl.multiple_of(c * CH, CH), CH)
        x = x_rows(x_ref, c)[...]
        shift, scale = mod_ref[:, 0:D_MODEL], mod_ref[:, D_MODEL:2 * D_MODEL]
        xn = x * lax.rsqrt(jnp.mean(x * x, axis=-1, keepdims=True) + EPS) * ng_ref[...]
        h_ref[rows, :] = _bf(xn * (1.0 + scale) + shift)
        return carry

    lax.fori_loop(0, NCH, norm_body, 0)

    sm_ref[...] = _dot(h_ref[...], win_ref[:, OFF_S:OFF_S + W_S])

    def la_body(c, carry):
        rows = pl.ds(pl.multiple_of(c * CH, CH), CH)
        la_ref[rows, :] = _log_sigmoid(_dot(_bf(sm_ref[rows, :]), w2_ref[...]) + b2_ref[...]) * (1.0 / GLA_TAU)
        return carry

    lax.fori_loop(0, NCH, la_body, 0)

    def prefix_body(c, carry):
        rows = pl.ds(pl.multiple_of(c * CH, CH), CH)
        bwd_cols = _bwd_gate_lanes()
        tri = _bf(_causal_masks()[0].astype(F32))
        ls = _log_sigmoid(sm_ref[rows, :] + gb_ref[...])
        la = la_ref[rows, :]
        xcat = jnp.concatenate([ls, la], axis=1)
        ps = _exact_dot_l(tri, xcat)
        tot = ps[CH - 1:CH, :]
        fc_ref[rows, :] = jnp.where(bwd_cols, tot[:, 0:LANES] - ps[:, 0:LANES] + ls, ps[:, 0:LANES])
        bf_ref[rows, :] = ps[:, LANES:2 * LANES]
        bb_ref[rows, :] = tot[:, 2 * LANES:] - ps[:, 2 * LANES:] + la[:, LANES:]
        tot_ref[c] = tot
        return carry

    lax.fori_loop(0, NCH, prefix_body, 0)

    pj_ref[:, 0:W_A] = _dot(h_ref[...], win_ref[:, OFF_A:OFF_A + W_A])
    AQ, AK, AV, AO = 0, D_A, 2 * D_A, 3 * D_A
    k_scale = DH_A ** -0.5

    ones_cols = jnp.ones((CH, DH_A), F32)

    def mlstm_state_body(c, carry):
        rows = pl.ds(pl.multiple_of(c * CH, CH), CH)
        g_pre = sm_ref[rows, :] + gb_ref[...]
        rw = pltpu.roll(g_pre, 4, 1) - fc_ref[rows, :]
        cm_ref[rows, :] = jnp.where(_bwd_gate_lanes(), _cum_max(rw, True), _cum_max(rw, False))
        bop_ref[rows, :] = _bf(_dot(_pieces(rw), sp_ref[3 * LANES:, :]) + gl_ref[2 * H_A:2 * H_A + 1, :])
        totg = tot_ref[c][:, 0:LANES]
        g = totg + rw
        gm = jnp.max(g, axis=0, keepdims=True)
        gm_ref[c] = gm
        wgt = jnp.exp(g - gm).T
        if not chained:
            m_new = jnp.maximum(totg, gm)
            b_row = jnp.exp(gm - m_new)
            om_ref[c] = m_new
            min_ref[c] = jnp.zeros((1, LANES), F32)
        for hh in range(H_A):
            kt = (pj_ref[rows, AK + hh * DH_A:AK + (hh + 1) * DH_A] * k_scale).T
            v_h = pj_ref[rows, AV + hh * DH_A:AV + (hh + 1) * DH_A]
            lhs = jnp.concatenate([kt * wgt[COL_FF + hh:COL_FF + hh + 1, :],
                                   kt * wgt[COL_FB + hh:COL_FB + hh + 1, :]], axis=0)
            u = _dot(_bf(lhs), _bf(jnp.concatenate([v_h, ones_cols], axis=1)))
            for d, col in ((0, COL_FF), (1, COL_FB)):
                u_d = u[d * DH_A:(d + 1) * DH_A, :]
                if chained:
                    uCN_ref[c, d, hh] = u_d
                else:
                    fin = b_row[:, col + hh:col + hh + 1] * u_d
                    oC_ref[c, d, hh] = fin[:, 0:DH_A]
                    on_ref[c, d, hh:hh + 1, :] = fin[:, DH_A:].T[0:1, :]
        return carry

    lax.fori_loop(0, NCH, mlstm_state_body, 0)

    if chained:
        for d, order, col in ((0, range(NCH), COL_FF), (1, range(NCH - 1, -1, -1), COL_FB)):
            m_run = sm_ref_in[...]
            cn_run = [jnp.concatenate([sC_ref[d, hh],
                                       jnp.broadcast_to(sn_ref[d, hh:hh + 1, :], (DH_A, DH_A)).T], axis=1)
                      for hh in range(H_A)]
            for idx, c in enumerate(order):
                last = idx == NCH - 1
                if not last:
                    totg, gm = tot_ref[c][:, 0:LANES], gm_ref[c]
                    m_new = jnp.maximum(totg + m_run, gm)
                    a_row, b_row = jnp.exp(totg + m_run - m_new), jnp.exp(gm - m_new)
                for hh in range(H_A):
                    u = uCN_ref[c, d, hh]
                    uCN_ref[c, d, hh] = cn_run[hh]
                    if not last:
                        cn_run[hh] = (a_row[:, col + hh:col + hh + 1] * cn_run[hh]
                                      + b_row[:, col + hh:col + hh + 1] * u)
                min_ref[c] = m_run if d == 0 else jnp.where(_bwd_gate_lanes(), m_run, min_ref[c])
                if not last:
                    m_run = m_new

    def mlstm_out_body(c, carry):
        rows = pl.ds(pl.multiple_of(c * CH, CH), CH)
        lower, upper = _causal_masks()
        m_prev = min_ref[c]
        cmx = jnp.maximum(cm_ref[rows, :], m_prev)
        mi = fc_ref[rows, :] + cmx
        a_op = _bf(_dot(_pieces(-cmx), sp_ref[0:3 * LANES, :]) + gl_ref[2 * H_A + 1:2 * H_A + 2, :])
        b_op = bop_ref[rows, :]
        pk = _bf(_dot(jnp.concatenate([_pieces(mi), _pieces(m_prev - cmx)], axis=1), sp_ref[...]))
        for hh in range(H_A):
            q16 = _bf(pj_ref[rows, AQ + hh * DH_A:AQ + (hh + 1) * DH_A])
            k_h = pj_ref[rows, AK + hh * DH_A:AK + (hh + 1) * DH_A] * k_scale
            v_h = pj_ref[rows, AV + hh * DH_A:AV + (hh + 1) * DH_A]
            o_h = pj_ref[rows, AO + hh * DH_A:AO + (hh + 1) * DH_A]
            s = _dot_nt(q16, _bf(k_h))
            qs = []
            for d, msk in ((0, lower), (1, upper)):
                hd = d * H_A + hh
                dm = _dot_nt(a_op * _bf(gl_ref[hd:hd + 1, :]), b_op)
                qs.append(_bf(s * jnp.exp(jnp.where(msk, dm, NEG))))
            nd = _dot(jnp.concatenate(qs, axis=0), _bf(jnp.concatenate([v_h, ones_cols], axis=1)))
            hs = None
            for d in range(2):
                hd = d * H_A + hh
                num, den = nd[d * CH:(d + 1) * CH, 0:DH_A], nd[d * CH:(d + 1) * CH, DH_A:]
                rep = _dot(pk, sel_ref[hd])
                if chained:
                    wa = jnp.exp(rep[:, DH_A:])
                    inter = _dot(q16, _bf(uCN_ref[c, d, hh]))
                    num = num + wa * inter[:, 0:DH_A]
                    den = den + wa * inter[:, DH_A:]
                part = num / jnp.maximum(jnp.abs(den), jnp.exp(-rep[:, 0:DH_A]))
                hs = part if hs is None else hs + part
            y_ref[rows, hh * DH_A:(hh + 1) * DH_A] = _sigmoid(o_h) * hs
        return carry

    lax.fori_loop(0, NCH, mlstm_out_body, 0)

    pj_ref[:, 0:W_B] = _dot(h_ref[...], win_ref[:, OFF_B:OFF_B + W_B])
    BQ, BK, BV = 0, D_B, 2 * D_B
    if rotary:
        def rotary_body(c, carry):
            rows = pl.ds(pl.multiple_of(c * CH, CH), CH)
            first_half = (lax.broadcasted_iota(jnp.int32, (1, D_B), 1) & (DH_B - 1)) < DH_B // 2
            for off in (BQ, BK):
                t = pj_ref[rows, off:off + D_B]
                partner = jnp.where(first_half, pltpu.roll(t, D_B - DH_B // 2, 1), pltpu.roll(t, DH_B // 2, 1))
                pj_ref[rows, off:off + D_B] = t * cos_ref[rows, :] + partner * sin_ref[rows, :]
            return carry

        lax.fori_loop(0, NCH, rotary_body, 0)

    lgam = _log_sigmoid(lg_ref[...])
    lg_f, lg_b = lgam[0:1, :], lgam[1:2, :]
    pos = lax.broadcasted_iota(jnp.int32, (CH, 1), 0).astype(F32)
    lower, upper = _causal_masks()
    rel = (lax.broadcasted_iota(jnp.int32, (CH, CH), 0) - lax.broadcasted_iota(jnp.int32, (CH, CH), 1)).astype(F32)
    for hh in range(H_B):
        lf = lg_f[:, hh * DH_B:hh * DH_B + 1]
        lb = lg_b[:, hh * DH_B:hh * DH_B + 1]
        dec_ref[hh * CH:(hh + 1) * CH, :] = (jnp.where(lower, jnp.exp(jnp.maximum(rel, 0.0) * lf), 0.0)
                                             + jnp.where(upper, jnp.exp(jnp.maximum(-rel, 0.0) * lb), 0.0))
    bd_b = _block_diag_mask(D_B, D_B, DH_B, DH_B)
    hm_b = [_lane_block_mask(D_B, DH_B, hh) for hh in range(H_B)]
    ret_scale = DH_B ** -0.5

    def ret_state_body(c, carry):
        rows = pl.ds(pl.multiple_of(c * CH, CH), CH)
        k = pj_ref[rows, BK:BK + D_B] * ret_scale
        v = pj_ref[rows, BV:BV + D_B]
        kf = k * jnp.exp((CH - 1.0 - pos) * lg_f)
        kb = k * jnp.exp(pos * lg_b)
        u = _dot(_bf(jnp.concatenate([kf, kb], axis=1).T), _bf(v))
        uR_ref[c, 0] = u[0:D_B, :] * bd_b
        uR_ref[c, 1] = u[D_B:, :] * bd_b
        return carry

    lax.fori_loop(0, NCH, ret_state_body, 0)

    if chained:
        for d, order, lg_row in ((0, range(NCH), lg_f), (1, range(NCH - 1, -1, -1), lg_b)):
            g_col = _row_to_col(jnp.exp(float(CH) * lg_row), D_B)
            s_run = sR_ref[d]
            for idx, c in enumerate(order):
                u = uR_ref[c, d]
                uR_ref[c, d] = s_run
                if idx != NCH - 1:
                    s_run = g_col * s_run + u

    def ret_out_body(c, carry):
        rows = pl.ds(pl.multiple_of(c * CH, CH), CH)
        q = pj_ref[rows, BQ:BQ + D_B]
        k = pj_ref[rows, BK:BK + D_B] * ret_scale
        v = pj_ref[rows, BV:BV + D_B]
        kb16 = _bf(k)
        ps = [_bf(_dot_nt(_bf(q * hm_b[hh]), kb16) * dec_ref[hh * CH:(hh + 1) * CH, :]) for hh in range(H_B)]
        vst = jnp.concatenate([_bf(v * hm_b[hh]) for hh in range(H_B)], axis=0)
        yb = _dot(jnp.concatenate(ps, axis=1), vst)
        if chained:
            qf = q * jnp.exp((pos + 1.0) * lg_f)
            qb = q * jnp.exp((float(CH) - pos) * lg_b)
            s_in = jnp.concatenate([uR_ref[c, 0], uR_ref[c, 1]], axis=0)
            yb = yb + _dot(_bf(jnp.concatenate([qf, qb], axis=1)), _bf(s_in))
        y_ref[rows, D_A:D_A + D_B] = yb
        return carry

    lax.fori_loop(0, NCH, ret_out_body, 0)

    pj_ref[:, 0:W_C] = _dot(h_ref[...], win_ref[:, OFF_C:OFF_C + W_C])
    KC = H_C * DK_C
    CQ, CK, CV = 0, KC, 2 * KC
    gla_scale = DK_C ** -0.5
    bd_c = _block_diag_mask(KC, D_C, DK_C, DV_C)
    hm_ck = [_lane_block_mask(KC, DK_C, hh) for hh in range(H_C)]
    hm_cv = [_lane_block_mask(D_C, DV_C, hh) for hh in range(H_C)]

    def gla_state_body(c, carry):
        rows = pl.ds(pl.multiple_of(c * CH, CH), CH)
        k = pj_ref[rows, CK:CK + KC] * gla_scale
        v = pj_ref[rows, CV:CV + D_C]
        tot = tot_ref[c]
        khf = k * jnp.exp(tot[:, LANES:2 * LANES] - bf_ref[rows, :])
        khb = k * jnp.exp(tot[:, 2 * LANES:] - bb_ref[rows, :])
        u = _dot(_bf(jnp.concatenate([khf, khb], axis=1).T), _bf(v))
        uG_ref[c, 0] = u[0:KC, :] * bd_c
        uG_ref[c, 1] = u[KC:, :] * bd_c
        return carry

    lax.fori_loop(0, NCH, gla_state_body, 0)

    if chained:
        for d, order in ((0, range(NCH)), (1, range(NCH - 1, -1, -1))):
            s_run = sG_ref[d]
            for idx, c in enumerate(order):
                u = uG_ref[c, d]
                uG_ref[c, d] = s_run
                if idx != NCH - 1:
                    tot_row = tot_ref[c][:, (1 + d) * LANES:(2 + d) * LANES]
                    s_run = _row_to_col(jnp.exp(tot_row), KC) * s_run + u

    def gla_out_body(c, carry):
        rows = pl.ds(pl.multiple_of(c * CH, CH), CH)
        q = pj_ref[rows, CQ:CQ + KC]
        k = pj_ref[rows, CK:CK + KC] * gla_scale
        v = pj_ref[rows, CV:CV + D_C]
        b_f, b_b = bf_ref[rows, :], bb_ref[rows, :]
        lower, upper = _causal_masks()
        qf, qb = q * jnp.exp(b_f), q * jnp.exp(b_b)
        kf16, kb16 = _bf(k * jnp.exp(-b_f)), _bf(k * jnp.exp(-b_b))
        ps = []
        for hh in range(H_C):
            s_f = _dot_nt(_bf(qf * hm_ck[hh]), kf16)
            s_b = _dot_nt(_bf(qb * hm_ck[hh]), kb16)
            ps.append(_bf(jnp.where(lower, s_f, 0.0) + jnp.where(upper, s_b, 0.0)))
        vst = jnp.concatenate([_bf(v * hm_cv[hh]) for hh in range(H_C)], axis=0)
        yc = _dot(jnp.concatenate(ps, axis=1), vst)
        if chained:
            s_in = jnp.concatenate([uG_ref[c, 0], uG_ref[c, 1]], axis=0)
            yc = yc + _dot(_bf(jnp.concatenate([qf, qb], axis=1)), _bf(s_in))
        y_ref[rows, D_A + D_B:D_MODEL] = yc
        return carry

    lax.fori_loop(0, NCH, gla_out_body, 0)

    pj_ref[:, 0:W_Z] = _dot(h_ref[...], win_ref[:, OFF_Z:OFF_Z + W_Z])
    def gate_body(c, carry):
        rows = pl.ds(pl.multiple_of(c * CH, CH), CH)
        parts = []
        for hh in range(H_A):
            ya = y_ref[rows, hh * DH_A:(hh + 1) * DH_A]
            parts.append(ya * lax.rsqrt(jnp.mean(ya * ya, axis=1, keepdims=True) + EPS))
        ybc = y_ref[rows, D_A:D_MODEL]
        seg = _bf(_block_diag_mask(D_B + D_C, D_B + D_C, DH_B, DH_B) * (1.0 / DH_B))
        parts.append(ybc * lax.rsqrt(_exact_dot_r(ybc * ybc, seg) + EPS))
        z = pj_ref[rows, 0:W_Z]
        yg = jnp.concatenate(parts, axis=1) * hng_ref[...] * (z * _sigmoid(z))
        h_ref[rows, :] = _bf(yg)
        return carry

    lax.fori_loop(0, NCH, gate_body, 0)
    pj_ref[:, 0:D_MODEL] = _dot(h_ref[...], wout_ref[...])

    def residual_body(c, carry):
        rows = pl.ds(pl.multiple_of(c * CH, CH), CH)
        xo = x_rows(x_ref, c)[...] + mod_ref[:, 2 * D_MODEL:] * pj_ref[rows, 0:D_MODEL]
        if final:
            xo = xo * lax.rsqrt(jnp.mean(xo * xo, axis=-1, keepdims=True) + EPS) * fg_ref[...]
        x_rows(out_ref, c)[...] = xo
        return carry

    lax.fori_loop(0, NCH, residual_body, 0)


def _layer_call(l, x, mod, norm_g, w_in_r, gate_b, lg_rows, w2_full, b2_full, hn_g, w_out_b, final_g,
                rot=None, states=None, final=False):
    batch, seqlen, d = x.shape
    chained = states is not None
    seqs = TOK // seqlen
    steps = batch // seqs
    assert seqs * seqlen == TOK and steps * seqs == batch and seqlen % CH == 0
    assert chained == (seqs == 1)
    assert chained or seqlen == CH
    per_batch_mod = mod.shape[0] > 1

    def const(shape):
        nd = len(shape)
        return pl.BlockSpec(shape, lambda i, _nd=nd: (0,) * _nd)

    def layer_block(shape):
        nd = len(shape)
        return pl.BlockSpec((None,) + shape, lambda i, _nd=nd: (l,) + (0,) * _nd)

    in_specs = [
        pl.BlockSpec((seqs, seqlen, d), lambda i: (i, 0, 0)),
        pl.BlockSpec((None, 1, 3 * d), (lambda i: (i, 0, 0)) if per_batch_mod else (lambda i: (0, 0, 0))),
        layer_block((1, d)),
        pl.BlockSpec((None, d, D_INR), lambda i: (l, 0, 0), pipeline_mode=pl.Buffered(1)),
        layer_block((1, LANES)),
        layer_block((2, D_B)),
        layer_block((LANES, 2 * LANES)),
        layer_block((1, 2 * LANES)),
        layer_block((1, d)),
        pl.BlockSpec((None, d, d), lambda i: (l, 0, 0), pipeline_mode=pl.Buffered(1)),
    ]
    args = [x, mod, norm_g.reshape(-1, 1, d), w_in_r, gate_b, lg_rows, w2_full, b2_full,
            hn_g.reshape(-1, 1, d), w_out_b]
    for cst in _gate_constants():
        in_specs.append(pl.BlockSpec(cst.shape, lambda i, _nd=cst.ndim: (0,) * _nd, pipeline_mode=pl.Buffered(1)))
        args.append(cst)
    if final:
        in_specs.append(const((1, d)))
        args.append(final_g.reshape(1, d))
    if rot is not None:
        in_specs += [pl.BlockSpec((seqlen, D_B), lambda i: (0, 0), pipeline_mode=pl.Buffered(1))] * 2
        args += list(rot)
    if chained:
        s_c, s_n, s_m, s_r, s_g = states
        in_specs += [
            pl.BlockSpec((None, None, 2, H_A, DH_A, DH_A), lambda i: (i, l, 0, 0, 0, 0)),
            pl.BlockSpec((None, None, 2, H_A, DH_A), lambda i: (i, l, 0, 0, 0)),
            pl.BlockSpec((None, None, 1, LANES), lambda i: (i, l, 0, 0)),
            pl.BlockSpec((None, None, 2, D_B, D_B), lambda i: (i, l, 0, 0, 0)),
            pl.BlockSpec((None, None, 2, H_C * DK_C, D_C), lambda i: (i, l, 0, 0, 0)),
        ]
        args += [s_c, s_n, s_m, s_r, s_g]

    out_shape = [jax.ShapeDtypeStruct(x.shape, x.dtype)]
    out_specs = [pl.BlockSpec((seqs, seqlen, d), lambda i: (i, 0, 0))]
    if not chained:
        out_shape += [
            jax.ShapeDtypeStruct((batch, 2, H_A, DH_A, DH_A), F32),
            jax.ShapeDtypeStruct((batch, 2, H_A, DH_A), F32),
            jax.ShapeDtypeStruct((batch, 1, LANES), F32),
            jax.ShapeDtypeStruct((batch, 2, D_B, D_B), F32),
            jax.ShapeDtypeStruct((batch, 2, H_C * DK_C, D_C), F32),
        ]
        out_specs += [
            pl.BlockSpec((seqs, 2, H_A, DH_A, DH_A), lambda i: (i, 0, 0, 0, 0)),
            pl.BlockSpec((seqs, 2, H_A, DH_A), lambda i: (i, 0, 0, 0)),
            pl.BlockSpec((seqs, 1, LANES), lambda i: (i, 0, 0)),
            pl.BlockSpec((seqs, 2, D_B, D_B), lambda i: (i, 0, 0, 0)),
            pl.BlockSpec((seqs, 2, H_C * DK_C, D_C), lambda i: (i, 0, 0, 0)),
        ]

    scratch = [
        pltpu.VMEM((TOK, d), BF16),
        pltpu.VMEM((TOK, W_A), F32),
        pltpu.VMEM((TOK, LANES), F32),
        pltpu.VMEM((TOK, LANES), F32),
        pltpu.VMEM((TOK, LANES), F32),
        pltpu.VMEM((TOK, LANES), F32),
        pltpu.VMEM((NCH, 1, 3 * LANES), F32),
        pltpu.VMEM((TOK, d), F32),
        pltpu.VMEM((NCH, 1, LANES), F32),
        pltpu.VMEM((NCH, 1, LANES), F32),
        pltpu.VMEM((TOK, LANES), F32),
        pltpu.VMEM((TOK, LANES), BF16),
    ]
    if chained:
        scratch += [
            pltpu.VMEM((NCH, 2, H_A, DH_A, 2 * DH_A), F32),
            pltpu.VMEM((NCH, 2, D_B, D_B), F32),
            pltpu.VMEM((NCH, 2, H_C * DK_C, D_C), F32),
        ]
    outs = pl.pallas_call(
        functools.partial(_layer_kernel, chained=chained, rotary=rot is not None, final=final),
        out_shape=out_shape,
        grid=(steps,),
        in_specs=in_specs,
        out_specs=out_specs,
        scratch_shapes=scratch,
        compiler_params=pltpu.CompilerParams(dimension_semantics=("arbitrary",), vmem_limit_bytes=VMEM_LIMIT),
        name=("latent" if chained else "context") + f"_layer{l}",
    )(*args)
    return outs


def _rotary_tables(seqlen):
    rows = seqlen // GRID_W
    r = jnp.repeat(jnp.arange(rows, dtype=F32), GRID_W)
    col = jnp.tile(jnp.arange(GRID_W, dtype=F32), rows)
    n_f = DH_B // 4
    freqs = ROPE_BASE ** (-jnp.arange(n_f, dtype=F32) / n_f)
    ang = jnp.concatenate([r[:, None] * freqs, col[:, None] * freqs], axis=-1)
    cos, sin = jnp.cos(ang), jnp.sin(ang)
    cos_l = jnp.tile(jnp.concatenate([cos, cos], axis=-1), (1, H_B))
    sin_l = jnp.tile(jnp.concatenate([-sin, sin], axis=-1), (1, H_B))
    return cos_l, sin_l


def _gate_lanes(m):
    z = jnp.zeros(m.shape[:-2] + (LANES,), m.dtype)
    z = z.at[..., COL_FF:COL_FF + H_A].set(m[..., 0, :]).at[..., COL_FB:COL_FB + H_A].set(m[..., 1, :])
    return z[..., None, :]


def _block_diag_heads(s):
    hn, a, b = s.shape[-3:]
    eye = jnp.eye(hn, dtype=s.dtype)
    out = s[..., :, :, None, :] * eye[:, None, :, None]
    return out.reshape(s.shape[:-3] + (hn * a, hn * b))


def _diag_blocks(s, hn):
    a, b = s.shape[-2] // hn, s.shape[-1] // hn
    s = s.reshape(s.shape[:-2] + (hn, a, hn, b))
    return jnp.stack([s[..., i, :, i, :] for i in range(hn)], axis=-3)


def kernel(x_prompt, x_sample, state_mlstm_C, state_mlstm_n, state_mlstm_m, state_ret, state_gla, c, c_ctx,
           norm_g, w_ada, b_ada, w_in, mlstm_gate_b, ret_decay_logit, gla_w2, gla_b2, headnorm_g, w_out, final_g):
    depth = w_in.shape[0]
    dec_batch = c.shape[0]

    parts = jnp.split(w_in, np.cumsum(IN_WIDTHS)[:-1].tolist(), axis=-1)
    (aq, ak, av, ao, az, ag, bq, bk, bv, bz, cq, ck, cv, cz, clr) = parts
    pad = jnp.zeros(w_in.shape[:2] + (W_S - ag.shape[-1] - clr.shape[-1],), w_in.dtype)
    w_in_r = jnp.concatenate([aq, ak, av, ao, bq, bk, bv, cq, ck, cv, az, bz, cz, ag, clr, pad], axis=-1).astype(BF16)
    w_out_b = w_out.astype(BF16)
    n_g = mlstm_gate_b.shape[-1]
    gate_b = jnp.pad(mlstm_gate_b, ((0, 0), (0, LANES - n_g)))[:, None, :]
    lg_rows = jnp.repeat(ret_decay_logit, DH_B, axis=-1)
    w2_full = jnp.zeros((depth, LANES, 2 * LANES), F32)
    w2_full = w2_full.at[:, n_g:n_g + GLA_RANK, 0:LANES].set(gla_w2[:, 0])
    w2_full = w2_full.at[:, n_g + GLA_RANK:n_g + 2 * GLA_RANK, LANES:].set(gla_w2[:, 1]).astype(BF16)
    b2_full = gla_b2.reshape(depth, 1, 2 * LANES)

    cstack = jnp.zeros((16, D_MODEL), F32).at[0:dec_batch].set(c).at[dec_batch].set(c_ctx)
    mods = _modulation(cstack, w_ada, b_ada)

    rot = _rotary_tables(x_sample.shape[1])
    s_m = _gate_lanes(state_mlstm_m)
    s_r = _block_diag_heads(state_ret)
    s_g = _block_diag_heads(state_gla)

    xp, xs = x_prompt, x_sample
    ctx = []
    for l in range(depth):
        final = l == depth - 1
        common = (norm_g, w_in_r, gate_b, lg_rows, w2_full, b2_full, headnorm_g, w_out_b, final_g)
        outs = _layer_call(l, xp, mods[l, dec_batch:dec_batch + 1][:, None, :], *common, final=final)
        xp = outs[0]
        ctx.append(outs[1:])
        xs = _layer_call(l, xs, mods[l, 0:dec_batch][:, None, :], *common, rot=rot,
                         states=(state_mlstm_C, state_mlstm_n, s_m, s_r, s_g), final=final)[0]

    new_c = jnp.stack([s[0] for s in ctx], axis=1)
    new_n = jnp.stack([s[1] for s in ctx], axis=1)
    m_l = jnp.stack([s[2][:, 0] for s in ctx], axis=1)
    new_m = jnp.stack([m_l[..., COL_FF:COL_FF + H_A], m_l[..., COL_FB:COL_FB + H_A]], axis=2)
    new_r = jnp.stack([_diag_blocks(s[3], H_B) for s in ctx], axis=1)
    new_g = jnp.stack([_diag_blocks(s[4], H_C) for s in ctx], axis=1)
    return (xp, xs, new_c, new_n, new_m, new_r, new_g)
```

```python
import functools

import jax
import jax.numpy as jnp
import numpy as np
from jax import lax
from jax.experimental import pallas as pl
from jax.experimental.pallas import tpu as pltpu

F32 = jnp.float32
BF16 = jnp.bfloat16

D_MODEL = 1024
H_A, DH_A = 4, 128
H_B, DH_B = 4, 64
H_C, DK_C, DV_C = 4, 32, 64
D_A, D_B, D_C = H_A * DH_A, H_B * DH_B, H_C * DV_C
GLA_RANK = 16
GLA_TAU = 16.0
GRID_W = 64
ROPE_BASE = 10000.0
EPS = 1e-6
IN_WIDTHS = (D_A, D_A, D_A, D_A, D_A, 4 * H_A, D_B, D_B, D_B, D_B, H_C * DK_C, H_C * DK_C, D_C, D_C, 2 * GLA_RANK)

LANES = 128
CH = 256
TOK = 1024
NCH = TOK // CH
VMEM_LIMIT = 60 * 1024 * 1024
NEG = -1e30

OFF_A, W_A = 0, 4 * D_A
OFF_B, W_B = OFF_A + W_A, 3 * D_B
OFF_C, W_C = OFF_B + W_B, 2 * H_C * DK_C + D_C
OFF_Z, W_Z = OFF_C + W_C, D_MODEL
OFF_S, W_S = OFF_Z + W_Z, LANES
D_INR = OFF_S + W_S
COL_FF, COL_FB = 4, 12


def _dot(a, b):
    return jnp.dot(a, b, preferred_element_type=F32)


def _dot_nt(a, b):
    return lax.dot_general(a, b, (((1,), (1,)), ((), ())), preferred_element_type=F32)


def _bf(x):
    return x.astype(BF16)


def _split3(x):
    hi = _bf(x)
    r1 = x - hi.astype(F32)
    mid = _bf(r1)
    lo = _bf(r1 - mid.astype(F32))
    return hi, mid, lo


def _exact_dot_l(m_bf, x):
    hi, mid, lo = _split3(x)
    return (_dot(m_bf, lo) + _dot(m_bf, mid)) + _dot(m_bf, hi)


def _exact_dot_r(x, m_bf):
    hi, mid, lo = _split3(x)
    return (_dot(lo, m_bf) + _dot(mid, m_bf)) + _dot(hi, m_bf)


def _log_sigmoid(x):
    return jnp.minimum(x, 0.0) - jnp.log(1.0 + jnp.exp(-jnp.abs(x)))


def _sigmoid(x):
    return 1.0 / (1.0 + jnp.exp(-x))


def _row_to_col(row, n):
    eye = lax.broadcasted_iota(jnp.int32, (n, n), 0) == lax.broadcasted_iota(jnp.int32, (n, n), 1)
    return jnp.sum(jnp.where(eye, row, 0.0), axis=1, keepdims=True)


def _lane_block_mask(width, block, h):
    lane = lax.broadcasted_iota(jnp.int32, (1, width), 1)
    return ((lane >= h * block) & (lane < (h + 1) * block)).astype(F32)


def _log2(n):
    assert n & (n - 1) == 0
    return n.bit_length() - 1


def _block_diag_mask(rows, cols, rblock, cblock):
    r = lax.broadcasted_iota(jnp.int32, (rows, cols), 0) >> _log2(rblock)
    c = lax.broadcasted_iota(jnp.int32, (rows, cols), 1) >> _log2(cblock)
    return (r == c).astype(F32)


def _causal_masks():
    ri = lax.broadcasted_iota(jnp.int32, (CH, CH), 0)
    cj = lax.broadcasted_iota(jnp.int32, (CH, CH), 1)
    return cj <= ri, cj >= ri


def _bwd_gate_lanes():
    lane = lax.broadcasted_iota(jnp.int32, (1, LANES), 1)
    return (lane >= COL_FB) & (lane < COL_FB + H_A)


def _block_diag_value(ref, hn, a, b):
    rows = []
    for h in range(hn):
        wide = jnp.concatenate([ref[h], jnp.zeros((a, (hn - 1) * b), F32)], axis=1)
        rows.append(wide if h == 0 else pltpu.roll(wide, h * b, 1))
    return jnp.concatenate(rows, axis=0)


def _pieces(x):
    return jnp.concatenate(_split3(x), axis=1)


def _cum_max(x, reverse):
    n = x.shape[0]
    row = lax.broadcasted_iota(jnp.int32, x.shape, 0)
    s = 1
    while s < n:
        if reverse:
            shifted = jnp.where(row < n - s, pltpu.roll(x, n - s, 0), NEG)
        else:
            shifted = jnp.where(row >= s, pltpu.roll(x, s, 0), NEG)
        x = jnp.maximum(x, shifted)
        s *= 2
    return x


GATE_BLOCK = 16
N_PIECES = 3


def _gate_constants():
    half = N_PIECES * GATE_BLOCK
    cols = [COL_FF + h for h in range(H_A)] + [COL_FB + h for h in range(H_A)]
    sp = np.zeros((2 * N_PIECES * LANES, LANES), np.float32)
    for part in range(2):
        for p in range(N_PIECES):
            for k in cols:
                sp[(part * N_PIECES + p) * LANES + k, part * half + p * GATE_BLOCK + k] = 1.0
    lane = np.arange(LANES)
    sel = np.zeros((2 * H_A, LANES, 2 * DH_A), np.float32)
    gl = np.zeros((16, LANES), np.float32)
    for hd, k in enumerate(cols):
        mine = (lane % GATE_BLOCK == k) & (lane < 2 * half)
        gl[hd] = mine
        sel[hd, mine & (lane < half), 0:DH_A] = 1.0
        sel[hd, mine & (lane >= half), DH_A:] = 1.0
    valid = np.isin(lane % GATE_BLOCK, cols)
    gl[2 * H_A] = valid & (lane < half)
    gl[2 * H_A + 1] = valid & (lane >= half) & (lane < 2 * half)
    return jnp.asarray(sp, BF16), jnp.asarray(sel, BF16), jnp.asarray(gl, F32)


def _modulation_kernel(c_ref, w_ref, b_ref, o_ref):
    cv = c_ref[...]
    s = cv * _sigmoid(cv)
    o_ref[...] = _dot(_bf(s), _bf(w_ref[...])) + b_ref[...]


def _modulation(cstack, w_ada, b_ada):
    depth, d, d3 = w_ada.shape
    rows = cstack.shape[0]
    tn = 512
    return pl.pallas_call(
        _modulation_kernel,
        out_shape=jax.ShapeDtypeStruct((depth, rows, d3), F32),
        grid=(depth, d3 // tn),
        in_specs=[
            pl.BlockSpec((rows, d), lambda l, j: (0, 0)),
            pl.BlockSpec((None, d, tn), lambda l, j: (l, 0, j)),
            pl.BlockSpec((None, 1, tn), lambda l, j: (l, 0, j)),
        ],
        out_specs=pl.BlockSpec((None, rows, tn), lambda l, j: (l, 0, j)),
        name="adaln_modulation",
    )(cstack, w_ada, b_ada.reshape(depth, 1, d3))


def _layer_kernel(*refs, chained, rotary, final):
    it = iter(refs)
    x_ref, mod_ref, ng_ref, win_ref, gb_ref, lg_ref, w2_ref, b2_ref, hng_ref, wout_ref = (next(it) for _ in range(10))
    sp_ref, sel_ref, gl_ref = (next(it) for _ in range(3))
    fg_ref = next(it) if final else None
    if rotary:
        cos_ref, sin_ref = next(it), next(it)
    sC_ref, sn_ref, sm_ref_in, sR_ref, sG_ref = (next(it) for _ in range(5))
    out_ref = next(it)
    if not chained:
        oC_ref, on_ref, om_ref, oR_ref, oG_ref = (next(it) for _ in range(5))
    (h_ref, pj_ref, sm_ref, fc_ref, bf_ref, bb_ref, tot_ref, y_ref,
     gm_ref, min_ref, cm_ref, bop_ref) = (next(it) for _ in range(12))
    la_ref = pj_ref.at[:, 0:2 * LANES]
    dec_ref = pj_ref.at[:, W_B:W_B + CH]
    if chained:
        uCN_ref, uR_ref, uG_ref = (next(it) for _ in range(3))

    seqs, seqlen, _ = x_ref.shape

    def x_rows(ref, c):
        r0 = pl.multiple_of(c * CH, CH)
        return ref.at[r0 >> _log2(seqlen), pl.ds(pl.multiple_of(r0 & (seqlen - 1), CH), CH), :]

    def norm_body(c, carry):
        rows = pl.ds(pl.multiple_of(c * CH, CH), CH)
        x = x_rows(x_ref, c)[...]
        shift, scale = mod_ref[:, 0:D_MODEL], mod_ref[:, D_MODEL:2 * D_MODEL]
        xn = x * lax.rsqrt(jnp.mean(x * x, axis=-1, keepdims=True) + EPS) * ng_ref[...]
        h_ref[rows, :] = _bf(xn * (1.0 + scale) + shift)
        return carry

    lax.fori_loop(0, NCH, norm_body, 0)

    sm_ref[...] = _dot(h_ref[...], win_ref[:, OFF_S:OFF_S + W_S])

    def la_body(c, carry):
        rows = pl.ds(pl.multiple_of(c * CH, CH), CH)
        la_ref[rows, :] = _log_sigmoid(_dot(_bf(sm_ref[rows, :]), w2_ref[...]) + b2_ref[...]) * (1.0 / GLA_TAU)
        return carry

    lax.fori_loop(0, NCH, la_body, 0)

    def prefix_body(c, carry):
        rows = pl.ds(pl.multiple_of(c * CH, CH), CH)
        bwd_cols = _bwd_gate_lanes()
        tri = _bf(_causal_masks()[0].astype(F32))
        ls = _log_sigmoid(sm_ref[rows, :] + gb_ref[...])
        la = la_ref[rows, :]
        xcat = jnp.concatenate([ls, la], axis=1)
        ps = _exact_dot_l(tri, xcat)
        tot = ps[CH - 1:CH, :]
        fc_ref[rows, :] = jnp.where(bwd_cols, tot[:, 0:LANES] - ps[:, 0:LANES] + ls, ps[:, 0:LANES])
        bf_ref[rows, :] = ps[:, LANES:2 * LANES]
        bb_ref[rows, :] = tot[:, 2 * LANES:] - ps[:, 2 * LANES:] + la[:, LANES:]
        tot_ref[c] = tot
        return carry

    lax.fori_loop(0, NCH, prefix_body, 0)

    pj_ref[:, 0:W_A] = _dot(h_ref[...], win_ref[:, OFF_A:OFF_A + W_A])
    AQ, AK, AV, AO = 0, D_A, 2 * D_A, 3 * D_A
    k_scale = DH_A ** -0.5

    ones_cols = jnp.ones((CH, DH_A), F32)

    def mlstm_state_body(c, carry):
        rows = pl.ds(pl.multiple_of(c * CH, CH), CH)
        g_pre = sm_ref[rows, :] + gb_ref[...]
        rw = pltpu.roll(g_pre, 4, 1) - fc_ref[rows, :]
        cm_ref[rows, :] = jnp.where(_bwd_gate_lanes(), _cum_max(rw, True), _cum_max(rw, False))
        bop_ref[rows, :] = _bf(_dot(_pieces(rw), sp_ref[3 * LANES:, :]) + gl_ref[2 * H_A:2 * H_A + 1, :])
        totg = tot_ref[c][:, 0:LANES]
        g = totg + rw
        gm = jnp.max(g, axis=0, keepdims=True)
        gm_ref[c] = gm
        wgt = jnp.exp(g - gm).T
        if not chained:
            m_new = jnp.maximum(totg, gm)
            b_row = jnp.exp(gm - m_new)
            om_ref[c] = m_new
            min_ref[c] = jnp.zeros((1, LANES), F32)
        for hh in range(H_A):
            kt = (pj_ref[rows, AK + hh * DH_A:AK + (hh + 1) * DH_A] * k_scale).T
            v_h = pj_ref[rows, AV + hh * DH_A:AV + (hh + 1) * DH_A]
            lhs = jnp.concatenate([kt * wgt[COL_FF + hh:COL_FF + hh + 1, :],
                                   kt * wgt[COL_FB + hh:COL_FB + hh + 1, :]], axis=0)
            u = _dot(_bf(lhs), _bf(jnp.concatenate([v_h, ones_cols], axis=1)))
            for d, col in ((0, COL_FF), (1, COL_FB)):
                u_d = u[d * DH_A:(d + 1) * DH_A, :]
                if chained:
                    uCN_ref[c, d, hh] = u_d
                else:
                    fin = b_row[:, col + hh:col + hh + 1] * u_d
                    oC_ref[c, d, hh] = fin[:, 0:DH_A]
                    on_ref[c, d, hh:hh + 1, :] = fin[:, DH_A:].T[0:1, :]
        return carry

    lax.fori_loop(0, NCH, mlstm_state_body, 0)

    if chained:
        for d, order, col in ((0, range(NCH), COL_FF), (1, range(NCH - 1, -1, -1), COL_FB)):
            m_run = sm_ref_in[...]
            cn_run = [jnp.concatenate([sC_ref[d, hh],
                                       jnp.broadcast_to(sn_ref[d, hh:hh + 1, :], (DH_A, DH_A)).T], axis=1)
                      for hh in range(H_A)]
            for idx, c in enumerate(order):
                last = idx == NCH - 1
                if not last:
                    totg, gm = tot_ref[c][:, 0:LANES], gm_ref[c]
                    m_new = jnp.maximum(totg + m_run, gm)
                    a_row, b_row = jnp.exp(totg + m_run - m_new), jnp.exp(gm - m_new)
                for hh in range(H_A):
                    u = uCN_ref[c, d, hh]
                    uCN_ref[c, d, hh] = cn_run[hh]
                    if not last:
                        cn_run[hh] = (a_row[:, col + hh:col + hh + 1] * cn_run[hh]
                                      + b_row[:, col + hh:col + hh + 1] * u)
                min_ref[c] = m_run if d == 0 else jnp.where(_bwd_gate_lanes(), m_run, min_ref[c])
                if not last:
                    m_run = m_new

    def mlstm_out_body(c, carry):
        rows = pl.ds(pl.multiple_of(c * CH, CH), CH)
        lower, upper = _causal_masks()
        m_prev = min_ref[c]
        cmx = jnp.maximum(cm_ref[rows, :], m_prev)
        mi = fc_ref[rows, :] + cmx
        a_op = _bf(_dot(_pieces(-cmx), sp_ref[0:3 * LANES, :]) + gl_ref[2 * H_A + 1:2 * H_A + 2, :])
        b_op = bop_ref[rows, :]
        pk = _bf(_dot(jnp.concatenate([_pieces(mi), _pieces(m_prev - cmx)], axis=1), sp_ref[...]))
        for hh in range(H_A):
            q16 = _bf(pj_ref[rows, AQ + hh * DH_A:AQ + (hh + 1) * DH_A])
            k_h = pj_ref[rows, AK + hh * DH_A:AK + (hh + 1) * DH_A] * k_scale
            v_h = pj_ref[rows, AV + hh * DH_A:AV + (hh + 1) * DH_A]
            o_h = pj_ref[rows, AO + hh * DH_A:AO + (hh + 1) * DH_A]
            s = _dot_nt(q16, _bf(k_h))
            qs = []
            for d, msk in ((0, lower), (1, upper)):
                hd = d * H_A + hh
                dm = _dot_nt(a_op * _bf(gl_ref[hd:hd + 1, :]), b_op)
                qs.append(_bf(s * jnp.exp(jnp.where(msk, dm, NEG))))
            nd = _dot(jnp.concatenate(qs, axis=0), _bf(jnp.concatenate([v_h, ones_cols], axis=1)))
            hs = None
            for d in range(2):
                hd = d * H_A + hh
                num, den = nd[d * CH:(d + 1) * CH, 0:DH_A], nd[d * CH:(d + 1) * CH, DH_A:]
                rep = _dot(pk, sel_ref[hd])
                if chained:
                    wa = jnp.exp(rep[:, DH_A:])
                    inter = _dot(q16, _bf(uCN_ref[c, d, hh]))
                    num = num + wa * inter[:, 0:DH_A]
                    den = den + wa * inter[:, DH_A:]
                part = num / jnp.maximum(jnp.abs(den), jnp.exp(-rep[:, 0:DH_A]))
                hs = part if hs is None else hs + part
            y_ref[rows, hh * DH_A:(hh + 1) * DH_A] = _sigmoid(o_h) * hs
        return carry

    lax.fori_loop(0, NCH, mlstm_out_body, 0)

    pj_ref[:, 0:W_B] = _dot(h_ref[...], win_ref[:, OFF_B:OFF_B + W_B])
    BQ, BK, BV = 0, D_B, 2 * D_B
    if rotary:
        def rotary_body(c, carry):
            rows = pl.ds(pl.multiple_of(c * CH, CH), CH)
            first_half = (lax.broadcasted_iota(jnp.int32, (1, D_B), 1) & (DH_B - 1)) < DH_B // 2
            for off in (BQ, BK):
                t = pj_ref[rows, off:off + D_B]
                partner = jnp.where(first_half, pltpu.roll(t, D_B - DH_B // 2, 1), pltpu.roll(t, DH_B // 2, 1))
                pj_ref[rows, off:off + D_B] = t * cos_ref[rows, :] + partner * sin_ref[rows, :]
            return carry

        lax.fori_loop(0, NCH, rotary_body, 0)

    lgam = _log_sigmoid(lg_ref[...])
    lg_f, lg_b = lgam[0:1, :], lgam[1:2, :]
    pos = lax.broadcasted_iota(jnp.int32, (CH, 1), 0).astype(F32)
    lower, upper = _causal_masks()
    rel = (lax.broadcasted_iota(jnp.int32, (CH, CH), 0) - lax.broadcasted_iota(jnp.int32, (CH, CH), 1)).astype(F32)
    for hh in range(H_B):
        lf = lg_f[:, hh * DH_B:hh * DH_B + 1]
        lb = lg_b[:, hh * DH_B:hh * DH_B + 1]
        dec_ref[hh * CH:(hh + 1) * CH, :] = (jnp.where(lower, jnp.exp(jnp.maximum(rel, 0.0) * lf), 0.0)
                                             + jnp.where(upper, jnp.exp(jnp.maximum(-rel, 0.0) * lb), 0.0))
    bd_b = _block_diag_mask(D_B, D_B, DH_B, DH_B)
    hm_b = [_lane_block_mask(D_B, DH_B, hh) for hh in range(H_B)]
    ret_scale = DH_B ** -0.5

    def ret_state_body(c, carry):
        rows = pl.ds(pl.multiple_of(c * CH, CH), CH)
        k = pj_ref[rows, BK:BK + D_B] * ret_scale
        v = pj_ref[rows, BV:BV + D_B]
        kf = k * jnp.exp((CH - 1.0 - pos) * lg_f)
        kb = k * jnp.exp(pos * lg_b)
        u = _dot(_bf(jnp.concatenate([kf, kb], axis=1).T), _bf(v))
        for d in range(2):
            u_d = u[d * D_B:(d + 1) * D_B, :]
            if chained:
                uR_ref[c, d] = u_d * bd_b
            else:
                for hh in range(H_B):
                    oR_ref[c, d, hh] = u_d[hh * DH_B:(hh + 1) * DH_B, hh * DH_B:(hh + 1) * DH_B]
        return carry

    lax.fori_loop(0, NCH, ret_state_body, 0)

    if chained:
        for d, order, lg_row in ((0, range(NCH), lg_f), (1, range(NCH - 1, -1, -1), lg_b)):
            g_col = _row_to_col(jnp.exp(float(CH) * lg_row), D_B)
            s_run = _block_diag_value(sR_ref.at[d], H_B, DH_B, DH_B)
            for idx, c in enumerate(order):
                u = uR_ref[c, d]
                uR_ref[c, d] = s_run
                if idx != NCH - 1:
                    s_run = g_col * s_run + u

    def ret_out_body(c, carry):
        rows = pl.ds(pl.multiple_of(c * CH, CH), CH)
        q = pj_ref[rows, BQ:BQ + D_B]
        k = pj_ref[rows, BK:BK + D_B] * ret_scale
        v = pj_ref[rows, BV:BV + D_B]
        kb16 = _bf(k)
        ps = [_bf(_dot_nt(_bf(q * hm_b[hh]), kb16) * dec_ref[hh * CH:(hh + 1) * CH, :]) for hh in range(H_B)]
        vst = jnp.concatenate([_bf(v * hm_b[hh]) for hh in range(H_B)], axis=0)
        yb = _dot(jnp.concatenate(ps, axis=1), vst)
        if chained:
            qf = q * jnp.exp((pos + 1.0) * lg_f)
            qb = q * jnp.exp((float(CH) - pos) * lg_b)
            s_in = jnp.concatenate([uR_ref[c, 0], uR_ref[c, 1]], axis=0)
            yb = yb + _dot(_bf(jnp.concatenate([qf, qb], axis=1)), _bf(s_in))
        y_ref[rows, D_A:D_A + D_B] = yb
        return carry

    lax.fori_loop(0, NCH, ret_out_body, 0)

    pj_ref[:, 0:W_C] = _dot(h_ref[...], win_ref[:, OFF_C:OFF_C + W_C])
    KC = H_C * DK_C
    CQ, CK, CV = 0, KC, 2 * KC
    gla_scale = DK_C ** -0.5
    bd_c = _block_diag_mask(KC, D_C, DK_C, DV_C)
    hm_ck = [_lane_block_mask(KC, DK_C, hh) for hh in range(H_C)]
    hm_cv = [_lane_block_mask(D_C, DV_C, hh) for hh in range(H_C)]

    def gla_state_body(c, carry):
        rows = pl.ds(pl.multiple_of(c * CH, CH), CH)
        k = pj_ref[rows, CK:CK + KC] * gla_scale
        v = pj_ref[rows, CV:CV + D_C]
        tot = tot_ref[c]
        khf = k * jnp.exp(tot[:, LANES:2 * LANES] - bf_ref[rows, :])
        khb = k * jnp.exp(tot[:, 2 * LANES:] - bb_ref[rows, :])
        u = _dot(_bf(jnp.concatenate([khf, khb], axis=1).T), _bf(v))
        for d in range(2):
            u_d = u[d * KC:(d + 1) * KC, :]
            if chained:
                uG_ref[c, d] = u_d * bd_c
            else:
                for hh in range(H_C):
                    oG_ref[c, d, hh] = u_d[hh * DK_C:(hh + 1) * DK_C, hh * DV_C:(hh + 1) * DV_C]
        return carry

    lax.fori_loop(0, NCH, gla_state_body, 0)

    if chained:
        for d, order in ((0, range(NCH)), (1, range(NCH - 1, -1, -1))):
            s_run = _block_diag_value(sG_ref.at[d], H_C, DK_C, DV_C)
            for idx, c in enumerate(order):
                u = uG_ref[c, d]
                uG_ref[c, d] = s_run
                if idx != NCH - 1:
                    tot_row = tot_ref[c][:, (1 + d) * LANES:(2 + d) * LANES]
                    s_run = _row_to_col(jnp.exp(tot_row), KC) * s_run + u

    def gla_out_body(c, carry):
        rows = pl.ds(pl.multiple_of(c * CH, CH), CH)
        q = pj_ref[rows, CQ:CQ + KC]
        k = pj_ref[rows, CK:CK + KC] * gla_scale
        v = pj_ref[rows, CV:CV + D_C]
        b_f, b_b = bf_ref[rows, :], bb_ref[rows, :]
        lower, upper = _causal_masks()
        qf, qb = q * jnp.exp(b_f), q * jnp.exp(b_b)
        kf16, kb16 = _bf(k * jnp.exp(-b_f)), _bf(k * jnp.exp(-b_b))
        ps = []
        for hh in range(H_C):
            s_f = _dot_nt(_bf(qf * hm_ck[hh]), kf16)
            s_b = _dot_nt(_bf(qb * hm_ck[hh]), kb16)
            ps.append(_bf(jnp.where(lower, s_f, 0.0) + jnp.where(upper, s_b, 0.0)))
        vst = jnp.concatenate([_bf(v * hm_cv[hh]) for hh in range(H_C)], axis=0)
        yc = _dot(jnp.concatenate(ps, axis=1), vst)
        if chained:
            s_in = jnp.concatenate([uG_ref[c, 0], uG_ref[c, 1]], axis=0)
            yc = yc + _dot(_bf(jnp.concatenate([qf, qb], axis=1)), _bf(s_in))
        y_ref[rows, D_A + D_B:D_MODEL] = yc
        return carry

    lax.fori_loop(0, NCH, gla_out_body, 0)

    pj_ref[:, 0:W_Z] = _dot(h_ref[...], win_ref[:, OFF_Z:OFF_Z + W_Z])
    def gate_body(c, carry):
        rows = pl.ds(pl.multiple_of(c * CH, CH), CH)
        parts = []
        for hh in range(H_A):
            ya = y_ref[rows, hh * DH_A:(hh + 1) * DH_A]
            parts.append(ya * lax.rsqrt(jnp.mean(ya * ya, axis=1, keepdims=True) + EPS))
        ybc = y_ref[rows, D_A:D_MODEL]
        seg = _bf(_block_diag_mask(D_B + D_C, D_B + D_C, DH_B, DH_B) * (1.0 / DH_B))
        parts.append(ybc * lax.rsqrt(_exact_dot_r(ybc * ybc, seg) + EPS))
        z = pj_ref[rows, 0:W_Z]
        yg = jnp.concatenate(parts, axis=1) * hng_ref[...] * (z * _sigmoid(z))
        h_ref[rows, :] = _bf(yg)
        return carry

    lax.fori_loop(0, NCH, gate_body, 0)
    pj_ref[:, 0:D_MODEL] = _dot(h_ref[...], wout_ref[...])

    def residual_body(c, carry):
        rows = pl.ds(pl.multiple_of(c * CH, CH), CH)
        xo = x_rows(x_ref, c)[...] + mod_ref[:, 2 * D_MODEL:] * pj_ref[rows, 0:D_MODEL]
        if final:
            xo = xo * lax.rsqrt(jnp.mean(xo * xo, axis=-1, keepdims=True) + EPS) * fg_ref[...]
        x_rows(out_ref, c)[...] = xo
        return carry

    lax.fori_loop(0, NCH, residual_body, 0)


def _layer_call(l, x, mods, mod_row, norm_g, w_in_r, gate_b, lg_rows, w2_full, b2_full, hn_g, w_out_b, final_g,
                states, rot=None, chained=False, final=False):
    batch, seqlen, d = x.shape
    seqs = TOK // seqlen
    steps = batch // seqs
    assert seqs * seqlen == TOK and steps * seqs == batch and seqlen % CH == 0
    assert chained == (seqs == 1)
    assert chained or seqlen == CH

    def const(shape):
        nd = len(shape)
        return pl.BlockSpec(shape, lambda i, _nd=nd: (0,) * _nd)

    def layer_block(shape):
        nd = len(shape)
        return pl.BlockSpec((None,) + shape, lambda i, _nd=nd: (l,) + (0,) * _nd)

    in_specs = [
        pl.BlockSpec((seqs, seqlen, d), lambda i: (i, 0, 0)),
        pl.BlockSpec((None, None, 1, 3 * d),
                     (lambda i: (l, i, 0, 0)) if mod_row is None else (lambda i: (l, mod_row, 0, 0))),
        layer_block((1, d)),
        pl.BlockSpec((None, d, D_INR), lambda i: (l, 0, 0), pipeline_mode=pl.Buffered(1)),
        layer_block((1, LANES)),
        layer_block((2, D_B)),
        layer_block((LANES, 2 * LANES)),
        layer_block((1, 2 * LANES)),
        layer_block((1, d)),
        pl.BlockSpec((None, d, d), lambda i: (l, 0, 0), pipeline_mode=pl.Buffered(1)),
    ]
    args = [x, mods, norm_g.reshape(-1, 1, d), w_in_r, gate_b, lg_rows, w2_full, b2_full,
            hn_g.reshape(-1, 1, d), w_out_b]
    for cst in _gate_constants():
        in_specs.append(pl.BlockSpec(cst.shape, lambda i, _nd=cst.ndim: (0,) * _nd, pipeline_mode=pl.Buffered(1)))
        args.append(cst)
    if final:
        in_specs.append(const((1, d)))
        args.append(final_g.reshape(1, d))
    if rot is not None:
        in_specs += [pl.BlockSpec((seqlen, D_B), lambda i: (0, 0), pipeline_mode=pl.Buffered(1))] * 2
        args += list(rot)
    state_blocks = [(2, H_A, DH_A, DH_A), (2, H_A, DH_A), (1, LANES), (2, H_B, DH_B, DH_B), (2, H_C, DK_C, DV_C)]
    n_in = len(args)
    out_shape = [jax.ShapeDtypeStruct(x.shape, x.dtype)]
    out_specs = [pl.BlockSpec((seqs, seqlen, d), lambda i: (i, 0, 0))]
    aliases = {}
    for k, (blk, arr) in enumerate(zip(state_blocks, states)):
        zeros = (0,) * len(blk)
        if chained:
            in_specs.append(pl.BlockSpec((None, None) + blk, lambda i, _z=zeros: (i, l) + _z))
        else:
            in_specs.append(pl.BlockSpec(memory_space=pl.ANY))
            out_shape.append(jax.ShapeDtypeStruct(arr.shape, arr.dtype))
            out_specs.append(pl.BlockSpec((seqs, None) + blk, lambda i, _z=zeros: (i, l) + _z))
            aliases[n_in + k] = 1 + k
        args.append(arr)

    scratch = [
        pltpu.VMEM((TOK, d), BF16),
        pltpu.VMEM((TOK, W_A), F32),
        pltpu.VMEM((TOK, LANES), F32),
        pltpu.VMEM((TOK, LANES), F32),
        pltpu.VMEM((TOK, LANES), F32),
        pltpu.VMEM((TOK, LANES), F32),
        pltpu.VMEM((NCH, 1, 3 * LANES), F32),
        pltpu.VMEM((TOK, d), F32),
        pltpu.VMEM((NCH, 1, LANES), F32),
        pltpu.VMEM((NCH, 1, LANES), F32),
        pltpu.VMEM((TOK, LANES), F32),
        pltpu.VMEM((TOK, LANES), BF16),
    ]
    if chained:
        scratch += [
            pltpu.VMEM((NCH, 2, H_A, DH_A, 2 * DH_A), F32),
            pltpu.VMEM((NCH, 2, D_B, D_B), F32),
            pltpu.VMEM((NCH, 2, H_C * DK_C, D_C), F32),
        ]
    outs = pl.pallas_call(
        functools.partial(_layer_kernel, chained=chained, rotary=rot is not None, final=final),
        out_shape=out_shape,
        grid=(steps,),
        in_specs=in_specs,
        out_specs=out_specs,
        scratch_shapes=scratch,
        input_output_aliases=aliases,
        compiler_params=pltpu.CompilerParams(dimension_semantics=("arbitrary",), vmem_limit_bytes=VMEM_LIMIT),
        name=("latent" if chained else "context") + f"_layer{l}",
    )(*args)
    return outs


def _rotary_tables(seqlen):
    rows = seqlen // GRID_W
    r = jnp.repeat(jnp.arange(rows, dtype=F32), GRID_W)
    col = jnp.tile(jnp.arange(GRID_W, dtype=F32), rows)
    n_f = DH_B // 4
    freqs = ROPE_BASE ** (-jnp.arange(n_f, dtype=F32) / n_f)
    ang = jnp.concatenate([r[:, None] * freqs, col[:, None] * freqs], axis=-1)
    cos, sin = jnp.cos(ang), jnp.sin(ang)
    cos_l = jnp.tile(jnp.concatenate([cos, cos], axis=-1), (1, H_B))
    sin_l = jnp.tile(jnp.concatenate([-sin, sin], axis=-1), (1, H_B))
    return cos_l, sin_l


def _gate_lanes(m):
    z = jnp.zeros(m.shape[:-2] + (LANES,), m.dtype)
    z = z.at[..., COL_FF:COL_FF + H_A].set(m[..., 0, :]).at[..., COL_FB:COL_FB + H_A].set(m[..., 1, :])
    return z[..., None, :]


def _pack_segments():
    src = dict(zip("aq ak av ao az ag bq bk bv bz cq ck cv cz clr".split(),
                   zip(np.cumsum((0,) + IN_WIDTHS[:-1]).tolist(), IN_WIDTHS)))
    order = "aq ak av ao bq bk bv cq ck cv az bz cz ag clr".split()
    segs, dst = [], 0
    for name in order:
        s, w = src[name]
        segs.append((s, dst, w))
        dst += w
    return segs, dst


def _pack_kernel(w_ref, o_ref):
    segs, end = _pack_segments()
    for s, dst, w in segs:
        o_ref[:, dst:dst + w] = _bf(w_ref[:, s:s + w])
    o_ref[:, end:] = jnp.zeros((o_ref.shape[0], D_INR - end), BF16)


def _pack_w_in(w_in):
    depth, d, d_in = w_in.shape
    tr = 256
    return pl.pallas_call(
        _pack_kernel,
        out_shape=jax.ShapeDtypeStruct((depth, d, D_INR), BF16),
        grid=(depth, d // tr),
        in_specs=[pl.BlockSpec((None, tr, d_in), lambda l, r: (l, r, 0))],
        out_specs=pl.BlockSpec((None, tr, D_INR), lambda l, r: (l, r, 0)),
        name="pack_w_in",
    )(w_in)


def kernel(x_prompt, x_sample, state_mlstm_C, state_mlstm_n, state_mlstm_m, state_ret, state_gla, c, c_ctx,
           norm_g, w_ada, b_ada, w_in, mlstm_gate_b, ret_decay_logit, gla_w2, gla_b2, headnorm_g, w_out, final_g):
    depth = w_in.shape[0]
    dec_batch = c.shape[0]

    w_in_r = _pack_w_in(w_in)
    w_out_b = w_out.astype(BF16)
    n_g = mlstm_gate_b.shape[-1]
    gate_b = jnp.pad(mlstm_gate_b, ((0, 0), (0, LANES - n_g)))[:, None, :]
    lg_rows = jnp.repeat(ret_decay_logit, DH_B, axis=-1)
    w2_full = jnp.zeros((depth, LANES, 2 * LANES), F32)
    w2_full = w2_full.at[:, n_g:n_g + GLA_RANK, 0:LANES].set(gla_w2[:, 0])
    w2_full = w2_full.at[:, n_g + GLA_RANK:n_g + 2 * GLA_RANK, LANES:].set(gla_w2[:, 1]).astype(BF16)
    b2_full = gla_b2.reshape(depth, 1, 2 * LANES)

    cstack = jnp.zeros((16, D_MODEL), F32).at[0:dec_batch].set(c).at[dec_batch].set(c_ctx)
    mods = _modulation(cstack, w_ada, b_ada)[:, :, None, :]

    rot = _rotary_tables(x_sample.shape[1])
    cache = (state_mlstm_C, state_mlstm_n, _gate_lanes(state_mlstm_m), state_ret, state_gla)
    bp = x_prompt.shape[0]
    new = tuple(jnp.zeros((bp,) + s.shape[1:], F32) for s in cache)

    xp, xs = x_prompt, x_sample
    for l in range(depth):
        final = l == depth - 1
        common = (norm_g, w_in_r, gate_b, lg_rows, w2_full, b2_full, headnorm_g, w_out_b, final_g)
        outs = _layer_call(l, xp, mods, dec_batch, *common, states=new, final=final)
        xp, new = outs[0], tuple(outs[1:])
        xs = _layer_call(l, xs, mods, None, *common, states=cache, rot=rot, chained=True, final=final)[0]

    new_c, new_n, m_l, new_r, new_g = new
    new_m = jnp.stack([m_l[:, :, 0, COL_FF:COL_FF + H_A], m_l[:, :, 0, COL_FB:COL_FB + H_A]], axis=2)
    return (xp, xs, new_c, new_n, new_m, new_r, new_g)
```

```python
import functools

import jax
import jax.numpy as jnp
import numpy as np
from jax import lax
from jax.experimental import pallas as pl
from jax.experimental.pallas import tpu as pltpu

F32 = jnp.float32
BF16 = jnp.bfloat16

D_MODEL = 1024
H_A, DH_A = 4, 128
H_B, DH_B = 4, 64
H_C, DK_C, DV_C = 4, 32, 64
D_A, D_B, D_C = H_A * DH_A, H_B * DH_B, H_C * DV_C
GLA_RANK = 16
GLA_TAU = 16.0
GRID_W = 64
ROPE_BASE = 10000.0
EPS = 1e-6
IN_WIDTHS = (D_A, D_A, D_A, D_A, D_A, 4 * H_A, D_B, D_B, D_B, D_B, H_C * DK_C, H_C * DK_C, D_C, D_C, 2 * GLA_RANK)

LANES = 128
CH = 256
TOK = 1024
NCH = TOK // CH
VMEM_LIMIT = 60 * 1024 * 1024
NEG = -1e30
GLA_SAFE_LOG = -80.0

OFF_A, W_A = 0, 4 * D_A
OFF_B, W_B = OFF_A + W_A, 3 * D_B
OFF_C, W_C = OFF_B + W_B, 2 * H_C * DK_C + D_C
OFF_Z, W_Z = OFF_C + W_C, D_MODEL
OFF_S, W_S = OFF_Z + W_Z, LANES
D_INR = OFF_S + W_S
COL_FF, COL_FB = 4, 12


def _dot(a, b):
    return jnp.dot(a, b, preferred_element_type=F32)


def _dot_nt(a, b):
    return lax.dot_general(a, b, (((1,), (1,)), ((), ())), preferred_element_type=F32)


def _bf(x):
    return x.astype(BF16)


def _split3(x):
    hi = _bf(x)
    r1 = x - hi.astype(F32)
    mid = _bf(r1)
    lo = _bf(r1 - mid.astype(F32))
    return hi, mid, lo


def _exact_dot_l(m_bf, x):
    hi, mid, lo = _split3(x)
    return (_dot(m_bf, lo) + _dot(m_bf, mid)) + _dot(m_bf, hi)


def _exact_dot_r(x, m_bf):
    hi, mid, lo = _split3(x)
    return (_dot(lo, m_bf) + _dot(mid, m_bf)) + _dot(hi, m_bf)


def _log_sigmoid(x):
    return jnp.minimum(x, 0.0) - jnp.log(1.0 + jnp.exp(-jnp.abs(x)))


def _sigmoid(x):
    return 1.0 / (1.0 + jnp.exp(-x))


def _row_to_col(row, n):
    eye = lax.broadcasted_iota(jnp.int32, (n, n), 0) == lax.broadcasted_iota(jnp.int32, (n, n), 1)
    return jnp.sum(jnp.where(eye, row, 0.0), axis=1, keepdims=True)


def _lane_block_mask(width, block, h):
    lane = lax.broadcasted_iota(jnp.int32, (1, width), 1)
    return ((lane >= h * block) & (lane < (h + 1) * block)).astype(F32)


def _log2(n):
    assert n & (n - 1) == 0
    return n.bit_length() - 1


def _block_diag_mask(rows, cols, rblock, cblock):
    r = lax.broadcasted_iota(jnp.int32, (rows, cols), 0) >> _log2(rblock)
    c = lax.broadcasted_iota(jnp.int32, (rows, cols), 1) >> _log2(cblock)
    return (r == c).astype(F32)


def _causal_masks():
    ri = lax.broadcasted_iota(jnp.int32, (CH, CH), 0)
    cj = lax.broadcasted_iota(jnp.int32, (CH, CH), 1)
    return cj <= ri, cj >= ri


def _bwd_gate_lanes():
    lane = lax.broadcasted_iota(jnp.int32, (1, LANES), 1)
    return (lane >= COL_FB) & (lane < COL_FB + H_A)


def _block_diag_value(ref, hn, a, b):
    rows = []
    for h in range(hn):
        wide = jnp.concatenate([ref[h], jnp.zeros((a, (hn - 1) * b), F32)], axis=1)
        rows.append(wide if h == 0 else pltpu.roll(wide, h * b, 1))
    return jnp.concatenate(rows, axis=0)


def _pieces(x):
    return jnp.concatenate(_split3(x), axis=1)


def _cum_max(x, reverse):
    n = x.shape[0]
    row = lax.broadcasted_iota(jnp.int32, x.shape, 0)
    s = 1
    while s < n:
        if reverse:
            shifted = jnp.where(row < n - s, pltpu.roll(x, n - s, 0), NEG)
        else:
            shifted = jnp.where(row >= s, pltpu.roll(x, s, 0), NEG)
        x = jnp.maximum(x, shifted)
        s *= 2
    return x


GATE_BLOCK = 16
N_PIECES = 3


def _gate_constants():
    half = N_PIECES * GATE_BLOCK
    cols = [COL_FF + h for h in range(H_A)] + [COL_FB + h for h in range(H_A)]
    sp = np.zeros((2 * N_PIECES * LANES, LANES), np.float32)
    for part in range(2):
        for p in range(N_PIECES):
            for k in cols:
                sp[(part * N_PIECES + p) * LANES + k, part * half + p * GATE_BLOCK + k] = 1.0
    lane = np.arange(LANES)
    gl = np.zeros((16, LANES), np.float32)
    for hd, k in enumerate(cols):
        gl[hd] = (lane % GATE_BLOCK == k) & (lane < 2 * half)
    valid = np.isin(lane % GATE_BLOCK, cols)
    gl[2 * H_A] = valid & (lane < half)
    gl[2 * H_A + 1] = valid & (lane >= half) & (lane < 2 * half)
    return jnp.asarray(sp, BF16), jnp.asarray(gl, F32)


def _modulation_kernel(c_ref, w_ref, b_ref, o_ref):
    cv = c_ref[...]
    s = cv * _sigmoid(cv)
    o_ref[...] = _dot(_bf(s), _bf(w_ref[...])) + b_ref[...]


def _modulation(cstack, w_ada, b_ada):
    depth, d, d3 = w_ada.shape
    rows = cstack.shape[0]
    tn = 512
    return pl.pallas_call(
        _modulation_kernel,
        out_shape=jax.ShapeDtypeStruct((depth, rows, d3), F32),
        grid=(depth, d3 // tn),
        in_specs=[
            pl.BlockSpec((rows, d), lambda l, j: (0, 0)),
            pl.BlockSpec((None, d, tn), lambda l, j: (l, 0, j)),
            pl.BlockSpec((None, 1, tn), lambda l, j: (l, 0, j)),
        ],
        out_specs=pl.BlockSpec((None, rows, tn), lambda l, j: (l, 0, j)),
        name="adaln_modulation",
    )(cstack, w_ada, b_ada.reshape(depth, 1, d3))


def _layer_kernel(*refs, chained, rotary, final):
    it = iter(refs)
    x_ref, mod_ref, ng_ref, win_ref, gb_ref, lg_ref, w2_ref, b2_ref, hng_ref, wout_ref = (next(it) for _ in range(10))
    sp_ref, gl_ref = next(it), next(it)
    fg_ref = next(it) if final else None
    if rotary:
        cos_ref, sin_ref = next(it), next(it)
    sC_ref, sn_ref, sm_ref_in, sR_ref, sG_ref = (next(it) for _ in range(5))
    out_ref = next(it)
    if not chained:
        oC_ref, on_ref, om_ref, oR_ref, oG_ref = (next(it) for _ in range(5))
    (h_ref, pj_ref, sm_ref, fc_ref, bf_ref, bb_ref, tot_ref, y_ref,
     gm_ref, min_ref, cm_ref, bop_ref) = (next(it) for _ in range(12))
    la_ref = pj_ref.at[:, 0:2 * LANES]
    dec_ref = pj_ref.at[:, W_B:W_B + CH]
    if chained:
        uCN_ref, uR_ref, uG_ref = (next(it) for _ in range(3))

    seqs, seqlen, _ = x_ref.shape

    def x_rows(ref, c):
        r0 = pl.multiple_of(c * CH, CH)
        return ref.at[r0 >> _log2(seqlen), pl.ds(pl.multiple_of(r0 & (seqlen - 1), CH), CH), :]

    def norm_body(c, carry):
        rows = pl.ds(pl.multiple_of(c * CH, CH), CH)
        x = x_rows(x_ref, c)[...]
        shift, scale = mod_ref[:, 0:D_MODEL], mod_ref[:, D_MODEL:2 * D_MODEL]
        xn = x * lax.rsqrt(jnp.mean(x * x, axis=-1, keepdims=True) + EPS) * ng_ref[...]
        h_ref[rows, :] = _bf(xn * (1.0 + scale) + shift)
        return carry

    lax.fori_loop(0, NCH, norm_body, 0)

    sm_ref[...] = _dot(h_ref[...], win_ref[:, OFF_S:OFF_S + W_S])

    def la_body(c, carry):
        rows = pl.ds(pl.multiple_of(c * CH, CH), CH)
        la_ref[rows, :] = _log_sigmoid(_dot(_bf(sm_ref[rows, :]), w2_ref[...]) + b2_ref[...]) * (1.0 / GLA_TAU)
        return carry

    lax.fori_loop(0, NCH, la_body, 0)

    def prefix_body(c, carry):
        rows = pl.ds(pl.multiple_of(c * CH, CH), CH)
        bwd_cols = _bwd_gate_lanes()
        tri = _bf(_causal_masks()[0].astype(F32))
        ls = _log_sigmoid(sm_ref[rows, :] + gb_ref[...])
        la = la_ref[rows, :]
        xcat = jnp.concatenate([ls, la], axis=1)
        ps = _exact_dot_l(tri, xcat)
        tot = ps[CH - 1:CH, :]
        fc_ref[rows, :] = jnp.where(bwd_cols, tot[:, 0:LANES] - ps[:, 0:LANES] + ls, ps[:, 0:LANES])
        bf_ref[rows, :] = ps[:, LANES:2 * LANES]
        bb_ref[rows, :] = tot[:, 2 * LANES:] - ps[:, 2 * LANES:] + la[:, LANES:]
        tot_ref[c] = tot
        return carry

    lax.fori_loop(0, NCH, prefix_body, 0)

    pj_ref[:, 0:W_A] = _dot(h_ref[...], win_ref[:, OFF_A:OFF_A + W_A])
    AQ, AK, AV, AO = 0, D_A, 2 * D_A, 3 * D_A
    k_scale = DH_A ** -0.5

    ones_cols = jnp.ones((CH, DH_A), F32)

    def mlstm_state_body(c, carry):
        rows = pl.ds(pl.multiple_of(c * CH, CH), CH)
        g_pre = sm_ref[rows, :] + gb_ref[...]
        rw = pltpu.roll(g_pre, 4, 1) - fc_ref[rows, :]
        cm_ref[rows, :] = jnp.where(_bwd_gate_lanes(), _cum_max(rw, True), _cum_max(rw, False))
        bop_ref[rows, :] = _bf(_dot(_pieces(rw), sp_ref[3 * LANES:, :]) + gl_ref[2 * H_A:2 * H_A + 1, :])
        totg = tot_ref[c][:, 0:LANES]
        g = totg + rw
        gm = jnp.max(g, axis=0, keepdims=True)
        gm_ref[c] = gm
        wgt = jnp.exp(g - gm).T
        if not chained:
            m_new = jnp.maximum(totg, gm)
            b_row = jnp.exp(gm - m_new)
            om_ref[c] = m_new
            min_ref[c] = jnp.zeros((1, LANES), F32)
        for hh in range(H_A):
            kt = (pj_ref[rows, AK + hh * DH_A:AK + (hh + 1) * DH_A] * k_scale).T
            v_h = pj_ref[rows, AV + hh * DH_A:AV + (hh + 1) * DH_A]
            lhs = jnp.concatenate([kt * wgt[COL_FF + hh:COL_FF + hh + 1, :],
                                   kt * wgt[COL_FB + hh:COL_FB + hh + 1, :]], axis=0)
            u = _dot(_bf(lhs), _bf(jnp.concatenate([v_h, ones_cols], axis=1)))
            for d, col in ((0, COL_FF), (1, COL_FB)):
                u_d = u[d * DH_A:(d + 1) * DH_A, :]
                if chained:
                    uCN_ref[c, d, hh] = u_d
                else:
                    fin = b_row[:, col + hh:col + hh + 1] * u_d
                    oC_ref[c, d, hh] = fin[:, 0:DH_A]
                    on_ref[c, d, hh:hh + 1, :] = fin[:, DH_A:].T[0:1, :]
        return carry

    lax.fori_loop(0, NCH, mlstm_state_body, 0)

    if chained:
        for d, order, col in ((0, range(NCH), COL_FF), (1, range(NCH - 1, -1, -1), COL_FB)):
            m_run = sm_ref_in[...]
            cn_run = [jnp.concatenate([sC_ref[d, hh],
                                       jnp.broadcast_to(sn_ref[d, hh:hh + 1, :], (DH_A, DH_A)).T], axis=1)
                      for hh in range(H_A)]
            for idx, c in enumerate(order):
                last = idx == NCH - 1
                if not last:
                    totg, gm = tot_ref[c][:, 0:LANES], gm_ref[c]
                    m_new = jnp.maximum(totg + m_run, gm)
                    a_row, b_row = jnp.exp(totg + m_run - m_new), jnp.exp(gm - m_new)
                for hh in range(H_A):
                    u = uCN_ref[c, d, hh]
                    uCN_ref[c, d, hh] = cn_run[hh]
                    if not last:
                        cn_run[hh] = (a_row[:, col + hh:col + hh + 1] * cn_run[hh]
                                      + b_row[:, col + hh:col + hh + 1] * u)
                min_ref[c] = m_run if d == 0 else jnp.where(_bwd_gate_lanes(), m_run, min_ref[c])
                if not last:
                    m_run = m_new

    def mlstm_out_body(c, carry):
        rows = pl.ds(pl.multiple_of(c * CH, CH), CH)
        lower, upper = _causal_masks()
        m_prev = min_ref[c]
        cmx = jnp.maximum(cm_ref[rows, :], m_prev)
        mi = fc_ref[rows, :] + cmx
        a_op = _bf(_dot(_pieces(-cmx), sp_ref[0:3 * LANES, :]) + gl_ref[2 * H_A + 1:2 * H_A + 2, :])
        b_op = bop_ref[rows, :]
        floor_all = jnp.exp(-mi)
        wa_all = jnp.exp(m_prev - cmx)
        for hh in range(H_A):
            q16 = _bf(pj_ref[rows, AQ + hh * DH_A:AQ + (hh + 1) * DH_A])
            k_h = pj_ref[rows, AK + hh * DH_A:AK + (hh + 1) * DH_A] * k_scale
            v_h = pj_ref[rows, AV + hh * DH_A:AV + (hh + 1) * DH_A]
            o_h = pj_ref[rows, AO + hh * DH_A:AO + (hh + 1) * DH_A]
            s = _dot_nt(q16, _bf(k_h))
            qs = []
            for d, msk in ((0, lower), (1, upper)):
                hd = d * H_A + hh
                dm = _dot_nt(a_op * _bf(gl_ref[hd:hd + 1, :]), b_op)
                qs.append(_bf(s * jnp.exp(jnp.where(msk, dm, NEG))))
            nd = _dot(jnp.concatenate(qs, axis=0), _bf(jnp.concatenate([v_h, ones_cols], axis=1)))
            hs = None
            for d, col in ((0, COL_FF + hh), (1, COL_FB + hh)):
                num, den = nd[d * CH:(d + 1) * CH, 0:DH_A], nd[d * CH:(d + 1) * CH, DH_A:]
                if chained:
                    wa = jnp.broadcast_to(wa_all[:, col:col + 1], (CH, DH_A))
                    inter = _dot(q16, _bf(uCN_ref[c, d, hh]))
                    num = num + wa * inter[:, 0:DH_A]
                    den = den + wa * inter[:, DH_A:]
                floor = jnp.broadcast_to(floor_all[:, col:col + 1], (CH, DH_A))
                part = num / jnp.maximum(jnp.abs(den), floor)
                hs = part if hs is None else hs + part
            y_ref[rows, hh * DH_A:(hh + 1) * DH_A] = _sigmoid(o_h) * hs
        return carry

    lax.fori_loop(0, NCH, mlstm_out_body, 0)

    pj_ref[:, 0:W_B] = _dot(h_ref[...], win_ref[:, OFF_B:OFF_B + W_B])
    BQ, BK, BV = 0, D_B, 2 * D_B
    if rotary:
        def rotary_body(c, carry):
            rows = pl.ds(pl.multiple_of(c * CH, CH), CH)
            first_half = (lax.broadcasted_iota(jnp.int32, (1, D_B), 1) & (DH_B - 1)) < DH_B // 2
            for off in (BQ, BK):
                t = pj_ref[rows, off:off + D_B]
                partner = jnp.where(first_half, pltpu.roll(t, D_B - DH_B // 2, 1), pltpu.roll(t, DH_B // 2, 1))
                pj_ref[rows, off:off + D_B] = t * cos_ref[rows, :] + partner * sin_ref[rows, :]
            return carry

        lax.fori_loop(0, NCH, rotary_body, 0)

    lgam = _log_sigmoid(lg_ref[...])
    lg_f, lg_b = lgam[0:1, :], lgam[1:2, :]
    pos = lax.broadcasted_iota(jnp.int32, (CH, 1), 0).astype(F32)
    lower, upper = _causal_masks()
    rel = (lax.broadcasted_iota(jnp.int32, (CH, CH), 0) - lax.broadcasted_iota(jnp.int32, (CH, CH), 1)).astype(F32)
    for hh in range(H_B):
        lf = lg_f[:, hh * DH_B:hh * DH_B + 1]
        lb = lg_b[:, hh * DH_B:hh * DH_B + 1]
        dec_ref[hh * CH:(hh + 1) * CH, :] = (jnp.where(lower, jnp.exp(jnp.maximum(rel, 0.0) * lf), 0.0)
                                             + jnp.where(upper, jnp.exp(jnp.maximum(-rel, 0.0) * lb), 0.0))
    bd_b = _block_diag_mask(D_B, D_B, DH_B, DH_B)
    hm_b = [_lane_block_mask(D_B, DH_B, hh) for hh in range(H_B)]
    ret_scale = DH_B ** -0.5

    def ret_state_body(c, carry):
        rows = pl.ds(pl.multiple_of(c * CH, CH), CH)
        k = pj_ref[rows, BK:BK + D_B] * ret_scale
        v = pj_ref[rows, BV:BV + D_B]
        kf = k * jnp.exp((CH - 1.0 - pos) * lg_f)
        kb = k * jnp.exp(pos * lg_b)
        u = _dot(_bf(jnp.concatenate([kf, kb], axis=1).T), _bf(v))
        for d in range(2):
            u_d = u[d * D_B:(d + 1) * D_B, :]
            if chained:
                uR_ref[c, d] = u_d * bd_b
            else:
                for hh in range(H_B):
                    oR_ref[c, d, hh] = u_d[hh * DH_B:(hh + 1) * DH_B, hh * DH_B:(hh + 1) * DH_B]
        return carry

    lax.fori_loop(0, NCH, ret_state_body, 0)

    if chained:
        for d, order, lg_row in ((0, range(NCH), lg_f), (1, range(NCH - 1, -1, -1), lg_b)):
            g_col = _row_to_col(jnp.exp(float(CH) * lg_row), D_B)
            s_run = _block_diag_value(sR_ref.at[d], H_B, DH_B, DH_B)
            for idx, c in enumerate(order):
                u = uR_ref[c, d]
                uR_ref[c, d] = s_run
                if idx != NCH - 1:
                    s_run = g_col * s_run + u

    def ret_out_body(c, carry):
        rows = pl.ds(pl.multiple_of(c * CH, CH), CH)
        q = pj_ref[rows, BQ:BQ + D_B]
        k = pj_ref[rows, BK:BK + D_B] * ret_scale
        v = pj_ref[rows, BV:BV + D_B]
        kb16 = _bf(k)
        ps = [_bf(_dot_nt(_bf(q * hm_b[hh]), kb16) * dec_ref[hh * CH:(hh + 1) * CH, :]) for hh in range(H_B)]
        vst = jnp.concatenate([_bf(v * hm_b[hh]) for hh in range(H_B)], axis=0)
        yb = _dot(jnp.concatenate(ps, axis=1), vst)
        if chained:
            qf = q * jnp.exp((pos + 1.0) * lg_f)
            qb = q * jnp.exp((float(CH) - pos) * lg_b)
            s_in = jnp.concatenate([uR_ref[c, 0], uR_ref[c, 1]], axis=0)
            yb = yb + _dot(_bf(jnp.concatenate([qf, qb], axis=1)), _bf(s_in))
        y_ref[rows, D_A:D_A + D_B] = yb
        return carry

    lax.fori_loop(0, NCH, ret_out_body, 0)

    pj_ref[:, 0:W_C] = _dot(h_ref[...], win_ref[:, OFF_C:OFF_C + W_C])
    KC = H_C * DK_C
    CQ, CK, CV = 0, KC, 2 * KC
    gla_scale = DK_C ** -0.5
    bd_c = _block_diag_mask(KC, D_C, DK_C, DV_C)
    hm_ck = [_lane_block_mask(KC, DK_C, hh) for hh in range(H_C)]
    hm_cv = [_lane_block_mask(D_C, DV_C, hh) for hh in range(H_C)]

    def gla_state_body(c, carry):
        rows = pl.ds(pl.multiple_of(c * CH, CH), CH)
        k = pj_ref[rows, CK:CK + KC] * gla_scale
        v = pj_ref[rows, CV:CV + D_C]
        tot = tot_ref[c]
        khf = k * jnp.exp(tot[:, LANES:2 * LANES] - bf_ref[rows, :])
        khb = k * jnp.exp(tot[:, 2 * LANES:] - bb_ref[rows, :])
        u = _dot(_bf(jnp.concatenate([khf, khb], axis=1).T), _bf(v))
        for d in range(2):
            u_d = u[d * KC:(d + 1) * KC, :]
            if chained:
                uG_ref[c, d] = u_d * bd_c
            else:
                for hh in range(H_C):
                    oG_ref[c, d, hh] = u_d[hh * DK_C:(hh + 1) * DK_C, hh * DV_C:(hh + 1) * DV_C]
        return carry

    lax.fori_loop(0, NCH, gla_state_body, 0)

    if chained:
        for d, order in ((0, range(NCH)), (1, range(NCH - 1, -1, -1))):
            s_run = _block_diag_value(sG_ref.at[d], H_C, DK_C, DV_C)
            for idx, c in enumerate(order):
                u = uG_ref[c, d]
                uG_ref[c, d] = s_run
                if idx != NCH - 1:
                    tot_row = tot_ref[c][:, (1 + d) * LANES:(2 + d) * LANES]
                    s_run = _row_to_col(jnp.exp(tot_row), KC) * s_run + u

    def gla_out_body(c, carry):
        rows = pl.ds(pl.multiple_of(c * CH, CH), CH)
        q = pj_ref[rows, CQ:CQ + KC]
        k = pj_ref[rows, CK:CK + KC] * gla_scale
        v = pj_ref[rows, CV:CV + D_C]
        b_f, b_b = bf_ref[rows, :], bb_ref[rows, :]
        lower, upper = _causal_masks()
        qf, qb = q * jnp.exp(b_f), q * jnp.exp(b_b)
        def inter_chunk():
            if not chained:
                return jnp.zeros((CH, D_C), F32)
            s_in = jnp.concatenate([uG_ref[c, 0], uG_ref[c, 1]], axis=0)
            return _dot(_bf(jnp.concatenate([qf, qb], axis=1)), _bf(s_in))

        safe = jnp.min(tot_ref[c][:, LANES:]) > GLA_SAFE_LOG

        @pl.when(safe)
        def _():
            kf16, kb16 = _bf(k * jnp.exp(-b_f)), _bf(k * jnp.exp(-b_b))
            ps = []
            for hh in range(H_C):
                s_f = _dot_nt(_bf(qf * hm_ck[hh]), kf16)
                s_b = _dot_nt(_bf(qb * hm_ck[hh]), kb16)
                ps.append(_bf(jnp.where(lower, s_f, 0.0) + jnp.where(upper, s_b, 0.0)))
            vst = jnp.concatenate([_bf(v * hm_cv[hh]) for hh in range(H_C)], axis=0)
            yc = _dot(jnp.concatenate(ps, axis=1), vst)
            y_ref[rows, D_A + D_B:D_MODEL] = yc + inter_chunk() if chained else yc

        @pl.when(jnp.logical_not(safe))
        def _():
            y_ref[rows, D_A + D_B:D_MODEL] = inter_chunk()
            key = lax.broadcasted_iota(jnp.int32, (CH, 1), 0)
            sub = 8

            def query_body(g, carry2):
                grp = pl.ds(pl.multiple_of(c * CH + g * sub, sub), sub)
                bq_f, bq_b, qg = bf_ref[grp, :], bb_ref[grp, :], pj_ref[grp, CQ:CQ + KC]
                outs = []
                for r in range(sub):
                    i = g * sub + r
                    w_f = jnp.exp(jnp.where(key <= i, bq_f[r:r + 1, :] - b_f, NEG))
                    w_b = jnp.exp(jnp.where(key >= i, bq_b[r:r + 1, :] - b_b, NEG))
                    t = _dot(_bf(qg[r:r + 1, :] * k * (w_f + w_b)), _bf(bd_c))
                    outs.append(jnp.sum(t * v, axis=0, keepdims=True))
                y_ref[grp, D_A + D_B:D_MODEL] += jnp.concatenate(outs, axis=0)
                return carry2

            lax.fori_loop(0, CH // sub, query_body, 0)

        return carry

    lax.fori_loop(0, NCH, gla_out_body, 0)

    pj_ref[:, 0:W_Z] = _dot(h_ref[...], win_ref[:, OFF_Z:OFF_Z + W_Z])
    def gate_body(c, carry):
        rows = pl.ds(pl.multiple_of(c * CH, CH), CH)
        parts = []
        for hh in range(H_A):
            ya = y_ref[rows, hh * DH_A:(hh + 1) * DH_A]
            parts.append(ya * lax.rsqrt(jnp.mean(ya * ya, axis=1, keepdims=True) + EPS))
        ybc = y_ref[rows, D_A:D_MODEL]
        seg = _bf(_block_diag_mask(D_B + D_C, D_B + D_C, DH_B, DH_B) * (1.0 / DH_B))
        parts.append(ybc * lax.rsqrt(_exact_dot_r(ybc * ybc, seg) + EPS))
        z = pj_ref[rows, 0:W_Z]
        yg = jnp.concatenate(parts, axis=1) * hng_ref[...] * (z * _sigmoid(z))
        h_ref[rows, :] = _bf(yg)
        return carry

    lax.fori_loop(0, NCH, gate_body, 0)
    pj_ref[:, 0:D_MODEL] = _dot(h_ref[...], wout_ref[...])

    def residual_body(c, carry):
        rows = pl.ds(pl.multiple_of(c * CH, CH), CH)
        xo = x_rows(x_ref, c)[...] + mod_ref[:, 2 * D_MODEL:] * pj_ref[rows, 0:D_MODEL]
        if final:
            xo = xo * lax.rsqrt(jnp.mean(xo * xo, axis=-1, keepdims=True) + EPS) * fg_ref[...]
        x_rows(out_ref, c)[...] = xo
        return carry

    lax.fori_loop(0, NCH, residual_body, 0)


def _layer_call(l, x, mods, mod_row, norm_g, w_in_r, gate_b, lg_rows, w2_full, b2_full, hn_g, w_out_b, final_g,
                states, rot=None, chained=False, final=False):
    batch, seqlen, d = x.shape
    seqs = TOK // seqlen
    steps = batch // seqs
    assert seqs * seqlen == TOK and steps * seqs == batch and seqlen % CH == 0
    assert chained == (seqs == 1)
    assert chained or seqlen == CH

    def const(shape):
        nd = len(shape)
        return pl.BlockSpec(shape, lambda i, _nd=nd: (0,) * _nd)

    def layer_block(shape):
        nd = len(shape)
        return pl.BlockSpec((None,) + shape, lambda i, _nd=nd: (l,) + (0,) * _nd)

    in_specs = [
        pl.BlockSpec((seqs, seqlen, d), lambda i: (i, 0, 0)),
        pl.BlockSpec((None, None, 1, 3 * d),
                     (lambda i: (l, i, 0, 0)) if mod_row is None else (lambda i: (l, mod_row, 0, 0))),
        layer_block((1, d)),
        pl.BlockSpec((None, d, D_INR), lambda i: (l, 0, 0), pipeline_mode=pl.Buffered(1)),
        layer_block((1, LANES)),
        layer_block((2, D_B)),
        layer_block((LANES, 2 * LANES)),
        layer_block((1, 2 * LANES)),
        layer_block((1, d)),
        pl.BlockSpec((None, d, d), lambda i: (l, 0, 0), pipeline_mode=pl.Buffered(1)),
    ]
    args = [x, mods, norm_g.reshape(-1, 1, d), w_in_r, gate_b, lg_rows, w2_full, b2_full,
            hn_g.reshape(-1, 1, d), w_out_b]
    for cst in _gate_constants():
        in_specs.append(pl.BlockSpec(cst.shape, lambda i, _nd=cst.ndim: (0,) * _nd, pipeline_mode=pl.Buffered(1)))
        args.append(cst)
    if final:
        in_specs.append(const((1, d)))
        args.append(final_g.reshape(1, d))
    if rot is not None:
        in_specs += [pl.BlockSpec((seqlen, D_B), lambda i: (0, 0), pipeline_mode=pl.Buffered(1))] * 2
        args += list(rot)
    state_blocks = [(2, H_A, DH_A, DH_A), (2, H_A, DH_A), (1, LANES), (2, H_B, DH_B, DH_B), (2, H_C, DK_C, DV_C)]
    n_in = len(args)
    out_shape = [jax.ShapeDtypeStruct(x.shape, x.dtype)]
    out_specs = [pl.BlockSpec((seqs, seqlen, d), lambda i: (i, 0, 0))]
    aliases = {}
    for k, (blk, arr) in enumerate(zip(state_blocks, states)):
        zeros = (0,) * len(blk)
        if chained:
            in_specs.append(pl.BlockSpec((None, None) + blk, lambda i, _z=zeros: (i, l) + _z))
        else:
            in_specs.append(pl.BlockSpec(memory_space=pl.ANY))
            out_shape.append(jax.ShapeDtypeStruct(arr.shape, arr.dtype))
            out_specs.append(pl.BlockSpec((seqs, None) + blk, lambda i, _z=zeros: (i, l) + _z))
            aliases[n_in + k] = 1 + k
        args.append(arr)

    scratch = [
        pltpu.VMEM((TOK, d), BF16),
        pltpu.VMEM((TOK, W_A), F32),
        pltpu.VMEM((TOK, LANES), F32),
        pltpu.VMEM((TOK, LANES), F32),
        pltpu.VMEM((TOK, LANES), F32),
        pltpu.VMEM((TOK, LANES), F32),
        pltpu.VMEM((NCH, 1, 3 * LANES), F32),
        pltpu.VMEM((TOK, d), F32),
        pltpu.VMEM((NCH, 1, LANES), F32),
        pltpu.VMEM((NCH, 1, LANES), F32),
        pltpu.VMEM((TOK, LANES), F32),
        pltpu.VMEM((TOK, LANES), BF16),
    ]
    if chained:
        scratch += [
            pltpu.VMEM((NCH, 2, H_A, DH_A, 2 * DH_A), F32),
            pltpu.VMEM((NCH, 2, D_B, D_B), F32),
            pltpu.VMEM((NCH, 2, H_C * DK_C, D_C), F32),
        ]
    outs = pl.pallas_call(
        functools.partial(_layer_kernel, chained=chained, rotary=rot is not None, final=final),
        out_shape=out_shape,
        grid=(steps,),
        in_specs=in_specs,
        out_specs=out_specs,
        scratch_shapes=scratch,
        input_output_aliases=aliases,
        compiler_params=pltpu.CompilerParams(dimension_semantics=("arbitrary",), vmem_limit_bytes=VMEM_LIMIT),
        name=("latent" if chained else "context") + f"_layer{l}",
    )(*args)
    return outs


def _rotary_tables(seqlen):
    rows = seqlen // GRID_W
    r = jnp.repeat(jnp.arange(rows, dtype=F32), GRID_W)
    col = jnp.tile(jnp.arange(GRID_W, dtype=F32), rows)
    n_f = DH_B // 4
    freqs = ROPE_BASE ** (-jnp.arange(n_f, dtype=F32) / n_f)
    ang = jnp.concatenate([r[:, None] * freqs, col[:, None] * freqs], axis=-1)
    cos, sin = jnp.cos(ang), jnp.sin(ang)
    cos_l = jnp.tile(jnp.concatenate([cos, cos], axis=-1), (1, H_B))
    sin_l = jnp.tile(jnp.concatenate([-sin, sin], axis=-1), (1, H_B))
    return cos_l, sin_l


def _gate_lanes(m):
    z = jnp.zeros(m.shape[:-2] + (LANES,), m.dtype)
    z = z.at[..., COL_FF:COL_FF + H_A].set(m[..., 0, :]).at[..., COL_FB:COL_FB + H_A].set(m[..., 1, :])
    return z[..., None, :]


def _pack_segments():
    src = dict(zip("aq ak av ao az ag bq bk bv bz cq ck cv cz clr".split(),
                   zip(np.cumsum((0,) + IN_WIDTHS[:-1]).tolist(), IN_WIDTHS)))
    order = "aq ak av ao bq bk bv cq ck cv az bz cz ag clr".split()
    segs, dst = [], 0
    for name in order:
        s, w = src[name]
        segs.append((s, dst, w))
        dst += w
    return segs, dst


def _pack_kernel(wt_ref, o_ref):
    segs, end = _pack_segments()
    tr = o_ref.shape[0]
    small = []
    for s, dst, w in segs:
        if w % LANES == 0:
            o_ref[:, dst:dst + w] = _bf(wt_ref[s:s + w, :].T)
        else:
            small.append(wt_ref[s:s + w, :])
    small.append(jnp.zeros((D_INR - end, tr), F32))
    o_ref[:, OFF_S:] = _bf(jnp.concatenate(small, axis=0).T)


def _pack_w_in(w_in):
    depth, d, d_in = w_in.shape
    tr = 256
    return pl.pallas_call(
        _pack_kernel,
        out_shape=jax.ShapeDtypeStruct((depth, d, D_INR), BF16),
        grid=(depth, d // tr),
        in_specs=[pl.BlockSpec((None, d_in, tr), lambda l, r: (l, 0, r))],
        out_specs=pl.BlockSpec((None, tr, D_INR), lambda l, r: (l, r, 0)),
        name="pack_w_in",
    )(jnp.swapaxes(w_in, 1, 2))


def kernel(x_prompt, x_sample, state_mlstm_C, state_mlstm_n, state_mlstm_m, state_ret, state_gla, c, c_ctx,
           norm_g, w_ada, b_ada, w_in, mlstm_gate_b, ret_decay_logit, gla_w2, gla_b2, headnorm_g, w_out, final_g):
    depth = w_in.shape[0]
    dec_batch = c.shape[0]

    w_in_r = _pack_w_in(w_in)
    w_out_b = w_out.astype(BF16)
    n_g = mlstm_gate_b.shape[-1]
    gate_b = jnp.pad(mlstm_gate_b, ((0, 0), (0, LANES - n_g)))[:, None, :]
    lg_rows = jnp.repeat(ret_decay_logit, DH_B, axis=-1)
    w2_full = jnp.zeros((depth, LANES, 2 * LANES), F32)
    w2_full = w2_full.at[:, n_g:n_g + GLA_RANK, 0:LANES].set(gla_w2[:, 0])
    w2_full = w2_full.at[:, n_g + GLA_RANK:n_g + 2 * GLA_RANK, LANES:].set(gla_w2[:, 1]).astype(BF16)
    b2_full = gla_b2.reshape(depth, 1, 2 * LANES)

    cstack = jnp.zeros((16, D_MODEL), F32).at[0:dec_batch].set(c).at[dec_batch].set(c_ctx)
    mods = _modulation(cstack, w_ada, b_ada)[:, :, None, :]

    rot = _rotary_tables(x_sample.shape[1])
    cache = (state_mlstm_C, state_mlstm_n, _gate_lanes(state_mlstm_m), state_ret, state_gla)
    bp = x_prompt.shape[0]
    new = tuple(jnp.zeros((bp,) + s.shape[1:], F32) for s in cache)

    xp, xs = x_prompt, x_sample
    for l in range(depth):
        final = l == depth - 1
        common = (norm_g, w_in_r, gate_b, lg_rows, w2_full, b2_full, headnorm_g, w_out_b, final_g)
        outs = _layer_call(l, xp, mods, dec_batch, *common, states=new, final=final)
        xp, new = outs[0], tuple(outs[1:])
        xs = _layer_call(l, xs, mods, None, *common, states=cache, rot=rot, chained=True, final=final)[0]

    new_c, new_n, m_l, new_r, new_g = new
    new_m = jnp.stack([m_l[:, :, 0, COL_FF:COL_FF + H_A], m_l[:, :, 0, COL_FB:COL_FB + H_A]], axis=2)
    return (xp, xs, new_c, new_n, new_m, new_r, new_g)
```

```python
import functools

import jax
import jax.numpy as jnp
import numpy as np
from jax import lax
from jax.experimental import pallas as pl
from jax.experimental.pallas import tpu as pltpu

F32 = jnp.float32
BF16 = jnp.bfloat16

D_MODEL = 1024
H_A, DH_A = 4, 128
H_B, DH_B = 4, 64
H_C, DK_C, DV_C = 4, 32, 64
D_A, D_B, D_C = H_A * DH_A, H_B * DH_B, H_C * DV_C
GLA_RANK = 16
GLA_TAU = 16.0
GRID_W = 64
ROPE_BASE = 10000.0
EPS = 1e-6
IN_WIDTHS = (D_A, D_A, D_A, D_A, D_A, 4 * H_A, D_B, D_B, D_B, D_B, H_C * DK_C, H_C * DK_C, D_C, D_C, 2 * GLA_RANK)

LANES = 128
CH = 256
TOK = 1024
NCH = TOK // CH
VMEM_LIMIT = 60 * 1024 * 1024
NEG = -1e30
GLA_SAFE_LOG = -80.0

OFF_A, W_A = 0, 4 * D_A
OFF_B, W_B = OFF_A + W_A, 3 * D_B
OFF_C, W_C = OFF_B + W_B, 2 * H_C * DK_C + D_C
OFF_Z, W_Z = OFF_C + W_C, D_MODEL
OFF_S, W_S = OFF_Z + W_Z, LANES
D_INR = OFF_S + W_S
COL_FF, COL_FB = 4, 12


def _dot(a, b):
    return jnp.dot(a, b, preferred_element_type=F32)


def _dot_nt(a, b):
    return lax.dot_general(a, b, (((1,), (1,)), ((), ())), preferred_element_type=F32)


def _bf(x):
    return x.astype(BF16)


def _split3(x):
    hi = _bf(x)
    r1 = x - hi.astype(F32)
    mid = _bf(r1)
    lo = _bf(r1 - mid.astype(F32))
    return hi, mid, lo


def _exact_dot_l(m_bf, x):
    hi, mid, lo = _split3(x)
    return (_dot(m_bf, lo) + _dot(m_bf, mid)) + _dot(m_bf, hi)


def _exact_dot_r(x, m_bf):
    hi, mid, lo = _split3(x)
    return (_dot(lo, m_bf) + _dot(mid, m_bf)) + _dot(hi, m_bf)


def _log_sigmoid(x):
    return jnp.minimum(x, 0.0) - jnp.log(1.0 + jnp.exp(-jnp.abs(x)))


def _sigmoid(x):
    return 1.0 / (1.0 + jnp.exp(-x))


def _row_to_col(row, n):
    eye = lax.broadcasted_iota(jnp.int32, (n, n), 0) == lax.broadcasted_iota(jnp.int32, (n, n), 1)
    return jnp.sum(jnp.where(eye, row, 0.0), axis=1, keepdims=True)


def _lane_block_mask(width, block, h):
    lane = lax.broadcasted_iota(jnp.int32, (1, width), 1)
    return ((lane >= h * block) & (lane < (h + 1) * block)).astype(F32)


def _log2(n):
    assert n & (n - 1) == 0
    return n.bit_length() - 1


def _block_diag_mask(rows, cols, rblock, cblock):
    r = lax.broadcasted_iota(jnp.int32, (rows, cols), 0) >> _log2(rblock)
    c = lax.broadcasted_iota(jnp.int32, (rows, cols), 1) >> _log2(cblock)
    return (r == c).astype(F32)


def _causal_masks():
    ri = lax.broadcasted_iota(jnp.int32, (CH, CH), 0)
    cj = lax.broadcasted_iota(jnp.int32, (CH, CH), 1)
    return cj <= ri, cj >= ri


def _bwd_gate_lanes():
    lane = lax.broadcasted_iota(jnp.int32, (1, LANES), 1)
    return (lane >= COL_FB) & (lane < COL_FB + H_A)


def _block_diag_value(ref, hn, a, b):
    rows = []
    for h in range(hn):
        wide = jnp.concatenate([ref[h], jnp.zeros((a, (hn - 1) * b), F32)], axis=1)
        rows.append(wide if h == 0 else pltpu.roll(wide, h * b, 1))
    return jnp.concatenate(rows, axis=0)


def _pieces(x):
    return jnp.concatenate(_split3(x), axis=1)


def _cum_max(x, reverse):
    n = x.shape[0]
    row = lax.broadcasted_iota(jnp.int32, x.shape, 0)
    s = 1
    while s < n:
        if reverse:
            shifted = jnp.where(row < n - s, pltpu.roll(x, n - s, 0), NEG)
        else:
            shifted = jnp.where(row >= s, pltpu.roll(x, s, 0), NEG)
        x = jnp.maximum(x, shifted)
        s *= 2
    return x


GATE_BLOCK = 16
N_PIECES = 3


def _gate_constants():
    half = N_PIECES * GATE_BLOCK
    cols = [COL_FF + h for h in range(H_A)] + [COL_FB + h for h in range(H_A)]
    sp = np.zeros((2 * N_PIECES * LANES, LANES), np.float32)
    for part in range(2):
        for p in range(N_PIECES):
            for k in cols:
                sp[(part * N_PIECES + p) * LANES + k, part * half + p * GATE_BLOCK + k] = 1.0
    lane = np.arange(LANES)
    gl = np.zeros((16, LANES), np.float32)
    for hd, k in enumerate(cols):
        gl[hd] = (lane % GATE_BLOCK == k) & (lane < 2 * half)
    valid = np.isin(lane % GATE_BLOCK, cols)
    gl[2 * H_A] = valid & (lane < half)
    gl[2 * H_A + 1] = valid & (lane >= half) & (lane < 2 * half)
    return jnp.asarray(sp, BF16), jnp.asarray(gl, F32)


def _modulation_kernel(c_ref, w_ref, b_ref, o_ref):
    cv = c_ref[...]
    s = cv * _sigmoid(cv)
    o_ref[...] = _dot(_bf(s), _bf(w_ref[...])) + b_ref[...]


def _modulation(cstack, w_ada, b_ada):
    depth, d, d3 = w_ada.shape
    rows = cstack.shape[0]
    tn = 512
    return pl.pallas_call(
        _modulation_kernel,
        out_shape=jax.ShapeDtypeStruct((depth, rows, d3), F32),
        grid=(depth, d3 // tn),
        in_specs=[
            pl.BlockSpec((rows, d), lambda l, j: (0, 0)),
            pl.BlockSpec((None, d, tn), lambda l, j: (l, 0, j)),
            pl.BlockSpec((None, 1, tn), lambda l, j: (l, 0, j)),
        ],
        out_specs=pl.BlockSpec((None, rows, tn), lambda l, j: (l, 0, j)),
        name="adaln_modulation",
    )(cstack, w_ada, b_ada.reshape(depth, 1, d3))


def _layer_kernel(*refs, chained, rotary, final):
    it = iter(refs)
    x_ref, mod_ref, ng_ref, win_ref, gb_ref, lg_ref, w2_ref, b2_ref, hng_ref, wout_ref = (next(it) for _ in range(10))
    sp_ref, gl_ref = next(it), next(it)
    fg_ref = next(it) if final else None
    if rotary:
        cos_ref, sin_ref = next(it), next(it)
    sC_ref, sn_ref, sm_ref_in, sR_ref, sG_ref = (next(it) for _ in range(5))
    out_ref = next(it)
    if not chained:
        oC_ref, on_ref, om_ref, oR_ref, oG_ref = (next(it) for _ in range(5))
    (h_ref, pj_ref, sm_ref, fc_ref, bf_ref, bb_ref, tot_ref, y_ref,
     gm_ref, min_ref, cm_ref, bop_ref) = (next(it) for _ in range(12))
    dec_ref = pj_ref.at[:, W_B:W_B + CH]
    if chained:
        uCN_ref, uR_ref, uG_ref = (next(it) for _ in range(3))

    seqs, seqlen, _ = x_ref.shape

    def x_rows(ref, c):
        if isinstance(c, int):
            r0 = c * CH
            return ref.at[r0 // seqlen, r0 % seqlen:r0 % seqlen + CH, :]
        r0 = pl.multiple_of(c * CH, CH)
        return ref.at[r0 >> _log2(seqlen), pl.ds(pl.multiple_of(r0 & (seqlen - 1), CH), CH), :]

    def front_body(c, carry):
        rows = pl.ds(pl.multiple_of(c * CH, CH), CH)
        x = x_rows(x_ref, c)[...]
        shift, scale = mod_ref[:, 0:D_MODEL], mod_ref[:, D_MODEL:2 * D_MODEL]
        xn = x * lax.rsqrt(jnp.mean(x * x, axis=-1, keepdims=True) + EPS) * ng_ref[...]
        h = _bf(xn * (1.0 + scale) + shift)
        h_ref[rows, :] = h
        pj_ref[rows, 0:W_A] = _dot(h, win_ref[:, OFF_A:OFF_A + W_A])
        sm = _dot(h, win_ref[:, OFF_S:OFF_S + W_S])
        sm_ref[rows, :] = sm
        la = _log_sigmoid(_dot(_bf(sm), w2_ref[...]) + b2_ref[...]) * (1.0 / GLA_TAU)
        tri = _bf(_causal_masks()[0].astype(F32))
        ls = _log_sigmoid(sm + gb_ref[...])
        ps = _exact_dot_l(tri, jnp.concatenate([ls, la], axis=1))
        tot = ps[CH - 1:CH, :]
        fc_ref[rows, :] = jnp.where(_bwd_gate_lanes(), tot[:, 0:LANES] - ps[:, 0:LANES] + ls, ps[:, 0:LANES])
        bf_ref[rows, :] = ps[:, LANES:2 * LANES]
        bb_ref[rows, :] = tot[:, 2 * LANES:] - ps[:, 2 * LANES:] + la[:, LANES:]
        tot_ref[c] = tot
        return carry

    lax.fori_loop(0, NCH, front_body, 0)

    AQ, AK, AV, AO = 0, D_A, 2 * D_A, 3 * D_A
    k_scale = DH_A ** -0.5

    ones_cols = jnp.ones((CH, DH_A), F32)

    def mlstm_state_body(c, carry):
        rows = pl.ds(pl.multiple_of(c * CH, CH), CH)
        g_pre = sm_ref[rows, :] + gb_ref[...]
        rw = pltpu.roll(g_pre, 4, 1) - fc_ref[rows, :]
        cm_ref[rows, :] = jnp.where(_bwd_gate_lanes(), _cum_max(rw, True), _cum_max(rw, False))
        bop_ref[rows, :] = _bf(_dot(_pieces(rw), sp_ref[3 * LANES:, :]) + gl_ref[2 * H_A:2 * H_A + 1, :])
        totg = tot_ref[c][:, 0:LANES]
        g = totg + rw
        gm = jnp.max(g, axis=0, keepdims=True)
        gm_ref[c] = gm
        wgt = jnp.exp(g - gm).T
        if not chained:
            m_new = jnp.maximum(totg, gm)
            b_row = jnp.exp(gm - m_new)
            om_ref[c] = m_new
            min_ref[c] = jnp.zeros((1, LANES), F32)
        for hh in range(H_A):
            kt = (pj_ref[rows, AK + hh * DH_A:AK + (hh + 1) * DH_A] * k_scale).T
            v_h = pj_ref[rows, AV + hh * DH_A:AV + (hh + 1) * DH_A]
            lhs = jnp.concatenate([kt * wgt[COL_FF + hh:COL_FF + hh + 1, :],
                                   kt * wgt[COL_FB + hh:COL_FB + hh + 1, :]], axis=0)
            u = _dot(_bf(lhs), _bf(jnp.concatenate([v_h, ones_cols], axis=1)))
            for d, col in ((0, COL_FF), (1, COL_FB)):
                u_d = u[d * DH_A:(d + 1) * DH_A, :]
                if chained:
                    uCN_ref[c, d, hh] = u_d
                else:
                    fin = b_row[:, col + hh:col + hh + 1] * u_d
                    oC_ref[c, d, hh] = fin[:, 0:DH_A]
                    on_ref[c, d, hh:hh + 1, :] = fin[:, DH_A:].T[0:1, :]
        return carry

    def chunk_loop(state_body, out_body):
        if chained:
            return lax.fori_loop(0, NCH, out_body, 0)
        return lax.fori_loop(0, NCH, lambda c, carry: out_body(c, state_body(c, carry)), 0)

    if chained:
        lax.fori_loop(0, NCH, mlstm_state_body, 0)

    if chained:
        for d, order, col in ((0, range(NCH), COL_FF), (1, range(NCH - 1, -1, -1), COL_FB)):
            m_run = sm_ref_in[...]
            cn_run = [jnp.concatenate([sC_ref[d, hh],
                                       jnp.broadcast_to(sn_ref[d, hh:hh + 1, :], (DH_A, DH_A)).T], axis=1)
                      for hh in range(H_A)]
            for idx, c in enumerate(order):
                last = idx == NCH - 1
                if not last:
                    totg, gm = tot_ref[c][:, 0:LANES], gm_ref[c]
                    m_new = jnp.maximum(totg + m_run, gm)
                    a_row, b_row = jnp.exp(totg + m_run - m_new), jnp.exp(gm - m_new)
                for hh in range(H_A):
                    u = uCN_ref[c, d, hh]
                    uCN_ref[c, d, hh] = cn_run[hh]
                    if not last:
                        cn_run[hh] = (a_row[:, col + hh:col + hh + 1] * cn_run[hh]
                                      + b_row[:, col + hh:col + hh + 1] * u)
                min_ref[c] = m_run if d == 0 else jnp.where(_bwd_gate_lanes(), m_run, min_ref[c])
                if not last:
                    m_run = m_new

    def mlstm_out_body(c, carry):
        rows = pl.ds(pl.multiple_of(c * CH, CH), CH)
        lower, upper = _causal_masks()
        m_prev = min_ref[c]
        cmx = jnp.maximum(cm_ref[rows, :], m_prev)
        mi = fc_ref[rows, :] + cmx
        a_op = _bf(_dot(_pieces(-cmx), sp_ref[0:3 * LANES, :]) + gl_ref[2 * H_A + 1:2 * H_A + 2, :])
        b_op = bop_ref[rows, :]
        floor_all = jnp.exp(-mi)
        wa_all = jnp.exp(m_prev - cmx)
        for hh in range(H_A):
            q16 = _bf(pj_ref[rows, AQ + hh * DH_A:AQ + (hh + 1) * DH_A])
            k_h = pj_ref[rows, AK + hh * DH_A:AK + (hh + 1) * DH_A] * k_scale
            v_h = pj_ref[rows, AV + hh * DH_A:AV + (hh + 1) * DH_A]
            o_h = pj_ref[rows, AO + hh * DH_A:AO + (hh + 1) * DH_A]
            s = _dot_nt(q16, _bf(k_h))
            qs = []
            for d, msk in ((0, lower), (1, upper)):
                hd = d * H_A + hh
                dm = _dot_nt(a_op * _bf(gl_ref[hd:hd + 1, :]), b_op)
                qs.append(_bf(s * jnp.exp(jnp.where(msk, dm, NEG))))
            nd = _dot(jnp.concatenate(qs, axis=0), _bf(jnp.concatenate([v_h, ones_cols], axis=1)))
            hs = None
            for d, col in ((0, COL_FF + hh), (1, COL_FB + hh)):
                num, den = nd[d * CH:(d + 1) * CH, 0:DH_A], nd[d * CH:(d + 1) * CH, DH_A:]
                if chained:
                    wa = jnp.broadcast_to(wa_all[:, col:col + 1], (CH, DH_A))
                    inter = _dot(q16, _bf(uCN_ref[c, d, hh]))
                    num = num + wa * inter[:, 0:DH_A]
                    den = den + wa * inter[:, DH_A:]
                floor = jnp.broadcast_to(floor_all[:, col:col + 1], (CH, DH_A))
                part = num / jnp.maximum(jnp.abs(den), floor)
                hs = part if hs is None else hs + part
            y_ref[rows, hh * DH_A:(hh + 1) * DH_A] = _sigmoid(o_h) * hs
        return carry

    chunk_loop(mlstm_state_body, mlstm_out_body)

    pj_ref[:, 0:W_B] = _dot(h_ref[...], win_ref[:, OFF_B:OFF_B + W_B])
    BQ, BK, BV = 0, D_B, 2 * D_B
    if rotary:
        def rotary_body(c, carry):
            rows = pl.ds(pl.multiple_of(c * CH, CH), CH)
            first_half = (lax.broadcasted_iota(jnp.int32, (1, D_B), 1) & (DH_B - 1)) < DH_B // 2
            for off in (BQ, BK):
                t = pj_ref[rows, off:off + D_B]
                partner = jnp.where(first_half, pltpu.roll(t, D_B - DH_B // 2, 1), pltpu.roll(t, DH_B // 2, 1))
                pj_ref[rows, off:off + D_B] = t * cos_ref[rows, :] + partner * sin_ref[rows, :]
            return carry

        lax.fori_loop(0, NCH, rotary_body, 0)

    lgam = _log_sigmoid(lg_ref[...])
    lg_f, lg_b = lgam[0:1, :], lgam[1:2, :]
    pos = lax.broadcasted_iota(jnp.int32, (CH, 1), 0).astype(F32)
    lower, upper = _causal_masks()
    rel = (lax.broadcasted_iota(jnp.int32, (CH, CH), 0) - lax.broadcasted_iota(jnp.int32, (CH, CH), 1)).astype(F32)
    for hh in range(H_B):
        lf = lg_f[:, hh * DH_B:hh * DH_B + 1]
        lb = lg_b[:, hh * DH_B:hh * DH_B + 1]
        dec_ref[hh * CH:(hh + 1) * CH, :] = (jnp.where(lower, jnp.exp(jnp.maximum(rel, 0.0) * lf), 0.0)
                                             + jnp.where(upper, jnp.exp(jnp.maximum(-rel, 0.0) * lb), 0.0))
    bd_b = _block_diag_mask(D_B, D_B, DH_B, DH_B)
    hm_b = [_lane_block_mask(D_B, DH_B, hh) for hh in range(H_B)]
    ret_scale = DH_B ** -0.5

    def ret_state_body(c, carry):
        rows = pl.ds(pl.multiple_of(c * CH, CH), CH)
        k = pj_ref[rows, BK:BK + D_B] * ret_scale
        v = pj_ref[rows, BV:BV + D_B]
        kf = k * jnp.exp((CH - 1.0 - pos) * lg_f)
        kb = k * jnp.exp(pos * lg_b)
        u = _dot(_bf(jnp.concatenate([kf, kb], axis=1).T), _bf(v))
        for d in range(2):
            u_d = u[d * D_B:(d + 1) * D_B, :]
            if chained:
                uR_ref[c, d] = u_d * bd_b
            else:
                for hh in range(H_B):
                    oR_ref[c, d, hh] = u_d[hh * DH_B:(hh + 1) * DH_B, hh * DH_B:(hh + 1) * DH_B]
        return carry

    if chained:
        lax.fori_loop(0, NCH, ret_state_body, 0)

    if chained:
        for d, order, lg_row in ((0, range(NCH), lg_f), (1, range(NCH - 1, -1, -1), lg_b)):
            g_col = _row_to_col(jnp.exp(float(CH) * lg_row), D_B)
            s_run = _block_diag_value(sR_ref.at[d], H_B, DH_B, DH_B)
            for idx, c in enumerate(order):
                u = uR_ref[c, d]
                uR_ref[c, d] = s_run
                if idx != NCH - 1:
                    s_run = g_col * s_run + u

    def ret_out_body(c, carry):
        rows = pl.ds(pl.multiple_of(c * CH, CH), CH)
        q = pj_ref[rows, BQ:BQ + D_B]
        k = pj_ref[rows, BK:BK + D_B] * ret_scale
        v = pj_ref[rows, BV:BV + D_B]
        kb16 = _bf(k)
        ps = [_bf(_dot_nt(_bf(q * hm_b[hh]), kb16) * dec_ref[hh * CH:(hh + 1) * CH, :]) for hh in range(H_B)]
        vst = jnp.concatenate([_bf(v * hm_b[hh]) for hh in range(H_B)], axis=0)
        yb = _dot(jnp.concatenate(ps, axis=1), vst)
        if chained:
            qf = q * jnp.exp((pos + 1.0) * lg_f)
            qb = q * jnp.exp((float(CH) - pos) * lg_b)
            s_in = jnp.concatenate([uR_ref[c, 0], uR_ref[c, 1]], axis=0)
            yb = yb + _dot(_bf(jnp.concatenate([qf, qb], axis=1)), _bf(s_in))
        y_ref[rows, D_A:D_A + D_B] = yb
        return carry

    chunk_loop(ret_state_body, ret_out_body)

    pj_ref[:, 0:W_C] = _dot(h_ref[...], win_ref[:, OFF_C:OFF_C + W_C])
    KC = H_C * DK_C
    CQ, CK, CV = 0, KC, 2 * KC
    gla_scale = DK_C ** -0.5
    bd_c = _block_diag_mask(KC, D_C, DK_C, DV_C)
    hm_ck = [_lane_block_mask(KC, DK_C, hh) for hh in range(H_C)]
    hm_cv = [_lane_block_mask(D_C, DV_C, hh) for hh in range(H_C)]

    def gla_state_body(c, carry):
        rows = pl.ds(pl.multiple_of(c * CH, CH), CH)
        k = pj_ref[rows, CK:CK + KC] * gla_scale
        v = pj_ref[rows, CV:CV + D_C]
        tot = tot_ref[c]
        khf = k * jnp.exp(tot[:, LANES:2 * LANES] - bf_ref[rows, :])
        khb = k * jnp.exp(tot[:, 2 * LANES:] - bb_ref[rows, :])
        u = _dot(_bf(jnp.concatenate([khf, khb], axis=1).T), _bf(v))
        for d in range(2):
            u_d = u[d * KC:(d + 1) * KC, :]
            if chained:
                uG_ref[c, d] = u_d * bd_c
            else:
                for hh in range(H_C):
                    oG_ref[c, d, hh] = u_d[hh * DK_C:(hh + 1) * DK_C, hh * DV_C:(hh + 1) * DV_C]
        return carry

    if chained:
        lax.fori_loop(0, NCH, gla_state_body, 0)

    if chained:
        for d, order in ((0, range(NCH)), (1, range(NCH - 1, -1, -1))):
            s_run = _block_diag_value(sG_ref.at[d], H_C, DK_C, DV_C)
            for idx, c in enumerate(order):
                u = uG_ref[c, d]
                uG_ref[c, d] = s_run
                if idx != NCH - 1:
                    tot_row = tot_ref[c][:, (1 + d) * LANES:(2 + d) * LANES]
                    s_run = _row_to_col(jnp.exp(tot_row), KC) * s_run + u

    def gla_out_body(c, carry):
        rows = pl.ds(pl.multiple_of(c * CH, CH), CH)
        q = pj_ref[rows, CQ:CQ + KC]
        k = pj_ref[rows, CK:CK + KC] * gla_scale
        v = pj_ref[rows, CV:CV + D_C]
        b_f, b_b = bf_ref[rows, :], bb_ref[rows, :]
        lower, upper = _causal_masks()
        qf, qb = q * jnp.exp(b_f), q * jnp.exp(b_b)
        def inter_chunk():
            if not chained:
                return jnp.zeros((CH, D_C), F32)
            s_in = jnp.concatenate([uG_ref[c, 0], uG_ref[c, 1]], axis=0)
            return _dot(_bf(jnp.concatenate([qf, qb], axis=1)), _bf(s_in))

        safe = jnp.min(tot_ref[c][:, LANES:]) > GLA_SAFE_LOG

        @pl.when(safe)
        def _():
            kf16, kb16 = _bf(k * jnp.exp(-b_f)), _bf(k * jnp.exp(-b_b))
            ps = []
            for hh in range(H_C):
                s_f = _dot_nt(_bf(qf * hm_ck[hh]), kf16)
                s_b = _dot_nt(_bf(qb * hm_ck[hh]), kb16)
                ps.append(_bf(jnp.where(lower, s_f, 0.0) + jnp.where(upper, s_b, 0.0)))
            vst = jnp.concatenate([_bf(v * hm_cv[hh]) for hh in range(H_C)], axis=0)
            yc = _dot(jnp.concatenate(ps, axis=1), vst)
            y_ref[rows, D_A + D_B:D_MODEL] = yc + inter_chunk() if chained else yc

        @pl.when(jnp.logical_not(safe))
        def _():
            y_ref[rows, D_A + D_B:D_MODEL] = inter_chunk()
            key = lax.broadcasted_iota(jnp.int32, (CH, 1), 0)
            sub = 8

            def query_body(g, carry2):
                grp = pl.ds(pl.multiple_of(c * CH + g * sub, sub), sub)
                bq_f, bq_b, qg = bf_ref[grp, :], bb_ref[grp, :], pj_ref[grp, CQ:CQ + KC]
                outs = []
                for r in range(sub):
                    i = g * sub + r
                    w_f = jnp.exp(jnp.where(key <= i, bq_f[r:r + 1, :] - b_f, NEG))
                    w_b = jnp.exp(jnp.where(key >= i, bq_b[r:r + 1, :] - b_b, NEG))
                    t = _dot(_bf(qg[r:r + 1, :] * k * (w_f + w_b)), _bf(bd_c))
                    outs.append(jnp.sum(t * v, axis=0, keepdims=True))
                y_ref[grp, D_A + D_B:D_MODEL] += jnp.concatenate(outs, axis=0)
                return carry2

            lax.fori_loop(0, CH // sub, query_body, 0)

        return carry

    chunk_loop(gla_state_body, gla_out_body)

    def gate_proj(c):
        rows = slice(c * CH, (c + 1) * CH)
        pj_ref[rows, 0:W_Z] = _dot(h_ref[rows, :], win_ref[:, OFF_Z:OFF_Z + W_Z])

    def head_norm_gate(c):
        rows = slice(c * CH, (c + 1) * CH)
        parts = []
        for hh in range(H_A):
            ya = y_ref[rows, hh * DH_A:(hh + 1) * DH_A]
            parts.append(ya * lax.rsqrt(jnp.mean(ya * ya, axis=1, keepdims=True) + EPS))
        ybc = y_ref[rows, D_A:D_MODEL]
        seg = _bf(_block_diag_mask(D_B + D_C, D_B + D_C, DH_B, DH_B) * (1.0 / DH_B))
        parts.append(ybc * lax.rsqrt(_exact_dot_r(ybc * ybc, seg) + EPS))
        z = pj_ref[rows, 0:W_Z]
        yg = jnp.concatenate(parts, axis=1) * hng_ref[...] * (z * _sigmoid(z))
        h_ref[rows, :] = _bf(yg)

    def out_proj_residual(c):
        rows = slice(c * CH, (c + 1) * CH)
        xo = x_rows(x_ref, c)[...] + mod_ref[:, 2 * D_MODEL:] * _dot(h_ref[rows, :], wout_ref[...])
        if final:
            xo = xo * lax.rsqrt(jnp.mean(xo * xo, axis=-1, keepdims=True) + EPS) * fg_ref[...]
        x_rows(out_ref, c)[...] = xo

    gate_proj(0)
    for c in range(NCH):
        if c + 1 < NCH:
            gate_proj(c + 1)
        head_norm_gate(c)
        if c > 0:
            out_proj_residual(c - 1)
    out_proj_residual(NCH - 1)


def _layer_call(l, x, mods, mod_row, norm_g, w_in_r, gate_b, lg_rows, w2_full, b2_full, hn_g, w_out_b, final_g,
                states, rot=None, chained=False, final=False):
    batch, seqlen, d = x.shape
    seqs = TOK // seqlen
    steps = batch // seqs
    assert seqs * seqlen == TOK and steps * seqs == batch and seqlen % CH == 0
    assert chained == (seqs == 1)
    assert chained or seqlen == CH

    def const(shape):
        nd = len(shape)
        return pl.BlockSpec(shape, lambda i, _nd=nd: (0,) * _nd)

    def layer_block(shape):
        nd = len(shape)
        return pl.BlockSpec((None,) + shape, lambda i, _nd=nd: (l,) + (0,) * _nd)

    in_specs = [
        pl.BlockSpec((seqs, seqlen, d), lambda i: (i, 0, 0)),
        pl.BlockSpec((None, None, 1, 3 * d),
                     (lambda i: (l, i, 0, 0)) if mod_row is None else (lambda i: (l, mod_row, 0, 0))),
        layer_block((1, d)),
        pl.BlockSpec((None, d, D_INR), lambda i: (l, 0, 0), pipeline_mode=pl.Buffered(1)),
        layer_block((1, LANES)),
        layer_block((2, D_B)),
        layer_block((LANES, 2 * LANES)),
        layer_block((1, 2 * LANES)),
        layer_block((1, d)),
        pl.BlockSpec((None, d, d), lambda i: (l, 0, 0), pipeline_mode=pl.Buffered(1)),
    ]
    args = [x, mods, norm_g.reshape(-1, 1, d), w_in_r, gate_b, lg_rows, w2_full, b2_full,
            hn_g.reshape(-1, 1, d), w_out_b]
    for cst in _gate_constants():
        in_specs.append(pl.BlockSpec(cst.shape, lambda i, _nd=cst.ndim: (0,) * _nd, pipeline_mode=pl.Buffered(1)))
        args.append(cst)
    if final:
        in_specs.append(const((1, d)))
        args.append(final_g.reshape(1, d))
    if rot is not None:
        in_specs += [pl.BlockSpec((seqlen, D_B), lambda i: (0, 0), pipeline_mode=pl.Buffered(1))] * 2
        args += list(rot)
    state_blocks = [(2, H_A, DH_A, DH_A), (2, H_A, DH_A), (1, LANES), (2, H_B, DH_B, DH_B), (2, H_C, DK_C, DV_C)]
    n_in = len(args)
    out_shape = [jax.ShapeDtypeStruct(x.shape, x.dtype)]
    out_specs = [pl.BlockSpec((seqs, seqlen, d), lambda i: (i, 0, 0))]
    aliases = {}
    for k, (blk, arr) in enumerate(zip(state_blocks, states)):
        zeros = (0,) * len(blk)
        if chained:
            in_specs.append(pl.BlockSpec((None, None) + blk, lambda i, _z=zeros: (i, l) + _z))
        else:
            in_specs.append(pl.BlockSpec(memory_space=pl.ANY))
            out_shape.append(jax.ShapeDtypeStruct(arr.shape, arr.dtype))
            out_specs.append(pl.BlockSpec((seqs, None) + blk, lambda i, _z=zeros: (i, l) + _z))
            aliases[n_in + k] = 1 + k
        args.append(arr)

    scratch = [
        pltpu.VMEM((TOK, d), BF16),
        pltpu.VMEM((TOK, W_A), F32),
        pltpu.VMEM((TOK, LANES), F32),
        pltpu.VMEM((TOK, LANES), F32),
        pltpu.VMEM((TOK, LANES), F32),
        pltpu.VMEM((TOK, LANES), F32),
        pltpu.VMEM((NCH, 1, 3 * LANES), F32),
        pltpu.VMEM((TOK, d), F32),
        pltpu.VMEM((NCH, 1, LANES), F32),
        pltpu.VMEM((NCH, 1, LANES), F32),
        pltpu.VMEM((TOK, LANES), F32),
        pltpu.VMEM((TOK, LANES), BF16),
    ]
    if chained:
        scratch += [
            pltpu.VMEM((NCH, 2, H_A, DH_A, 2 * DH_A), F32),
            pltpu.VMEM((NCH, 2, D_B, D_B), F32),
            pltpu.VMEM((NCH, 2, H_C * DK_C, D_C), F32),
        ]
    outs = pl.pallas_call(
        functools.partial(_layer_kernel, chained=chained, rotary=rot is not None, final=final),
        out_shape=out_shape,
        grid=(steps,),
        in_specs=in_specs,
        out_specs=out_specs,
        scratch_shapes=scratch,
        input_output_aliases=aliases,
        compiler_params=pltpu.CompilerParams(dimension_semantics=("arbitrary",), vmem_limit_bytes=VMEM_LIMIT),
        name=("latent" if chained else "context") + f"_layer{l}",
    )(*args)
    return outs


def _rotary_tables(seqlen):
    rows = seqlen // GRID_W
    r = jnp.repeat(jnp.arange(rows, dtype=F32), GRID_W)
    col = jnp.tile(jnp.arange(GRID_W, dtype=F32), rows)
    n_f = DH_B // 4
    freqs = ROPE_BASE ** (-jnp.arange(n_f, dtype=F32) / n_f)
    ang = jnp.concatenate([r[:, None] * freqs, col[:, None] * freqs], axis=-1)
    cos, sin = jnp.cos(ang), jnp.sin(ang)
    cos_l = jnp.tile(jnp.concatenate([cos, cos], axis=-1), (1, H_B))
    sin_l = jnp.tile(jnp.concatenate([-sin, sin], axis=-1), (1, H_B))
    return cos_l, sin_l


def _gate_lanes(m):
    z = jnp.zeros(m.shape[:-2] + (LANES,), m.dtype)
    z = z.at[..., COL_FF:COL_FF + H_A].set(m[..., 0, :]).at[..., COL_FB:COL_FB + H_A].set(m[..., 1, :])
    return z[..., None, :]


def _pack_segments():
    src = dict(zip("aq ak av ao az ag bq bk bv bz cq ck cv cz clr".split(),
                   zip(np.cumsum((0,) + IN_WIDTHS[:-1]).tolist(), IN_WIDTHS)))
    order = "aq ak av ao bq bk bv cq ck cv az bz cz ag clr".split()
    segs, dst = [], 0
    for name in order:
        s, w = src[name]
        segs.append((s, dst, w))
        dst += w
    return segs, dst


def _pack_kernel(wt_ref, o_ref):
    segs, end = _pack_segments()
    tr = o_ref.shape[0]
    small = []
    for s, dst, w in segs:
        if w % LANES == 0:
            o_ref[:, dst:dst + w] = _bf(wt_ref[s:s + w, :].T)
        else:
            small.append(wt_ref[s:s + w, :])
    small.append(jnp.zeros((D_INR - end, tr), F32))
    o_ref[:, OFF_S:] = _bf(jnp.concatenate(small, axis=0).T)


def _pack_w_in(w_in):
    depth, d, d_in = w_in.shape
    tr = 256
    return pl.pallas_call(
        _pack_kernel,
        out_shape=jax.ShapeDtypeStruct((depth, d, D_INR), BF16),
        grid=(depth, d // tr),
        in_specs=[pl.BlockSpec((None, d_in, tr), lambda l, r: (l, 0, r))],
        out_specs=pl.BlockSpec((None, tr, D_INR), lambda l, r: (l, r, 0)),
        name="pack_w_in",
    )(jnp.swapaxes(w_in, 1, 2))


def kernel(x_prompt, x_sample, state_mlstm_C, state_mlstm_n, state_mlstm_m, state_ret, state_gla, c, c_ctx,
           norm_g, w_ada, b_ada, w_in, mlstm_gate_b, ret_decay_logit, gla_w2, gla_b2, headnorm_g, w_out, final_g):
    depth = w_in.shape[0]
    dec_batch = c.shape[0]

    w_in_r = _pack_w_in(w_in)
    w_out_b = w_out.astype(BF16)
    n_g = mlstm_gate_b.shape[-1]
    gate_b = jnp.pad(mlstm_gate_b, ((0, 0), (0, LANES - n_g)))[:, None, :]
    lg_rows = jnp.repeat(ret_decay_logit, DH_B, axis=-1)
    w2_full = jnp.zeros((depth, LANES, 2 * LANES), F32)
    w2_full = w2_full.at[:, n_g:n_g + GLA_RANK, 0:LANES].set(gla_w2[:, 0])
    w2_full = w2_full.at[:, n_g + GLA_RANK:n_g + 2 * GLA_RANK, LANES:].set(gla_w2[:, 1]).astype(BF16)
    b2_full = gla_b2.reshape(depth, 1, 2 * LANES)

    cstack = jnp.zeros((16, D_MODEL), F32).at[0:dec_batch].set(c).at[dec_batch].set(c_ctx)
    mods = _modulation(cstack, w_ada, b_ada)[:, :, None, :]

    rot = _rotary_tables(x_sample.shape[1])
    cache = (state_mlstm_C, state_mlstm_n, _gate_lanes(state_mlstm_m), state_ret, state_gla)
    bp = x_prompt.shape[0]
    new = tuple(jnp.zeros((bp,) + s.shape[1:], F32) for s in cache)

    xp, xs = x_prompt, x_sample
    for l in range(depth):
        final = l == depth - 1
        common = (norm_g, w_in_r, gate_b, lg_rows, w2_full, b2_full, headnorm_g, w_out_b, final_g)
        outs = _layer_call(l, xp, mods, dec_batch, *common, states=new, final=final)
        xp, new = outs[0], tuple(outs[1:])
        xs = _layer_call(l, xs, mods, None, *common, states=cache, rot=rot, chained=True, final=final)[0]

    new_c, new_n, m_l, new_r, new_g = new
    new_m = jnp.stack([m_l[:, :, 0, COL_FF:COL_FF + H_A], m_l[:, :, 0, COL_FB:COL_FB + H_A]], axis=2)
    return (xp, xs, new_c, new_n, new_m, new_r, new_g)
```

```python
import functools

import jax
import jax.numpy as jnp
import numpy as np
from jax import lax
from jax.experimental import pallas as pl
from jax.experimental.pallas import tpu as pltpu

F32 = jnp.float32
BF16 = jnp.bfloat16

D_MODEL = 1024
H_A, DH_A = 4, 128
H_B, DH_B = 4, 64
H_C, DK_C, DV_C = 4, 32, 64
D_A, D_B, D_C = H_A * DH_A, H_B * DH_B, H_C * DV_C
GLA_RANK = 16
GLA_TAU = 16.0
GRID_W = 64
ROPE_BASE = 10000.0
EPS = 1e-6
IN_WIDTHS = (D_A, D_A, D_A, D_A, D_A, 4 * H_A, D_B, D_B, D_B, D_B, H_C * DK_C, H_C * DK_C, D_C, D_C, 2 * GLA_RANK)

LANES = 128
CH = 256
TOK = 1024
NCH = TOK // CH
VMEM_LIMIT = 60 * 1024 * 1024
NEG = -1e30
GLA_SAFE_LOG = -80.0

OFF_A, W_A = 0, 4 * D_A
OFF_S, W_S = OFF_A + W_A, LANES
OFF_B, W_B = OFF_S + W_S, 3 * D_B
OFF_C, W_C = OFF_B + W_B, 2 * H_C * DK_C + D_C
OFF_Z, W_Z = OFF_C + W_C, D_MODEL
D_INR = OFF_Z + W_Z
COL_FF, COL_FB = 4, 12


def _dot(a, b):
    return jnp.dot(a, b, preferred_element_type=F32)


def _dot_nt(a, b):
    return lax.dot_general(a, b, (((1,), (1,)), ((), ())), preferred_element_type=F32)


def _bf(x):
    return x.astype(BF16)


def _split3(x):
    hi = _bf(x)
    r1 = x - hi.astype(F32)
    mid = _bf(r1)
    lo = _bf(r1 - mid.astype(F32))
    return hi, mid, lo


def _exact_dot_l(m_bf, x):
    hi, mid, lo = _split3(x)
    return (_dot(m_bf, lo) + _dot(m_bf, mid)) + _dot(m_bf, hi)


def _exact_dot_r(x, m_bf):
    hi, mid, lo = _split3(x)
    return (_dot(lo, m_bf) + _dot(mid, m_bf)) + _dot(hi, m_bf)


def _log_sigmoid(x):
    return jnp.minimum(x, 0.0) - jnp.log(1.0 + jnp.exp(-jnp.abs(x)))


def _sigmoid(x):
    return 1.0 / (1.0 + jnp.exp(-x))


def _row_to_col(row, n):
    eye = lax.broadcasted_iota(jnp.int32, (n, n), 0) == lax.broadcasted_iota(jnp.int32, (n, n), 1)
    return jnp.sum(jnp.where(eye, row, 0.0), axis=1, keepdims=True)


def _lane_block_mask(width, block, h):
    lane = lax.broadcasted_iota(jnp.int32, (1, width), 1)
    return ((lane >= h * block) & (lane < (h + 1) * block)).astype(F32)


def _log2(n):
    assert n & (n - 1) == 0
    return n.bit_length() - 1


def _block_diag_mask(rows, cols, rblock, cblock):
    r = lax.broadcasted_iota(jnp.int32, (rows, cols), 0) >> _log2(rblock)
    c = lax.broadcasted_iota(jnp.int32, (rows, cols), 1) >> _log2(cblock)
    return (r == c).astype(F32)


def _causal_masks():
    ri = lax.broadcasted_iota(jnp.int32, (CH, CH), 0)
    cj = lax.broadcasted_iota(jnp.int32, (CH, CH), 1)
    return cj <= ri, cj >= ri


def _bwd_gate_lanes():
    lane = lax.broadcasted_iota(jnp.int32, (1, LANES), 1)
    return (lane >= COL_FB) & (lane < COL_FB + H_A)


def _block_diag_value(ref, hn, a, b):
    rows = []
    for h in range(hn):
        wide = jnp.concatenate([ref[h], jnp.zeros((a, (hn - 1) * b), F32)], axis=1)
        rows.append(wide if h == 0 else pltpu.roll(wide, h * b, 1))
    return jnp.concatenate(rows, axis=0)


def _pieces(x):
    return jnp.concatenate(_split3(x), axis=1)


def _cum_max(x, reverse):
    n = x.shape[0]
    row = lax.broadcasted_iota(jnp.int32, x.shape, 0)
    s = 1
    while s < n:
        if reverse:
            shifted = jnp.where(row < n - s, pltpu.roll(x, n - s, 0), NEG)
        else:
            shifted = jnp.where(row >= s, pltpu.roll(x, s, 0), NEG)
        x = jnp.maximum(x, shifted)
        s *= 2
    return x


GATE_BLOCK = 16
N_PIECES = 3


def _gate_constants():
    half = N_PIECES * GATE_BLOCK
    cols = [COL_FF + h for h in range(H_A)] + [COL_FB + h for h in range(H_A)]
    sp = np.zeros((2 * N_PIECES * LANES, LANES), np.float32)
    for part in range(2):
        for p in range(N_PIECES):
            for k in cols:
                sp[(part * N_PIECES + p) * LANES + k, part * half + p * GATE_BLOCK + k] = 1.0
    lane = np.arange(LANES)
    gl = np.zeros((16, LANES), np.float32)
    for hd, k in enumerate(cols):
        gl[hd] = (lane % GATE_BLOCK == k) & (lane < 2 * half)
    valid = np.isin(lane % GATE_BLOCK, cols)
    gl[2 * H_A] = valid & (lane < half)
    gl[2 * H_A + 1] = valid & (lane >= half) & (lane < 2 * half)
    return jnp.asarray(sp, BF16), jnp.asarray(gl, F32)


def _modulation_kernel(c_ref, w_ref, b_ref, o_ref):
    cv = c_ref[...]
    s = cv * _sigmoid(cv)
    o_ref[...] = _dot(_bf(s), _bf(w_ref[...])) + b_ref[...]


def _modulation(cstack, w_ada, b_ada):
    depth, d, d3 = w_ada.shape
    rows = cstack.shape[0]
    tn = 512
    return pl.pallas_call(
        _modulation_kernel,
        out_shape=jax.ShapeDtypeStruct((depth, rows, d3), F32),
        grid=(depth, d3 // tn),
        in_specs=[
            pl.BlockSpec((rows, d), lambda l, j: (0, 0)),
            pl.BlockSpec((None, d, tn), lambda l, j: (l, 0, j)),
            pl.BlockSpec((None, 1, tn), lambda l, j: (l, 0, j)),
        ],
        out_specs=pl.BlockSpec((None, rows, tn), lambda l, j: (l, 0, j)),
        name="adaln_modulation",
    )(cstack, w_ada, b_ada.reshape(depth, 1, d3))


def _layer_kernel(*refs, chained, rotary, final):
    it = iter(refs)
    x_ref, mod_ref, ng_ref, win_ref, gb_ref, lg_ref, w2_ref, b2_ref, hng_ref, wout_ref = (next(it) for _ in range(10))
    sp_ref, gl_ref = next(it), next(it)
    fg_ref = next(it) if final else None
    if rotary:
        cos_ref, sin_ref = next(it), next(it)
    sC_ref, sn_ref, sm_ref_in, sR_ref, sG_ref = (next(it) for _ in range(5))
    out_ref = next(it)
    if not chained:
        oC_ref, on_ref, om_ref, oR_ref, oG_ref = (next(it) for _ in range(5))
    (h_ref, pj_ref, sm_ref, fc_ref, bf_ref, bb_ref, tot_ref, y_ref,
     gm_ref, min_ref, cm_ref, bop_ref) = (next(it) for _ in range(12))
    dec_ref = pj_ref.at[:, W_B:W_B + CH]
    if chained:
        uCN_ref, uR_ref, uG_ref = (next(it) for _ in range(3))

    seqs, seqlen, _ = x_ref.shape

    def x_rows(ref, c):
        if isinstance(c, int):
            r0 = c * CH
            return ref.at[r0 // seqlen, r0 % seqlen:r0 % seqlen + CH, :]
        r0 = pl.multiple_of(c * CH, CH)
        return ref.at[r0 >> _log2(seqlen), pl.ds(pl.multiple_of(r0 & (seqlen - 1), CH), CH), :]

    def norm_rows(c):
        x = x_rows(x_ref, c)[...]
        shift, scale = mod_ref[:, 0:D_MODEL], mod_ref[:, D_MODEL:2 * D_MODEL]
        xn = x * lax.rsqrt(jnp.mean(x * x, axis=-1, keepdims=True) + EPS) * ng_ref[...]
        h_ref[c * CH:(c + 1) * CH, :] = _bf(xn * (1.0 + scale) + shift)

    def front_rows(c):
        rows = slice(c * CH, (c + 1) * CH)
        proj = _dot(h_ref[rows, :], win_ref[:, OFF_A:OFF_S + W_S])
        pj_ref[rows, 0:W_A] = proj[:, 0:W_A]
        sm = proj[:, W_A:]
        sm_ref[rows, :] = sm
        la = _log_sigmoid(_dot(_bf(sm), w2_ref[...]) + b2_ref[...]) * (1.0 / GLA_TAU)
        tri = _bf(_causal_masks()[0].astype(F32))
        ls = _log_sigmoid(sm + gb_ref[...])
        ps = _exact_dot_l(tri, jnp.concatenate([ls, la], axis=1))
        tot = ps[CH - 1:CH, :]
        fc_ref[rows, :] = jnp.where(_bwd_gate_lanes(), tot[:, 0:LANES] - ps[:, 0:LANES] + ls, ps[:, 0:LANES])
        bf_ref[rows, :] = ps[:, LANES:2 * LANES]
        bb_ref[rows, :] = tot[:, 2 * LANES:] - ps[:, 2 * LANES:] + la[:, LANES:]
        tot_ref[c] = tot

    norm_rows(0)
    for c in range(NCH):
        if c + 1 < NCH:
            norm_rows(c + 1)
        front_rows(c)

    AQ, AK, AV, AO = 0, D_A, 2 * D_A, 3 * D_A
    k_scale = DH_A ** -0.5

    ones_cols = jnp.ones((CH, DH_A), F32)

    def mlstm_state_body(c, carry):
        rows = pl.ds(pl.multiple_of(c * CH, CH), CH)
        g_pre = sm_ref[rows, :] + gb_ref[...]
        rw = pltpu.roll(g_pre, 4, 1) - fc_ref[rows, :]
        cm_ref[rows, :] = jnp.where(_bwd_gate_lanes(), _cum_max(rw, True), _cum_max(rw, False))
        bop_ref[rows, :] = _bf(_dot(_pieces(rw), sp_ref[3 * LANES:, :]) + gl_ref[2 * H_A:2 * H_A + 1, :])
        totg = tot_ref[c][:, 0:LANES]
        g = totg + rw
        gm = jnp.max(g, axis=0, keepdims=True)
        gm_ref[c] = gm
        wgt = jnp.exp(g - gm).T
        y_ref[rows, D_A:D_MODEL] = _dot(h_ref[rows, :], win_ref[:, OFF_C:OFF_C + W_C])
        if not chained:
            m_new = jnp.maximum(totg, gm)
            b_row = jnp.exp(gm - m_new)
            om_ref[c] = m_new
            min_ref[c] = jnp.zeros((1, LANES), F32)
        for hh in range(H_A):
            kt = (pj_ref[rows, AK + hh * DH_A:AK + (hh + 1) * DH_A] * k_scale).T
            v_h = pj_ref[rows, AV + hh * DH_A:AV + (hh + 1) * DH_A]
            lhs = jnp.concatenate([kt * wgt[COL_FF + hh:COL_FF + hh + 1, :],
                                   kt * wgt[COL_FB + hh:COL_FB + hh + 1, :]], axis=0)
            u = _dot(_bf(lhs), _bf(jnp.concatenate([v_h, ones_cols], axis=1)))
            for d, col in ((0, COL_FF), (1, COL_FB)):
                u_d = u[d * DH_A:(d + 1) * DH_A, :]
                if chained:
                    uCN_ref[c, d, hh] = u_d
                else:
                    fin = b_row[:, col + hh:col + hh + 1] * u_d
                    oC_ref[c, d, hh] = fin[:, 0:DH_A]
                    on_ref[c, d, hh:hh + 1, :] = fin[:, DH_A:].T[0:1, :]
        return carry

    def chunk_loop(state_body, out_body):
        if chained:
            return lax.fori_loop(0, NCH, out_body, 0)
        return lax.fori_loop(0, NCH, lambda c, carry: out_body(c, state_body(c, carry)), 0)

    if chained:
        lax.fori_loop(0, NCH, mlstm_state_body, 0)

    if chained:
        for d, order, col in ((0, range(NCH), COL_FF), (1, range(NCH - 1, -1, -1), COL_FB)):
            m_run = sm_ref_in[...]
            cn_run = [jnp.concatenate([sC_ref[d, hh],
                                       jnp.broadcast_to(sn_ref[d, hh:hh + 1, :], (DH_A, DH_A)).T], axis=1)
                      for hh in range(H_A)]
            for idx, c in enumerate(order):
                last = idx == NCH - 1
                if not last:
                    totg, gm = tot_ref[c][:, 0:LANES], gm_ref[c]
                    m_new = jnp.maximum(totg + m_run, gm)
                    a_row, b_row = jnp.exp(totg + m_run - m_new), jnp.exp(gm - m_new)
                for hh in range(H_A):
                    u = uCN_ref[c, d, hh]
                    uCN_ref[c, d, hh] = cn_run[hh]
                    if not last:
                        cn_run[hh] = (a_row[:, col + hh:col + hh + 1] * cn_run[hh]
                                      + b_row[:, col + hh:col + hh + 1] * u)
                min_ref[c] = m_run if d == 0 else jnp.where(_bwd_gate_lanes(), m_run, min_ref[c])
                if not last:
                    m_run = m_new

    def mlstm_out_body(c, carry):
        rows = pl.ds(pl.multiple_of(c * CH, CH), CH)
        lower, upper = _causal_masks()
        m_prev = min_ref[c]
        cmx = jnp.maximum(cm_ref[rows, :], m_prev)
        mi = fc_ref[rows, :] + cmx
        a_op = _bf(_dot(_pieces(-cmx), sp_ref[0:3 * LANES, :]) + gl_ref[2 * H_A + 1:2 * H_A + 2, :])
        b_op = bop_ref[rows, :]
        floor_all = jnp.exp(-mi)
        wa_all = jnp.exp(m_prev - cmx)
        for hh in range(H_A):
            q16 = _bf(pj_ref[rows, AQ + hh * DH_A:AQ + (hh + 1) * DH_A])
            k_h = pj_ref[rows, AK + hh * DH_A:AK + (hh + 1) * DH_A] * k_scale
            v_h = pj_ref[rows, AV + hh * DH_A:AV + (hh + 1) * DH_A]
            o_h = pj_ref[rows, AO + hh * DH_A:AO + (hh + 1) * DH_A]
            s = _dot_nt(q16, _bf(k_h))
            qs = []
            for d, msk in ((0, lower), (1, upper)):
                hd = d * H_A + hh
                dm = _dot_nt(a_op * _bf(gl_ref[hd:hd + 1, :]), b_op)
                qs.append(_bf(s * jnp.exp(jnp.where(msk, dm, NEG))))
            nd = _dot(jnp.concatenate(qs, axis=0), _bf(jnp.concatenate([v_h, ones_cols], axis=1)))
            hs = None
            for d, col in ((0, COL_FF + hh), (1, COL_FB + hh)):
                num, den = nd[d * CH:(d + 1) * CH, 0:DH_A], nd[d * CH:(d + 1) * CH, DH_A:]
                if chained:
                    wa = jnp.broadcast_to(wa_all[:, col:col + 1], (CH, DH_A))
                    inter = _dot(q16, _bf(uCN_ref[c, d, hh]))
                    num = num + wa * inter[:, 0:DH_A]
                    den = den + wa * inter[:, DH_A:]
                floor = jnp.broadcast_to(floor_all[:, col:col + 1], (CH, DH_A))
                part = num / jnp.maximum(jnp.abs(den), floor)
                hs = part if hs is None else hs + part
            y_ref[rows, hh * DH_A:(hh + 1) * DH_A] = _sigmoid(o_h) * hs
        return carry

    chunk_loop(mlstm_state_body, mlstm_out_body)

    BQ, BK, BV = 0, D_B, 2 * D_B

    def ret_proj_body(c, carry):
        rows = pl.ds(pl.multiple_of(c * CH, CH), CH)
        pj_ref[rows, 0:W_B] = _dot(h_ref[rows, :], win_ref[:, OFF_B:OFF_B + W_B])
        return carry

    def rotary_body(c, carry):
        rows = pl.ds(pl.multiple_of(c * CH, CH), CH)
        first_half = (lax.broadcasted_iota(jnp.int32, (1, D_B), 1) & (DH_B - 1)) < DH_B // 2
        for off in (BQ, BK):
            t = pj_ref[rows, off:off + D_B]
            partner = jnp.where(first_half, pltpu.roll(t, D_B - DH_B // 2, 1), pltpu.roll(t, DH_B // 2, 1))
            pj_ref[rows, off:off + D_B] = t * cos_ref[rows, :] + partner * sin_ref[rows, :]
        return carry

    lgam = _log_sigmoid(lg_ref[...])
    lg_f, lg_b = lgam[0:1, :], lgam[1:2, :]
    pos = lax.broadcasted_iota(jnp.int32, (CH, 1), 0).astype(F32)
    lower, upper = _causal_masks()
    rel = (lax.broadcasted_iota(jnp.int32, (CH, CH), 0) - lax.broadcasted_iota(jnp.int32, (CH, CH), 1)).astype(F32)
    for hh in range(H_B):
        lf = lg_f[:, hh * DH_B:hh * DH_B + 1]
        lb = lg_b[:, hh * DH_B:hh * DH_B + 1]
        dec_ref[hh * CH:(hh + 1) * CH, :] = (jnp.where(lower, jnp.exp(jnp.maximum(rel, 0.0) * lf), 0.0)
                                             + jnp.where(upper, jnp.exp(jnp.maximum(-rel, 0.0) * lb), 0.0))
    bd_b = _block_diag_mask(D_B, D_B, DH_B, DH_B)
    hm_b = [_lane_block_mask(D_B, DH_B, hh) for hh in range(H_B)]
    ret_scale = DH_B ** -0.5

    def ret_state_body(c, carry):
        rows = pl.ds(pl.multiple_of(c * CH, CH), CH)
        k = pj_ref[rows, BK:BK + D_B] * ret_scale
        v = pj_ref[rows, BV:BV + D_B]
        kf = k * jnp.exp((CH - 1.0 - pos) * lg_f)
        kb = k * jnp.exp(pos * lg_b)
        u = _dot(_bf(jnp.concatenate([kf, kb], axis=1).T), _bf(v))
        for d in range(2):
            u_d = u[d * D_B:(d + 1) * D_B, :]
            if chained:
                uR_ref[c, d] = u_d * bd_b
            else:
                for hh in range(H_B):
                    oR_ref[c, d, hh] = u_d[hh * DH_B:(hh + 1) * DH_B, hh * DH_B:(hh + 1) * DH_B]
        return carry

    def ret_scan():
        for d, order, lg_row in ((0, range(NCH), lg_f), (1, range(NCH - 1, -1, -1), lg_b)):
            g_col = _row_to_col(jnp.exp(float(CH) * lg_row), D_B)
            s_run = _block_diag_value(sR_ref.at[d], H_B, DH_B, DH_B)
            for idx, c in enumerate(order):
                u = uR_ref[c, d]
                uR_ref[c, d] = s_run
                if idx != NCH - 1:
                    s_run = g_col * s_run + u

    def ret_out_body(c, carry):
        rows = pl.ds(pl.multiple_of(c * CH, CH), CH)
        q = pj_ref[rows, BQ:BQ + D_B]
        k = pj_ref[rows, BK:BK + D_B] * ret_scale
        v = pj_ref[rows, BV:BV + D_B]
        kb16 = _bf(k)
        ps = [_bf(_dot_nt(_bf(q * hm_b[hh]), kb16) * dec_ref[hh * CH:(hh + 1) * CH, :]) for hh in range(H_B)]
        vst = jnp.concatenate([_bf(v * hm_b[hh]) for hh in range(H_B)], axis=0)
        yb = _dot(jnp.concatenate(ps, axis=1), vst)
        if chained:
            qf = q * jnp.exp((pos + 1.0) * lg_f)
            qb = q * jnp.exp((float(CH) - pos) * lg_b)
            s_in = jnp.concatenate([uR_ref[c, 0], uR_ref[c, 1]], axis=0)
            yb = yb + _dot(_bf(jnp.concatenate([qf, qb], axis=1)), _bf(s_in))
        y_ref[rows, D_A:D_A + D_B] = yb
        return carry

    KC = H_C * DK_C
    CQ, CK, CV = D_A, D_A + KC, D_A + 2 * KC
    gla_scale = DK_C ** -0.5
    bd_c = _block_diag_mask(KC, D_C, DK_C, DV_C)
    hm_ck = [_lane_block_mask(KC, DK_C, hh) for hh in range(H_C)]
    hm_cv = [_lane_block_mask(D_C, DV_C, hh) for hh in range(H_C)]

    def gla_state_body(c, carry):
        rows = pl.ds(pl.multiple_of(c * CH, CH), CH)
        k = y_ref[rows, CK:CK + KC] * gla_scale
        v = y_ref[rows, CV:CV + D_C]
        tot = tot_ref[c]
        khf = k * jnp.exp(tot[:, LANES:2 * LANES] - bf_ref[rows, :])
        khb = k * jnp.exp(tot[:, 2 * LANES:] - bb_ref[rows, :])
        u = _dot(_bf(jnp.concatenate([khf, khb], axis=1).T), _bf(v))
        for d in range(2):
            u_d = u[d * KC:(d + 1) * KC, :]
            if chained:
                uG_ref[c, d] = u_d * bd_c
            else:
                for hh in range(H_C):
                    oG_ref[c, d, hh] = u_d[hh * DK_C:(hh + 1) * DK_C, hh * DV_C:(hh + 1) * DV_C]
        return carry

    def gla_state_ret_proj(c, carry):
        return gla_state_body(c, ret_proj_body(c, carry))

    if chained:
        lax.fori_loop(0, NCH, gla_state_ret_proj, 0)

    if chained:
        for d, order in ((0, range(NCH)), (1, range(NCH - 1, -1, -1))):
            s_run = _block_diag_value(sG_ref.at[d], H_C, DK_C, DV_C)
            for idx, c in enumerate(order):
                u = uG_ref[c, d]
                uG_ref[c, d] = s_run
                if idx != NCH - 1:
                    tot_row = tot_ref[c][:, (1 + d) * LANES:(2 + d) * LANES]
                    s_run = _row_to_col(jnp.exp(tot_row), KC) * s_run + u

    def gla_out_body(c, carry):
        rows = pl.ds(pl.multiple_of(c * CH, CH), CH)

        def operands():
            q = y_ref[rows, CQ:CQ + KC]
            k = y_ref[rows, CK:CK + KC] * gla_scale
            v = y_ref[rows, CV:CV + D_C]
            b_f, b_b = bf_ref[rows, :], bb_ref[rows, :]
            return q * jnp.exp(b_f), q * jnp.exp(b_b), k, v, b_f, b_b

        def inter_chunk(qf, qb):
            if not chained:
                return jnp.zeros((CH, D_C), F32)
            s_in = jnp.concatenate([uG_ref[c, 0], uG_ref[c, 1]], axis=0)
            return _dot(_bf(jnp.concatenate([qf, qb], axis=1)), _bf(s_in))

        safe = jnp.min(tot_ref[c][:, LANES:]) > GLA_SAFE_LOG

        @pl.when(safe)
        def _():
            qf, qb, k, v, b_f, b_b = operands()
            lower, upper = _causal_masks()
            kf16, kb16 = _bf(k * jnp.exp(-b_f)), _bf(k * jnp.exp(-b_b))
            ps = []
            for hh in range(H_C):
                s_f = _dot_nt(_bf(qf * hm_ck[hh]), kf16)
                s_b = _dot_nt(_bf(qb * hm_ck[hh]), kb16)
                ps.append(_bf(jnp.where(lower, s_f, 0.0) + jnp.where(upper, s_b, 0.0)))
            vst = jnp.concatenate([_bf(v * hm_cv[hh]) for hh in range(H_C)], axis=0)
            yc = _dot(jnp.concatenate(ps, axis=1), vst)
            y_ref[rows, D_A + D_B:D_MODEL] = yc + inter_chunk(qf, qb) if chained else yc
            if rotary:
                rotary_body(c, 0)

        @pl.when(jnp.logical_not(safe))
        def _():
            qf, qb, k, v, b_f, b_b = operands()
            y_ref[rows, D_A + D_B:D_MODEL] = inter_chunk(qf, qb)
            if rotary:
                rotary_body(c, 0)
            key = lax.broadcasted_iota(jnp.int32, (CH, 1), 0)
            sub = 8

            def query_body(g, carry2):
                grp = pl.ds(pl.multiple_of(c * CH + g * sub, sub), sub)
                bq_f, bq_b, qg = bf_ref[grp, :], bb_ref[grp, :], y_ref[grp, CQ:CQ + KC]
                outs = []
                for r in range(sub):
                    i = g * sub + r
                    w_f = jnp.exp(jnp.where(key <= i, bq_f[r:r + 1, :] - b_f, NEG))
                    w_b = jnp.exp(jnp.where(key >= i, bq_b[r:r + 1, :] - b_b, NEG))
                    t = _dot(_bf(qg[r:r + 1, :] * k * (w_f + w_b)), _bf(bd_c))
                    outs.append(jnp.sum(t * v, axis=0, keepdims=True))
                y_ref[grp, D_A + D_B:D_MODEL] += jnp.concatenate(outs, axis=0)
                return carry2

            lax.fori_loop(0, CH // sub, query_body, 0)

        return carry

    chunk_loop(gla_state_ret_proj, gla_out_body)

    if chained:
        lax.fori_loop(0, NCH, ret_state_body, 0)
        ret_scan()
    chunk_loop(ret_state_body, ret_out_body)

    def gate_proj(c):
        rows = slice(c * CH, (c + 1) * CH)
        pj_ref[rows, 0:W_Z] = _dot(h_ref[rows, :], win_ref[:, OFF_Z:OFF_Z + W_Z])

    def head_norm_gate(c):
        rows = slice(c * CH, (c + 1) * CH)
        parts = []
        for hh in range(H_A):
            ya = y_ref[rows, hh * DH_A:(hh + 1) * DH_A]
            parts.append(ya * lax.rsqrt(jnp.mean(ya * ya, axis=1, keepdims=True) + EPS))
        ybc = y_ref[rows, D_A:D_MODEL]
        seg = _bf(_block_diag_mask(D_B + D_C, D_B + D_C, DH_B, DH_B) * (1.0 / DH_B))
        parts.append(ybc * lax.rsqrt(_exact_dot_r(ybc * ybc, seg) + EPS))
        z = pj_ref[rows, 0:W_Z]
        yg = jnp.concatenate(parts, axis=1) * hng_ref[...] * (z * _sigmoid(z))
        h_ref[rows, :] = _bf(yg)

    def out_proj_residual(c):
        rows = slice(c * CH, (c + 1) * CH)
        xo = x_rows(x_ref, c)[...] + mod_ref[:, 2 * D_MODEL:] * _dot(h_ref[rows, :], wout_ref[...])
        if final:
            xo = xo * lax.rsqrt(jnp.mean(xo * xo, axis=-1, keepdims=True) + EPS) * fg_ref[...]
        x_rows(out_ref, c)[...] = xo

    gate_proj(0)
    for c in range(NCH):
        if c + 1 < NCH:
            gate_proj(c + 1)
        head_norm_gate(c)
        if c > 0:
            out_proj_residual(c - 1)
    out_proj_residual(NCH - 1)


def _layer_call(l, x, mods, mod_row, norm_g, w_in_r, gate_b, lg_rows, w2_full, b2_full, hn_g, w_out_b, final_g,
                states, rot=None, chained=False, final=False):
    batch, seqlen, d = x.shape
    seqs = TOK // seqlen
    steps = batch // seqs
    assert seqs * seqlen == TOK and steps * seqs == batch and seqlen % CH == 0
    assert chained == (seqs == 1)
    assert chained or seqlen == CH

    def const(shape):
        nd = len(shape)
        return pl.BlockSpec(shape, lambda i, _nd=nd: (0,) * _nd)

    def layer_block(shape):
        nd = len(shape)
        return pl.BlockSpec((None,) + shape, lambda i, _nd=nd: (l,) + (0,) * _nd)

    in_specs = [
        pl.BlockSpec((seqs, seqlen, d), lambda i: (i, 0, 0)),
        pl.BlockSpec((None, None, 1, 3 * d),
                     (lambda i: (l, i, 0, 0)) if mod_row is None else (lambda i: (l, mod_row, 0, 0))),
        layer_block((1, d)),
        pl.BlockSpec((None, d, D_INR), lambda i: (l, 0, 0), pipeline_mode=pl.Buffered(1)),
        layer_block((1, LANES)),
        layer_block((2, D_B)),
        layer_block((LANES, 2 * LANES)),
        layer_block((1, 2 * LANES)),
        layer_block((1, d)),
        pl.BlockSpec((None, d, d), lambda i: (l, 0, 0), pipeline_mode=pl.Buffered(1)),
    ]
    args = [x, mods, norm_g.reshape(-1, 1, d), w_in_r, gate_b, lg_rows, w2_full, b2_full,
            hn_g.reshape(-1, 1, d), w_out_b]
    for cst in _gate_constants():
        in_specs.append(pl.BlockSpec(cst.shape, lambda i, _nd=cst.ndim: (0,) * _nd, pipeline_mode=pl.Buffered(1)))
        args.append(cst)
    if final:
        in_specs.append(const((1, d)))
        args.append(final_g.reshape(1, d))
    if rot is not None:
        in_specs += [pl.BlockSpec((seqlen, D_B), lambda i: (0, 0), pipeline_mode=pl.Buffered(1))] * 2
        args += list(rot)
    state_blocks = [(2, H_A, DH_A, DH_A), (2, H_A, DH_A), (1, LANES), (2, H_B, DH_B, DH_B), (2, H_C, DK_C, DV_C)]
    n_in = len(args)
    out_shape = [jax.ShapeDtypeStruct(x.shape, x.dtype)]
    out_specs = [pl.BlockSpec((seqs, seqlen, d), lambda i: (i, 0, 0))]
    aliases = {}
    for k, (blk, arr) in enumerate(zip(state_blocks, states)):
        zeros = (0,) * len(blk)
        if chained:
            in_specs.append(pl.BlockSpec((None, None) + blk, lambda i, _z=zeros: (i, l) + _z))
        else:
            in_specs.append(pl.BlockSpec(memory_space=pl.ANY))
            out_shape.append(jax.ShapeDtypeStruct(arr.shape, arr.dtype))
            out_specs.append(pl.BlockSpec((seqs, None) + blk, lambda i, _z=zeros: (i, l) + _z))
            aliases[n_in + k] = 1 + k
        args.append(arr)

    scratch = [
        pltpu.VMEM((TOK, d), BF16),
        pltpu.VMEM((TOK, W_A), F32),
        pltpu.VMEM((TOK, LANES), F32),
        pltpu.VMEM((TOK, LANES), F32),
        pltpu.VMEM((TOK, LANES), F32),
        pltpu.VMEM((TOK, LANES), F32),
        pltpu.VMEM((NCH, 1, 3 * LANES), F32),
        pltpu.VMEM((TOK, d), F32),
        pltpu.VMEM((NCH, 1, LANES), F32),
        pltpu.VMEM((NCH, 1, LANES), F32),
        pltpu.VMEM((TOK, LANES), F32),
        pltpu.VMEM((TOK, LANES), BF16),
    ]
    if chained:
        scratch += [
            pltpu.VMEM((NCH, 2, H_A, DH_A, 2 * DH_A), F32),
            pltpu.VMEM((NCH, 2, D_B, D_B), F32),
            pltpu.VMEM((NCH, 2, H_C * DK_C, D_C), F32),
        ]
    outs = pl.pallas_call(
        functools.partial(_layer_kernel, chained=chained, rotary=rot is not None, final=final),
        out_shape=out_shape,
        grid=(steps,),
        in_specs=in_specs,
        out_specs=out_specs,
        scratch_shapes=scratch,
        input_output_aliases=aliases,
        compiler_params=pltpu.CompilerParams(dimension_semantics=("arbitrary",), vmem_limit_bytes=VMEM_LIMIT),
        name=("latent" if chained else "context") + f"_layer{l}",
    )(*args)
    return outs


def _rotary_tables(seqlen):
    rows = seqlen // GRID_W
    r = jnp.repeat(jnp.arange(rows, dtype=F32), GRID_W)
    col = jnp.tile(jnp.arange(GRID_W, dtype=F32), rows)
    n_f = DH_B // 4
    freqs = ROPE_BASE ** (-jnp.arange(n_f, dtype=F32) / n_f)
    ang = jnp.concatenate([r[:, None] * freqs, col[:, None] * freqs], axis=-1)
    cos, sin = jnp.cos(ang), jnp.sin(ang)
    cos_l = jnp.tile(jnp.concatenate([cos, cos], axis=-1), (1, H_B))
    sin_l = jnp.tile(jnp.concatenate([-sin, sin], axis=-1), (1, H_B))
    return cos_l, sin_l


def _gate_lanes(m):
    z = jnp.zeros(m.shape[:-2] + (LANES,), m.dtype)
    z = z.at[..., COL_FF:COL_FF + H_A].set(m[..., 0, :]).at[..., COL_FB:COL_FB + H_A].set(m[..., 1, :])
    return z[..., None, :]


def _pack_segments():
    src = dict(zip("aq ak av ao az ag bq bk bv bz cq ck cv cz clr".split(),
                   zip(np.cumsum((0,) + IN_WIDTHS[:-1]).tolist(), IN_WIDTHS)))
    segs = []
    for names, dst in (("aq ak av ao", OFF_A), ("ag clr", OFF_S), ("bq bk bv", OFF_B), ("cq ck cv", OFF_C),
                       ("az bz cz", OFF_Z)):
        for name in names.split():
            s, w = src[name]
            segs.append((s, dst, w))
            dst += w
    return segs


def _pack_kernel(wt_ref, o_ref):
    tr = o_ref.shape[0]
    small = []
    for s, dst, w in _pack_segments():
        if w % LANES == 0:
            o_ref[:, dst:dst + w] = _bf(wt_ref[s:s + w, :].T)
        else:
            small.append(wt_ref[s:s + w, :])
    small.append(jnp.zeros((W_S - sum(t.shape[0] for t in small), tr), F32))
    o_ref[:, OFF_S:OFF_S + W_S] = _bf(jnp.concatenate(small, axis=0).T)


def _pack_w_in(w_in):
    depth, d, d_in = w_in.shape
    tr = 256
    return pl.pallas_call(
        _pack_kernel,
        out_shape=jax.ShapeDtypeStruct((depth, d, D_INR), BF16),
        grid=(depth, d // tr),
        in_specs=[pl.BlockSpec((None, d_in, tr), lambda l, r: (l, 0, r))],
        out_specs=pl.BlockSpec((None, tr, D_INR), lambda l, r: (l, r, 0)),
        name="pack_w_in",
    )(jnp.swapaxes(w_in, 1, 2))


def kernel(x_prompt, x_sample, state_mlstm_C, state_mlstm_n, state_mlstm_m, state_ret, state_gla, c, c_ctx,
           norm_g, w_ada, b_ada, w_in, mlstm_gate_b, ret_decay_logit, gla_w2, gla_b2, headnorm_g, w_out, final_g):
    depth = w_in.shape[0]
    dec_batch = c.shape[0]

    w_in_r = _pack_w_in(w_in)
    w_out_b = w_out.astype(BF16)
    n_g = mlstm_gate_b.shape[-1]
    gate_b = jnp.pad(mlstm_gate_b, ((0, 0), (0, LANES - n_g)))[:, None, :]
    lg_rows = jnp.repeat(ret_decay_logit, DH_B, axis=-1)
    w2_full = jnp.zeros((depth, LANES, 2 * LANES), F32)
    w2_full = w2_full.at[:, n_g:n_g + GLA_RANK, 0:LANES].set(gla_w2[:, 0])
    w2_full = w2_full.at[:, n_g + GLA_RANK:n_g + 2 * GLA_RANK, LANES:].set(gla_w2[:, 1]).astype(BF16)
    b2_full = gla_b2.reshape(depth, 1, 2 * LANES)

    cstack = jnp.zeros((16, D_MODEL), F32).at[0:dec_batch].set(c).at[dec_batch].set(c_ctx)
    mods = _modulation(cstack, w_ada, b_ada)[:, :, None, :]

    rot = _rotary_tables(x_sample.shape[1])
    cache = (state_mlstm_C, state_mlstm_n, _gate_lanes(state_mlstm_m), state_ret, state_gla)
    bp = x_prompt.shape[0]
    new = tuple(jnp.zeros((bp,) + s.shape[1:], F32) for s in cache)

    xp, xs = x_prompt, x_sample
    for l in range(depth):
        final = l == depth - 1
        common = (norm_g, w_in_r, gate_b, lg_rows, w2_full, b2_full, headnorm_g, w_out_b, final_g)
        outs = _layer_call(l, xp, mods, dec_batch, *common, states=new, final=final)
        xp, new = outs[0], tuple(outs[1:])
        xs = _layer_call(l, xs, mods, None, *common, states=cache, rot=rot, chained=True, final=final)[0]

    new_c, new_n, m_l, new_r, new_g = new
    new_m = jnp.stack([m_l[:, :, 0, COL_FF:COL_FF + H_A], m_l[:, :, 0, COL_FB:COL_FB + H_A]], axis=2)
    return (xp, xs, new_c, new_n, new_m, new_r, new_g)
```

```python
import functools

import jax
import jax.numpy as jnp
import numpy as np
from jax import lax
from jax.experimental import pallas as pl
from jax.experimental.pallas import tpu as pltpu

F32 = jnp.float32
BF16 = jnp.bfloat16

D_MODEL = 1024
H_A, DH_A = 4, 128
H_B, DH_B = 4, 64
H_C, DK_C, DV_C = 4, 32, 64
D_A, D_B, D_C = H_A * DH_A, H_B * DH_B, H_C * DV_C
GLA_RANK = 16
GLA_TAU = 16.0
GRID_W = 64
ROPE_BASE = 10000.0
EPS = 1e-6
IN_WIDTHS = (D_A, D_A, D_A, D_A, D_A, 4 * H_A, D_B, D_B, D_B, D_B, H_C * DK_C, H_C * DK_C, D_C, D_C, 2 * GLA_RANK)

LANES = 128
CH = 256
TOK = 1024
NCH = TOK // CH
CHUNK_UNROLL = 2
VMEM_LIMIT = 60 * 1024 * 1024
NEG = -1e30
GLA_SAFE_LOG = -80.0

OFF_A, W_A = 0, 4 * D_A
OFF_S, W_S = OFF_A + W_A, LANES
OFF_B, W_B = OFF_S + W_S, 3 * D_B
OFF_C, W_C = OFF_B + W_B, 2 * H_C * DK_C + D_C
OFF_Z, W_Z = OFF_C + W_C, D_MODEL
D_INR = OFF_Z + W_Z
COL_FF, COL_FB = 4, 12


def _dot(a, b):
    return jnp.dot(a, b, preferred_element_type=F32)


def _dot_nt(a, b):
    return lax.dot_general(a, b, (((1,), (1,)), ((), ())), preferred_element_type=F32)


def _bf(x):
    return x.astype(BF16)


def _split3(x):
    hi = _bf(x)
    r1 = x - hi.astype(F32)
    mid = _bf(r1)
    lo = _bf(r1 - mid.astype(F32))
    return hi, mid, lo


def _exact_dot_l(m_bf, x):
    hi, mid, lo = _split3(x)
    return (_dot(m_bf, lo) + _dot(m_bf, mid)) + _dot(m_bf, hi)


def _exact_dot_r(x, m_bf):
    hi, mid, lo = _split3(x)
    return (_dot(lo, m_bf) + _dot(mid, m_bf)) + _dot(hi, m_bf)


def _log_sigmoid(x):
    return jnp.minimum(x, 0.0) - jnp.log(1.0 + jnp.exp(-jnp.abs(x)))


def _sigmoid(x):
    return 1.0 / (1.0 + jnp.exp(-x))


def _row_to_col(row, n):
    eye = lax.broadcasted_iota(jnp.int32, (n, n), 0) == lax.broadcasted_iota(jnp.int32, (n, n), 1)
    return jnp.sum(jnp.where(eye, row, 0.0), axis=1, keepdims=True)


def _lane_block_mask(width, block, h):
    lane = lax.broadcasted_iota(jnp.int32, (1, width), 1)
    return ((lane >= h * block) & (lane < (h + 1) * block)).astype(F32)


def _log2(n):
    assert n & (n - 1) == 0
    return n.bit_length() - 1


def _block_diag_mask(rows, cols, rblock, cblock):
    r = lax.broadcasted_iota(jnp.int32, (rows, cols), 0) >> _log2(rblock)
    c = lax.broadcasted_iota(jnp.int32, (rows, cols), 1) >> _log2(cblock)
    return (r == c).astype(F32)


def _causal_masks():
    ri = lax.broadcasted_iota(jnp.int32, (CH, CH), 0)
    cj = lax.broadcasted_iota(jnp.int32, (CH, CH), 1)
    return cj <= ri, cj >= ri


def _bwd_gate_lanes():
    lane = lax.broadcasted_iota(jnp.int32, (1, LANES), 1)
    return (lane >= COL_FB) & (lane < COL_FB + H_A)


def _block_diag_value(ref, hn, a, b):
    rows = []
    for h in range(hn):
        wide = jnp.concatenate([ref[h], jnp.zeros((a, (hn - 1) * b), F32)], axis=1)
        rows.append(wide if h == 0 else pltpu.roll(wide, h * b, 1))
    return jnp.concatenate(rows, axis=0)


def _pieces(x):
    return jnp.concatenate(_split3(x), axis=1)


def _cum_max(x, reverse):
    n = x.shape[0]
    row = lax.broadcasted_iota(jnp.int32, x.shape, 0)
    s = 1
    while s < n:
        if reverse:
            shifted = jnp.where(row < n - s, pltpu.roll(x, n - s, 0), NEG)
        else:
            shifted = jnp.where(row >= s, pltpu.roll(x, s, 0), NEG)
        x = jnp.maximum(x, shifted)
        s *= 2
    return x


GATE_BLOCK = 16
N_PIECES = 3


def _gate_constants():
    half = N_PIECES * GATE_BLOCK
    cols = [COL_FF + h for h in range(H_A)] + [COL_FB + h for h in range(H_A)]
    sp = np.zeros((2 * N_PIECES * LANES, LANES), np.float32)
    for part in range(2):
        for p in range(N_PIECES):
            for k in cols:
                sp[(part * N_PIECES + p) * LANES + k, part * half + p * GATE_BLOCK + k] = 1.0
    lane = np.arange(LANES)
    gl = np.zeros((16, LANES), np.float32)
    for hd, k in enumerate(cols):
        gl[hd] = (lane % GATE_BLOCK == k) & (lane < 2 * half)
    valid = np.isin(lane % GATE_BLOCK, cols)
    gl[2 * H_A] = valid & (lane < half)
    gl[2 * H_A + 1] = valid & (lane >= half) & (lane < 2 * half)
    return jnp.asarray(sp, BF16), jnp.asarray(gl, F32)


def _modulation_kernel(c_ref, w_ref, b_ref, o_ref):
    cv = c_ref[...]
    s = cv * _sigmoid(cv)
    o_ref[...] = _dot(_bf(s), _bf(w_ref[...])) + b_ref[...]


def _modulation(cstack, w_ada, b_ada):
    depth, d, d3 = w_ada.shape
    rows = cstack.shape[0]
    tn = 512
    return pl.pallas_call(
        _modulation_kernel,
        out_shape=jax.ShapeDtypeStruct((depth, rows, d3), F32),
        grid=(depth, d3 // tn),
        in_specs=[
            pl.BlockSpec((rows, d), lambda l, j: (0, 0)),
            pl.BlockSpec((None, d, tn), lambda l, j: (l, 0, j)),
            pl.BlockSpec((None, 1, tn), lambda l, j: (l, 0, j)),
        ],
        out_specs=pl.BlockSpec((None, rows, tn), lambda l, j: (l, 0, j)),
        name="adaln_modulation",
    )(cstack, w_ada, b_ada.reshape(depth, 1, d3))


def _layer_kernel(*refs, chained, rotary, final, state_inputs):
    it = iter(refs)
    x_ref, mod_ref, ng_ref, win_ref, gb_ref, lg_ref, w2_ref, b2_ref, hng_ref, wout_ref = (next(it) for _ in range(10))
    sp_ref, gl_ref = next(it), next(it)
    fg_ref = next(it) if final else None
    if rotary:
        cos_ref, sin_ref = next(it), next(it)
    if state_inputs:
        sC_ref, sn_ref, sm_ref_in, sR_ref, sG_ref = (next(it) for _ in range(5))
    out_ref = next(it)
    if not chained:
        oC_ref, on_ref, om_ref, oR_ref, oG_ref = (next(it) for _ in range(5))
    (h_ref, pj_ref, sm_ref, fc_ref, bf_ref, bb_ref, tot_ref, y_ref,
     gm_ref, min_ref, cm_ref, bop_ref) = (next(it) for _ in range(12))
    dec_ref = pj_ref.at[:, W_B:W_B + CH]
    if chained:
        uCN_ref, uR_ref, uG_ref = (next(it) for _ in range(3))

    seqs, seqlen, _ = x_ref.shape

    def x_rows(ref, c):
        if isinstance(c, int):
            r0 = c * CH
            return ref.at[r0 // seqlen, r0 % seqlen:r0 % seqlen + CH, :]
        r0 = pl.multiple_of(c * CH, CH)
        return ref.at[r0 >> _log2(seqlen), pl.ds(pl.multiple_of(r0 & (seqlen - 1), CH), CH), :]

    def norm_rows(c):
        x = x_rows(x_ref, c)[...]
        shift, scale = mod_ref[:, 0:D_MODEL], mod_ref[:, D_MODEL:2 * D_MODEL]
        xn = x * lax.rsqrt(jnp.mean(x * x, axis=-1, keepdims=True) + EPS) * ng_ref[...]
        h_ref[c * CH:(c + 1) * CH, :] = _bf(xn * (1.0 + scale) + shift)

    def front_rows(c):
        rows = slice(c * CH, (c + 1) * CH)
        proj = _dot(h_ref[rows, :], win_ref[:, OFF_A:OFF_S + W_S])
        pj_ref[rows, 0:W_A] = proj[:, 0:W_A]
        sm = proj[:, W_A:]
        sm_ref[rows, :] = sm
        la = _log_sigmoid(_dot(_bf(sm), w2_ref[...]) + b2_ref[...]) * (1.0 / GLA_TAU)
        tri = _bf(_causal_masks()[0].astype(F32))
        ls = _log_sigmoid(sm + gb_ref[...])
        ps = _exact_dot_l(tri, jnp.concatenate([ls, la], axis=1))
        tot = ps[CH - 1:CH, :]
        fc_ref[rows, :] = jnp.where(_bwd_gate_lanes(), tot[:, 0:LANES] - ps[:, 0:LANES] + ls, ps[:, 0:LANES])
        bf_ref[rows, :] = ps[:, LANES:2 * LANES]
        bb_ref[rows, :] = tot[:, 2 * LANES:] - ps[:, 2 * LANES:] + la[:, LANES:]
        tot_ref[c] = tot

    norm_rows(0)
    for c in range(NCH):
        if c + 1 < NCH:
            norm_rows(c + 1)
        front_rows(c)

    AQ, AK, AV, AO = 0, D_A, 2 * D_A, 3 * D_A
    k_scale = DH_A ** -0.5

    ones_cols = jnp.ones((CH, DH_A), F32)

    def mlstm_state_body(c, carry):
        rows = pl.ds(pl.multiple_of(c * CH, CH), CH)
        g_pre = sm_ref[rows, :] + gb_ref[...]
        rw = pltpu.roll(g_pre, 4, 1) - fc_ref[rows, :]
        cm_ref[rows, :] = jnp.where(_bwd_gate_lanes(), _cum_max(rw, True), _cum_max(rw, False))
        bop_ref[rows, :] = _bf(_dot(_pieces(rw), sp_ref[3 * LANES:, :]) + gl_ref[2 * H_A:2 * H_A + 1, :])
        totg = tot_ref[c][:, 0:LANES]
        g = totg + rw
        gm = jnp.max(g, axis=0, keepdims=True)
        gm_ref[c] = gm
        wgt = jnp.exp(g - gm).T
        y_ref[rows, D_A:D_MODEL] = _dot(h_ref[rows, :], win_ref[:, OFF_C:OFF_C + W_C])
        if not chained:
            m_new = jnp.maximum(totg, gm)
            b_row = jnp.exp(gm - m_new)
            om_ref[c] = m_new
            min_ref[c] = jnp.zeros((1, LANES), F32)
        for hh in range(H_A):
            kt = (pj_ref[rows, AK + hh * DH_A:AK + (hh + 1) * DH_A] * k_scale).T
            v_h = pj_ref[rows, AV + hh * DH_A:AV + (hh + 1) * DH_A]
            lhs = jnp.concatenate([kt * wgt[COL_FF + hh:COL_FF + hh + 1, :],
                                   kt * wgt[COL_FB + hh:COL_FB + hh + 1, :]], axis=0)
            u = _dot(_bf(lhs), _bf(jnp.concatenate([v_h, ones_cols], axis=1)))
            for d, col in ((0, COL_FF), (1, COL_FB)):
                u_d = u[d * DH_A:(d + 1) * DH_A, :]
                if chained:
                    uCN_ref[c, d, hh] = u_d
                else:
                    fin = b_row[:, col + hh:col + hh + 1] * u_d
                    oC_ref[c, d, hh] = fin[:, 0:DH_A]
                    on_ref[c, d, hh:hh + 1, :] = fin[:, DH_A:].T[0:1, :]
        return carry

    def chunk_loop(state_body, out_body):
        if chained:
            return lax.fori_loop(0, NCH, out_body, 0, unroll=CHUNK_UNROLL)
        return lax.fori_loop(0, NCH, lambda c, carry: out_body(c, state_body(c, carry)), 0, unroll=CHUNK_UNROLL)

    if chained:
        lax.fori_loop(0, NCH, mlstm_state_body, 0, unroll=CHUNK_UNROLL)

    if chained:
        for d, order, col in ((0, range(NCH), COL_FF), (1, range(NCH - 1, -1, -1), COL_FB)):
            m_run = sm_ref_in[...]
            cn_run = [jnp.concatenate([sC_ref[d, hh],
                                       jnp.broadcast_to(sn_ref[d, hh:hh + 1, :], (DH_A, DH_A)).T], axis=1)
                      for hh in range(H_A)]
            for idx, c in enumerate(order):
                last = idx == NCH - 1
                if not last:
                    totg, gm = tot_ref[c][:, 0:LANES], gm_ref[c]
                    m_new = jnp.maximum(totg + m_run, gm)
                    a_row, b_row = jnp.exp(totg + m_run - m_new), jnp.exp(gm - m_new)
                for hh in range(H_A):
                    u = uCN_ref[c, d, hh]
                    uCN_ref[c, d, hh] = cn_run[hh]
                    if not last:
                        cn_run[hh] = (a_row[:, col + hh:col + hh + 1] * cn_run[hh]
                                      + b_row[:, col + hh:col + hh + 1] * u)
                min_ref[c] = m_run if d == 0 else jnp.where(_bwd_gate_lanes(), m_run, min_ref[c])
                if not last:
                    m_run = m_new

    def mlstm_out_body(c, carry):
        rows = pl.ds(pl.multiple_of(c * CH, CH), CH)
        lower, upper = _causal_masks()
        m_prev = min_ref[c]
        cmx = jnp.maximum(cm_ref[rows, :], m_prev)
        mi = fc_ref[rows, :] + cmx
        a_op = _bf(_dot(_pieces(-cmx), sp_ref[0:3 * LANES, :]) + gl_ref[2 * H_A + 1:2 * H_A + 2, :])
        b_op = bop_ref[rows, :]
        floor_all = jnp.exp(-mi)
        wa_all = jnp.exp(m_prev - cmx)
        for hh in range(H_A):
            q16 = _bf(pj_ref[rows, AQ + hh * DH_A:AQ + (hh + 1) * DH_A])
            k_h = pj_ref[rows, AK + hh * DH_A:AK + (hh + 1) * DH_A] * k_scale
            v_h = pj_ref[rows, AV + hh * DH_A:AV + (hh + 1) * DH_A]
            o_h = pj_ref[rows, AO + hh * DH_A:AO + (hh + 1) * DH_A]
            s = _dot_nt(q16, _bf(k_h))
            qs = []
            for d, msk in ((0, lower), (1, upper)):
                hd = d * H_A + hh
                dm = _dot_nt(a_op * _bf(gl_ref[hd:hd + 1, :]), b_op)
                qs.append(_bf(s * jnp.exp(jnp.where(msk, dm, NEG))))
            nd = _dot(jnp.concatenate(qs, axis=0), _bf(jnp.concatenate([v_h, ones_cols], axis=1)))
            hs = None
            for d, col in ((0, COL_FF + hh), (1, COL_FB + hh)):
                num, den = nd[d * CH:(d + 1) * CH, 0:DH_A], nd[d * CH:(d + 1) * CH, DH_A:]
                if chained:
                    wa = jnp.broadcast_to(wa_all[:, col:col + 1], (CH, DH_A))
                    inter = _dot(q16, _bf(uCN_ref[c, d, hh]))
                    num = num + wa * inter[:, 0:DH_A]
                    den = den + wa * inter[:, DH_A:]
                floor = jnp.broadcast_to(floor_all[:, col:col + 1], (CH, DH_A))
                part = num / jnp.maximum(jnp.abs(den), floor)
                hs = part if hs is None else hs + part
            y_ref[rows, hh * DH_A:(hh + 1) * DH_A] = _sigmoid(o_h) * hs
        return carry

    chunk_loop(mlstm_state_body, mlstm_out_body)

    BQ, BK, BV = 0, D_B, 2 * D_B

    def ret_proj_body(c, carry):
        rows = pl.ds(pl.multiple_of(c * CH, CH), CH)
        pj_ref[rows, 0:W_B] = _dot(h_ref[rows, :], win_ref[:, OFF_B:OFF_B + W_B])
        return carry

    def rotary_body(c, carry):
        rows = pl.ds(pl.multiple_of(c * CH, CH), CH)
        first_half = (lax.broadcasted_iota(jnp.int32, (1, D_B), 1) & (DH_B - 1)) < DH_B // 2
        for off in (BQ, BK):
            t = pj_ref[rows, off:off + D_B]
            partner = jnp.where(first_half, pltpu.roll(t, D_B - DH_B // 2, 1), pltpu.roll(t, DH_B // 2, 1))
            pj_ref[rows, off:off + D_B] = t * cos_ref[rows, :] + partner * sin_ref[rows, :]
        return carry

    lgam = _log_sigmoid(lg_ref[...])
    lg_f, lg_b = lgam[0:1, :], lgam[1:2, :]
    pos = lax.broadcasted_iota(jnp.int32, (CH, 1), 0).astype(F32)
    lower, upper = _causal_masks()
    rel = (lax.broadcasted_iota(jnp.int32, (CH, CH), 0) - lax.broadcasted_iota(jnp.int32, (CH, CH), 1)).astype(F32)
    for hh in range(H_B):
        lf = lg_f[:, hh * DH_B:hh * DH_B + 1]
        lb = lg_b[:, hh * DH_B:hh * DH_B + 1]
        dec_ref[hh * CH:(hh + 1) * CH, :] = (jnp.where(lower, jnp.exp(jnp.maximum(rel, 0.0) * lf), 0.0)
                                             + jnp.where(upper, jnp.exp(jnp.maximum(-rel, 0.0) * lb), 0.0))
    bd_b = _block_diag_mask(D_B, D_B, DH_B, DH_B)
    hm_b = [_lane_block_mask(D_B, DH_B, hh) for hh in range(H_B)]
    ret_scale = DH_B ** -0.5

    def ret_state_body(c, carry):
        rows = pl.ds(pl.multiple_of(c * CH, CH), CH)
        k = pj_ref[rows, BK:BK + D_B] * ret_scale
        v = pj_ref[rows, BV:BV + D_B]
        kf = k * jnp.exp((CH - 1.0 - pos) * lg_f)
        kb = k * jnp.exp(pos * lg_b)
        u = _dot(_bf(jnp.concatenate([kf, kb], axis=1).T), _bf(v))
        for d in range(2):
            u_d = u[d * D_B:(d + 1) * D_B, :]
            if chained:
                uR_ref[c, d] = u_d * bd_b
            else:
                for hh in range(H_B):
                    oR_ref[c, d, hh] = u_d[hh * DH_B:(hh + 1) * DH_B, hh * DH_B:(hh + 1) * DH_B]
        return carry

    def ret_scan():
        for d, order, lg_row in ((0, range(NCH), lg_f), (1, range(NCH - 1, -1, -1), lg_b)):
            g_col = _row_to_col(jnp.exp(float(CH) * lg_row), D_B)
            s_run = _block_diag_value(sR_ref.at[d], H_B, DH_B, DH_B)
            for idx, c in enumerate(order):
                u = uR_ref[c, d]
                uR_ref[c, d] = s_run
                if idx != NCH - 1:
                    s_run = g_col * s_run + u

    def ret_out_body(c, carry):
        rows = pl.ds(pl.multiple_of(c * CH, CH), CH)
        q = pj_ref[rows, BQ:BQ + D_B]
        k = pj_ref[rows, BK:BK + D_B] * ret_scale
        v = pj_ref[rows, BV:BV + D_B]
        kb16 = _bf(k)
        ps = [_bf(_dot_nt(_bf(q * hm_b[hh]), kb16) * dec_ref[hh * CH:(hh + 1) * CH, :]) for hh in range(H_B)]
        vst = jnp.concatenate([_bf(v * hm_b[hh]) for hh in range(H_B)], axis=0)
        yb = _dot(jnp.concatenate(ps, axis=1), vst)
        if chained:
            qf = q * jnp.exp((pos + 1.0) * lg_f)
            qb = q * jnp.exp((float(CH) - pos) * lg_b)
            s_in = jnp.concatenate([uR_ref[c, 0], uR_ref[c, 1]], axis=0)
            yb = yb + _dot(_bf(jnp.concatenate([qf, qb], axis=1)), _bf(s_in))
        y_ref[rows, D_A:D_A + D_B] = yb
        return carry

    KC = H_C * DK_C
    CQ, CK, CV = D_A, D_A + KC, D_A + 2 * KC
    gla_scale = DK_C ** -0.5
    bd_c = _block_diag_mask(KC, D_C, DK_C, DV_C)
    hm_ck = [_lane_block_mask(KC, DK_C, hh) for hh in range(H_C)]
    hm_cv = [_lane_block_mask(D_C, DV_C, hh) for hh in range(H_C)]

    def gla_state_body(c, carry):
        rows = pl.ds(pl.multiple_of(c * CH, CH), CH)
        k = y_ref[rows, CK:CK + KC] * gla_scale
        v = y_ref[rows, CV:CV + D_C]
        tot = tot_ref[c]
        khf = k * jnp.exp(tot[:, LANES:2 * LANES] - bf_ref[rows, :])
        khb = k * jnp.exp(tot[:, 2 * LANES:] - bb_ref[rows, :])
        u = _dot(_bf(jnp.concatenate([khf, khb], axis=1).T), _bf(v))
        for d in range(2):
            u_d = u[d * KC:(d + 1) * KC, :]
            if chained:
                uG_ref[c, d] = u_d * bd_c
            else:
                for hh in range(H_C):
                    oG_ref[c, d, hh] = u_d[hh * DK_C:(hh + 1) * DK_C, hh * DV_C:(hh + 1) * DV_C]
        return carry

    def gla_state_ret_proj(c, carry):
        return gla_state_body(c, ret_proj_body(c, carry))

    if chained:
        lax.fori_loop(0, NCH, gla_state_ret_proj, 0, unroll=CHUNK_UNROLL)

    if chained:
        for d, order in ((0, range(NCH)), (1, range(NCH - 1, -1, -1))):
            s_run = _block_diag_value(sG_ref.at[d], H_C, DK_C, DV_C)
            for idx, c in enumerate(order):
                u = uG_ref[c, d]
                uG_ref[c, d] = s_run
                if idx != NCH - 1:
                    tot_row = tot_ref[c][:, (1 + d) * LANES:(2 + d) * LANES]
                    s_run = _row_to_col(jnp.exp(tot_row), KC) * s_run + u

    def gla_out_body(c, carry):
        rows = pl.ds(pl.multiple_of(c * CH, CH), CH)

        def operands():
            q = y_ref[rows, CQ:CQ + KC]
            k = y_ref[rows, CK:CK + KC] * gla_scale
            v = y_ref[rows, CV:CV + D_C]
            b_f, b_b = bf_ref[rows, :], bb_ref[rows, :]
            return q * jnp.exp(b_f), q * jnp.exp(b_b), k, v, b_f, b_b

        def inter_chunk(qf, qb):
            if not chained:
                return jnp.zeros((CH, D_C), F32)
            s_in = jnp.concatenate([uG_ref[c, 0], uG_ref[c, 1]], axis=0)
            return _dot(_bf(jnp.concatenate([qf, qb], axis=1)), _bf(s_in))

        safe = jnp.min(tot_ref[c][:, LANES:]) > GLA_SAFE_LOG

        @pl.when(safe)
        def _():
            qf, qb, k, v, b_f, b_b = operands()
            lower, upper = _causal_masks()
            kf16, kb16 = _bf(k * jnp.exp(-b_f)), _bf(k * jnp.exp(-b_b))
            ps = []
            for hh in range(H_C):
                s_f = _dot_nt(_bf(qf * hm_ck[hh]), kf16)
                s_b = _dot_nt(_bf(qb * hm_ck[hh]), kb16)
                ps.append(_bf(jnp.where(lower, s_f, 0.0) + jnp.where(upper, s_b, 0.0)))
            vst = jnp.concatenate([_bf(v * hm_cv[hh]) for hh in range(H_C)], axis=0)
            yc = _dot(jnp.concatenate(ps, axis=1), vst)
            y_ref[rows, D_A + D_B:D_MODEL] = yc + inter_chunk(qf, qb) if chained else yc
            if rotary:
                rotary_body(c, 0)

        @pl.when(jnp.logical_not(safe))
        def _():
            qf, qb, k, v, b_f, b_b = operands()
            y_ref[rows, D_A + D_B:D_MODEL] = inter_chunk(qf, qb)
            if rotary:
                rotary_body(c, 0)
            key = lax.broadcasted_iota(jnp.int32, (CH, 1), 0)
            sub = 8

            def query_body(g, carry2):
                grp = pl.ds(pl.multiple_of(c * CH + g * sub, sub), sub)
                bq_f, bq_b, qg = bf_ref[grp, :], bb_ref[grp, :], y_ref[grp, CQ:CQ + KC]
                outs = []
                for r in range(sub):
                    i = g * sub + r
                    w_f = jnp.exp(jnp.where(key <= i, bq_f[r:r + 1, :] - b_f, NEG))
                    w_b = jnp.exp(jnp.where(key >= i, bq_b[r:r + 1, :] - b_b, NEG))
                    t = _dot(_bf(qg[r:r + 1, :] * k * (w_f + w_b)), _bf(bd_c))
                    outs.append(jnp.sum(t * v, axis=0, keepdims=True))
                y_ref[grp, D_A + D_B:D_MODEL] += jnp.concatenate(outs, axis=0)
                return carry2

            lax.fori_loop(0, CH // sub, query_body, 0)

        return carry

    chunk_loop(gla_state_ret_proj, gla_out_body)

    if chained:
        lax.fori_loop(0, NCH, ret_state_body, 0, unroll=CHUNK_UNROLL)
        ret_scan()
    chunk_loop(ret_state_body, ret_out_body)

    def gate_proj(c):
        rows = slice(c * CH, (c + 1) * CH)
        pj_ref[rows, 0:W_Z] = _dot(h_ref[rows, :], win_ref[:, OFF_Z:OFF_Z + W_Z])

    def head_norm_gate(c):
        rows = slice(c * CH, (c + 1) * CH)
        parts = []
        for hh in range(H_A):
            ya = y_ref[rows, hh * DH_A:(hh + 1) * DH_A]
            parts.append(ya * lax.rsqrt(jnp.mean(ya * ya, axis=1, keepdims=True) + EPS))
        ybc = y_ref[rows, D_A:D_MODEL]
        seg = _bf(_block_diag_mask(D_B + D_C, D_B + D_C, DH_B, DH_B) * (1.0 / DH_B))
        parts.append(ybc * lax.rsqrt(_exact_dot_r(ybc * ybc, seg) + EPS))
        z = pj_ref[rows, 0:W_Z]
        yg = jnp.concatenate(parts, axis=1) * hng_ref[...] * (z * _sigmoid(z))
        h_ref[rows, :] = _bf(yg)

    def out_proj_residual(c):
        rows = slice(c * CH, (c + 1) * CH)
        xo = x_rows(x_ref, c)[...] + mod_ref[:, 2 * D_MODEL:] * _dot(h_ref[rows, :], wout_ref[...])
        if final:
            xo = xo * lax.rsqrt(jnp.mean(xo * xo, axis=-1, keepdims=True) + EPS) * fg_ref[...]
        x_rows(out_ref, c)[...] = xo

    gate_proj(0)
    for c in range(NCH):
        if c + 1 < NCH:
            gate_proj(c + 1)
        head_norm_gate(c)
        if c > 0:
            out_proj_residual(c - 1)
    out_proj_residual(NCH - 1)


def _layer_call(l, x, mods, mod_row, norm_g, w_in_r, gate_b, lg_rows, w2_full, b2_full, hn_g, w_out_b, final_g,
                states, rot=None, chained=False, final=False):
    batch, seqlen, d = x.shape
    seqs = TOK // seqlen
    steps = batch // seqs
    assert seqs * seqlen == TOK and steps * seqs == batch and seqlen % CH == 0
    assert chained == (seqs == 1)
    assert chained or seqlen == CH

    def const(shape):
        nd = len(shape)
        return pl.BlockSpec(shape, lambda i, _nd=nd: (0,) * _nd)

    def layer_block(shape):
        nd = len(shape)
        return pl.BlockSpec((None,) + shape, lambda i, _nd=nd: (l,) + (0,) * _nd)

    in_specs = [
        pl.BlockSpec((seqs, seqlen, d), lambda i: (i, 0, 0)),
        pl.BlockSpec((None, None, 1, 3 * d),
                     (lambda i: (l, i, 0, 0)) if mod_row is None else (lambda i: (l, mod_row, 0, 0))),
        layer_block((1, d)),
        pl.BlockSpec((None, d, D_INR), lambda i: (l, 0, 0), pipeline_mode=pl.Buffered(1)),
        layer_block((1, LANES)),
        layer_block((2, D_B)),
        layer_block((LANES, 2 * LANES)),
        layer_block((1, 2 * LANES)),
        layer_block((1, d)),
        pl.BlockSpec((None, d, d), lambda i: (l, 0, 0), pipeline_mode=pl.Buffered(1)),
    ]
    args = [x, mods, norm_g.reshape(-1, 1, d), w_in_r, gate_b, lg_rows, w2_full, b2_full,
            hn_g.reshape(-1, 1, d), w_out_b]
    for cst in _gate_constants():
        in_specs.append(pl.BlockSpec(cst.shape, lambda i, _nd=cst.ndim: (0,) * _nd, pipeline_mode=pl.Buffered(1)))
        args.append(cst)
    if final:
        in_specs.append(const((1, d)))
        args.append(final_g.reshape(1, d))
    if rot is not None:
        in_specs += [pl.BlockSpec((seqlen, D_B), lambda i: (0, 0), pipeline_mode=pl.Buffered(1))] * 2
        args += list(rot)
    state_blocks = [(2, H_A, DH_A, DH_A), (2, H_A, DH_A), (1, LANES), (2, H_B, DH_B, DH_B), (2, H_C, DK_C, DV_C)]
    n_in = len(args)
    out_shape = [jax.ShapeDtypeStruct(x.shape, x.dtype)]
    out_specs = [pl.BlockSpec((seqs, seqlen, d), lambda i: (i, 0, 0))]
    aliases = {}
    for k, (blk, arr) in enumerate(zip(state_blocks, states)):
        zeros = (0,) * len(blk)
        if chained:
            in_specs.append(pl.BlockSpec((None, None) + blk, lambda i, _z=zeros: (i, l) + _z))
            args.append(arr)
            continue
        out_shape.append(jax.ShapeDtypeStruct(arr.shape, arr.dtype))
        out_specs.append(pl.BlockSpec((seqs, None) + blk, lambda i, _z=zeros: (i, l) + _z))
        if not isinstance(arr, jax.ShapeDtypeStruct):
            in_specs.append(pl.BlockSpec(memory_space=pl.ANY))
            args.append(arr)
            aliases[n_in + k] = 1 + k

    scratch = [
        pltpu.VMEM((TOK, d), BF16),
        pltpu.VMEM((TOK, W_A), F32),
        pltpu.VMEM((TOK, LANES), F32),
        pltpu.VMEM((TOK, LANES), F32),
        pltpu.VMEM((TOK, LANES), F32),
        pltpu.VMEM((TOK, LANES), F32),
        pltpu.VMEM((NCH, 1, 3 * LANES), F32),
        pltpu.VMEM((TOK, d), F32),
        pltpu.VMEM((NCH, 1, LANES), F32),
        pltpu.VMEM((NCH, 1, LANES), F32),
        pltpu.VMEM((TOK, LANES), F32),
        pltpu.VMEM((TOK, LANES), BF16),
    ]
    if chained:
        scratch += [
            pltpu.VMEM((NCH, 2, H_A, DH_A, 2 * DH_A), F32),
            pltpu.VMEM((NCH, 2, D_B, D_B), F32),
            pltpu.VMEM((NCH, 2, H_C * DK_C, D_C), F32),
        ]
    outs = pl.pallas_call(
        functools.partial(_layer_kernel, chained=chained, rotary=rot is not None, final=final,
                          state_inputs=len(args) > n_in),
        out_shape=out_shape,
        grid=(steps,),
        in_specs=in_specs,
        out_specs=out_specs,
        scratch_shapes=scratch,
        input_output_aliases=aliases,
        compiler_params=pltpu.CompilerParams(dimension_semantics=("arbitrary",), vmem_limit_bytes=VMEM_LIMIT),
        name=("latent" if chained else "context") + f"_layer{l}",
    )(*args)
    return outs


def _rotary_tables(seqlen):
    rows = seqlen // GRID_W
    r = jnp.repeat(jnp.arange(rows, dtype=F32), GRID_W)
    col = jnp.tile(jnp.arange(GRID_W, dtype=F32), rows)
    n_f = DH_B // 4
    freqs = ROPE_BASE ** (-jnp.arange(n_f, dtype=F32) / n_f)
    ang = jnp.concatenate([r[:, None] * freqs, col[:, None] * freqs], axis=-1)
    cos, sin = jnp.cos(ang), jnp.sin(ang)
    cos_l = jnp.tile(jnp.concatenate([cos, cos], axis=-1), (1, H_B))
    sin_l = jnp.tile(jnp.concatenate([-sin, sin], axis=-1), (1, H_B))
    return cos_l, sin_l


def _gate_lanes(m):
    z = jnp.zeros(m.shape[:-2] + (LANES,), m.dtype)
    z = z.at[..., COL_FF:COL_FF + H_A].set(m[..., 0, :]).at[..., COL_FB:COL_FB + H_A].set(m[..., 1, :])
    return z[..., None, :]


def _pack_segments():
    src = dict(zip("aq ak av ao az ag bq bk bv bz cq ck cv cz clr".split(),
                   zip(np.cumsum((0,) + IN_WIDTHS[:-1]).tolist(), IN_WIDTHS)))
    segs = []
    for names, dst in (("aq ak av ao", OFF_A), ("ag clr", OFF_S), ("bq bk bv", OFF_B), ("cq ck cv", OFF_C),
                       ("az bz cz", OFF_Z)):
        for name in names.split():
            s, w = src[name]
            segs.append((s, dst, w))
            dst += w
    return segs


def _pack_kernel(wt_ref, o_ref):
    tr = o_ref.shape[0]
    small = []
    for s, dst, w in _pack_segments():
        if w % LANES == 0:
            o_ref[:, dst:dst + w] = _bf(wt_ref[s:s + w, :].T)
        else:
            small.append(wt_ref[s:s + w, :])
    small.append(jnp.zeros((W_S - sum(t.shape[0] for t in small), tr), F32))
    o_ref[:, OFF_S:OFF_S + W_S] = _bf(jnp.concatenate(small, axis=0).T)


def _pack_w_in(w_in):
    depth, d, d_in = w_in.shape
    tr = 256
    return pl.pallas_call(
        _pack_kernel,
        out_shape=jax.ShapeDtypeStruct((depth, d, D_INR), BF16),
        grid=(depth, d // tr),
        in_specs=[pl.BlockSpec((None, d_in, tr), lambda l, r: (l, 0, r))],
        out_specs=pl.BlockSpec((None, tr, D_INR), lambda l, r: (l, r, 0)),
        name="pack_w_in",
    )(jnp.swapaxes(w_in, 1, 2))


def kernel(x_prompt, x_sample, state_mlstm_C, state_mlstm_n, state_mlstm_m, state_ret, state_gla, c, c_ctx,
           norm_g, w_ada, b_ada, w_in, mlstm_gate_b, ret_decay_logit, gla_w2, gla_b2, headnorm_g, w_out, final_g):
    depth = w_in.shape[0]
    dec_batch = c.shape[0]

    w_in_r = _pack_w_in(w_in)
    w_out_b = w_out.astype(BF16)
    n_g = mlstm_gate_b.shape[-1]
    gate_b = jnp.pad(mlstm_gate_b, ((0, 0), (0, LANES - n_g)))[:, None, :]
    lg_rows = jnp.repeat(ret_decay_logit, DH_B, axis=-1)
    w2_full = jnp.zeros((depth, LANES, 2 * LANES), F32)
    w2_full = w2_full.at[:, n_g:n_g + GLA_RANK, 0:LANES].set(gla_w2[:, 0])
    w2_full = w2_full.at[:, n_g + GLA_RANK:n_g + 2 * GLA_RANK, LANES:].set(gla_w2[:, 1]).astype(BF16)
    b2_full = gla_b2.reshape(depth, 1, 2 * LANES)

    cstack = jnp.zeros((16, D_MODEL), F32).at[0:dec_batch].set(c).at[dec_batch].set(c_ctx)
    mods = _modulation(cstack, w_ada, b_ada)[:, :, None, :]

    rot = _rotary_tables(x_sample.shape[1])
    cache = (state_mlstm_C, state_mlstm_n, _gate_lanes(state_mlstm_m), state_ret, state_gla)
    bp = x_prompt.shape[0]
    new = tuple(jax.ShapeDtypeStruct((bp,) + s.shape[1:], F32) for s in cache)

    xp, xs = x_prompt, x_sample
    for l in range(depth):
        final = l == depth - 1
        common = (norm_g, w_in_r, gate_b, lg_rows, w2_full, b2_full, headnorm_g, w_out_b, final_g)
        outs = _layer_call(l, xp, mods, dec_batch, *common, states=new, final=final)
        xp, new = outs[0], tuple(outs[1:])
        xs = _layer_call(l, xs, mods, None, *common, states=cache, rot=rot, chained=True, final=final)[0]

    new_c, new_n, m_l, new_r, new_g = new
    new_m = jnp.stack([m_l[:, :, 0, COL_FF:COL_FF + H_A], m_l[:, :, 0, COL_FB:COL_FB + H_A]], axis=2)
    return (xp, xs, new_c, new_n, new_m, new_r, new_g)
```

```python
import functools

import jax
import jax.numpy as jnp
import numpy as np
from jax import lax
from jax.experimental import pallas as pl
from jax.experimental.pallas import tpu as pltpu

F32 = jnp.float32
BF16 = jnp.bfloat16

D_MODEL = 1024
H_A, DH_A = 4, 128
H_B, DH_B = 4, 64
H_C, DK_C, DV_C = 4, 32, 64
D_A, D_B, D_C = H_A * DH_A, H_B * DH_B, H_C * DV_C
GLA_RANK = 16
GLA_TAU = 16.0
GRID_W = 64
ROPE_BASE = 10000.0
EPS = 1e-6
IN_WIDTHS = (D_A, D_A, D_A, D_A, D_A, 4 * H_A, D_B, D_B, D_B, D_B, H_C * DK_C, H_C * DK_C, D_C, D_C, 2 * GLA_RANK)

LANES = 128
CH = 256
TOK = 1024
NCH = TOK // CH
CHUNK_UNROLL = 4
VMEM_LIMIT = 62 * 1024 * 1024
NEG = -1e30
GLA_SAFE_LOG = -80.0

OFF_A, W_A = 0, 4 * D_A
OFF_S, W_S = OFF_A + W_A, LANES
OFF_B, W_B = OFF_S + W_S, 3 * D_B
OFF_C, W_C = OFF_B + W_B, 2 * H_C * DK_C + D_C
OFF_Z, W_Z = OFF_C + W_C, D_MODEL
D_INR = OFF_Z + W_Z
COL_FF, COL_FB = 4, 12


def _dot(a, b):
    return jnp.dot(a, b, preferred_element_type=F32)


def _dot_nt(a, b):
    return lax.dot_general(a, b, (((1,), (1,)), ((), ())), preferred_element_type=F32)


def _bf(x):
    return x.astype(BF16)


def _split3(x):
    hi = _bf(x)
    r1 = x - hi.astype(F32)
    mid = _bf(r1)
    lo = _bf(r1 - mid.astype(F32))
    return hi, mid, lo


def _exact_dot_l(m_bf, x):
    hi, mid, lo = _split3(x)
    return (_dot(m_bf, lo) + _dot(m_bf, mid)) + _dot(m_bf, hi)


def _exact_dot_r(x, m_bf):
    hi, mid, lo = _split3(x)
    return (_dot(lo, m_bf) + _dot(mid, m_bf)) + _dot(hi, m_bf)


def _log_sigmoid(x):
    return jnp.minimum(x, 0.0) - jnp.log(1.0 + jnp.exp(-jnp.abs(x)))


def _sigmoid(x):
    return 1.0 / (1.0 + jnp.exp(-x))


def _row_to_col(row, n):
    eye = lax.broadcasted_iota(jnp.int32, (n, n), 0) == lax.broadcasted_iota(jnp.int32, (n, n), 1)
    return jnp.sum(jnp.where(eye, row, 0.0), axis=1, keepdims=True)


def _lane_block_mask(width, block, h):
    lane = lax.broadcasted_iota(jnp.int32, (1, width), 1)
    return ((lane >= h * block) & (lane < (h + 1) * block)).astype(F32)


def _log2(n):
    assert n & (n - 1) == 0
    return n.bit_length() - 1


def _block_diag_mask(rows, cols, rblock, cblock):
    r = lax.broadcasted_iota(jnp.int32, (rows, cols), 0) >> _log2(rblock)
    c = lax.broadcasted_iota(jnp.int32, (rows, cols), 1) >> _log2(cblock)
    return (r == c).astype(F32)


def _causal_masks():
    ri = lax.broadcasted_iota(jnp.int32, (CH, CH), 0)
    cj = lax.broadcasted_iota(jnp.int32, (CH, CH), 1)
    return cj <= ri, cj >= ri


def _bwd_gate_lanes():
    lane = lax.broadcasted_iota(jnp.int32, (1, LANES), 1)
    return (lane >= COL_FB) & (lane < COL_FB + H_A)


def _block_diag_value(ref, hn, a, b):
    rows = []
    for h in range(hn):
        wide = jnp.concatenate([ref[h], jnp.zeros((a, (hn - 1) * b), F32)], axis=1)
        rows.append(wide if h == 0 else pltpu.roll(wide, h * b, 1))
    return jnp.concatenate(rows, axis=0)


def _pieces(x):
    return jnp.concatenate(_split3(x), axis=1)


def _cum_max(x, reverse):
    n = x.shape[0]
    row = lax.broadcasted_iota(jnp.int32, x.shape, 0)
    s = 1
    while s < n:
        if reverse:
            shifted = jnp.where(row < n - s, pltpu.roll(x, n - s, 0), NEG)
        else:
            shifted = jnp.where(row >= s, pltpu.roll(x, s, 0), NEG)
        x = jnp.maximum(x, shifted)
        s *= 2
    return x


GATE_BLOCK = 16
N_PIECES = 3


def _gate_constants():
    half = N_PIECES * GATE_BLOCK
    cols = [COL_FF + h for h in range(H_A)] + [COL_FB + h for h in range(H_A)]
    sp = np.zeros((2 * N_PIECES * LANES, LANES), np.float32)
    for part in range(2):
        for p in range(N_PIECES):
            for k in cols:
                sp[(part * N_PIECES + p) * LANES + k, part * half + p * GATE_BLOCK + k] = 1.0
    lane = np.arange(LANES)
    gl = np.zeros((16, LANES), np.float32)
    for hd, k in enumerate(cols):
        gl[hd] = (lane % GATE_BLOCK == k) & (lane < 2 * half)
    valid = np.isin(lane % GATE_BLOCK, cols)
    gl[2 * H_A] = valid & (lane < half)
    gl[2 * H_A + 1] = valid & (lane >= half) & (lane < 2 * half)
    return jnp.asarray(sp, BF16), jnp.asarray(gl, F32)


def _modulation_kernel(c_ref, w_ref, b_ref, o_ref):
    cv = c_ref[...]
    s = cv * _sigmoid(cv)
    o_ref[...] = _dot(_bf(s), _bf(w_ref[...])) + b_ref[...]


def _modulation(cstack, w_ada, b_ada):
    depth, d, d3 = w_ada.shape
    rows = cstack.shape[0]
    tn = 512
    return pl.pallas_call(
        _modulation_kernel,
        out_shape=jax.ShapeDtypeStruct((depth, rows, d3), F32),
        grid=(depth, d3 // tn),
        in_specs=[
            pl.BlockSpec((rows, d), lambda l, j: (0, 0)),
            pl.BlockSpec((None, d, tn), lambda l, j: (l, 0, j)),
            pl.BlockSpec((None, 1, tn), lambda l, j: (l, 0, j)),
        ],
        out_specs=pl.BlockSpec((None, rows, tn), lambda l, j: (l, 0, j)),
        name="adaln_modulation",
    )(cstack, w_ada, b_ada.reshape(depth, 1, d3))


def _layer_kernel(*refs, chained, rotary, final, state_inputs):
    it = iter(refs)
    x_ref, mod_ref, ng_ref, win_ref, gb_ref, lg_ref, w2_ref, b2_ref, hng_ref, wout_ref = (next(it) for _ in range(10))
    sp_ref, gl_ref = next(it), next(it)
    fg_ref = next(it) if final else None
    if rotary:
        cos_ref, sin_ref = next(it), next(it)
    if state_inputs:
        sC_ref, sn_ref, sm_ref_in, sR_ref, sG_ref = (next(it) for _ in range(5))
    out_ref = next(it)
    if not chained:
        oC_ref, on_ref, om_ref, oR_ref, oG_ref = (next(it) for _ in range(5))
    (h_ref, pj_ref, sm_ref, fc_ref, bf_ref, bb_ref, tot_ref, y_ref,
     gm_ref, min_ref, cm_ref, bop_ref) = (next(it) for _ in range(12))
    dec_ref = pj_ref.at[:, W_B:W_B + CH]
    if chained:
        uCN_ref, uR_ref, uG_ref = (next(it) for _ in range(3))

    seqs, seqlen, _ = x_ref.shape

    def x_rows(ref, c):
        if isinstance(c, int):
            r0 = c * CH
            return ref.at[r0 // seqlen, r0 % seqlen:r0 % seqlen + CH, :]
        r0 = pl.multiple_of(c * CH, CH)
        return ref.at[r0 >> _log2(seqlen), pl.ds(pl.multiple_of(r0 & (seqlen - 1), CH), CH), :]

    def norm_rows(c):
        x = x_rows(x_ref, c)[...]
        shift, scale = mod_ref[:, 0:D_MODEL], mod_ref[:, D_MODEL:2 * D_MODEL]
        xn = x * lax.rsqrt(jnp.mean(x * x, axis=-1, keepdims=True) + EPS) * ng_ref[...]
        h_ref[c * CH:(c + 1) * CH, :] = _bf(xn * (1.0 + scale) + shift)

    def front_rows(c):
        rows = slice(c * CH, (c + 1) * CH)
        proj = _dot(h_ref[rows, :], win_ref[:, OFF_A:OFF_S + W_S])
        pj_ref[rows, 0:W_A] = proj[:, 0:W_A]
        sm = proj[:, W_A:]
        sm_ref[rows, :] = sm
        la = _log_sigmoid(_dot(_bf(sm), w2_ref[...]) + b2_ref[...]) * (1.0 / GLA_TAU)
        tri = _bf(_causal_masks()[0].astype(F32))
        ls = _log_sigmoid(sm + gb_ref[...])
        ps = _exact_dot_l(tri, jnp.concatenate([ls, la], axis=1))
        tot = ps[CH - 1:CH, :]
        fc_ref[rows, :] = jnp.where(_bwd_gate_lanes(), tot[:, 0:LANES] - ps[:, 0:LANES] + ls, ps[:, 0:LANES])
        bf_ref[rows, :] = ps[:, LANES:2 * LANES]
        bb_ref[rows, :] = tot[:, 2 * LANES:] - ps[:, 2 * LANES:] + la[:, LANES:]
        tot_ref[c] = tot

    norm_rows(0)
    for c in range(NCH):
        if c + 1 < NCH:
            norm_rows(c + 1)
        front_rows(c)

    AQ, AK, AV, AO = 0, D_A, 2 * D_A, 3 * D_A
    k_scale = DH_A ** -0.5

    ones_cols = jnp.ones((CH, DH_A), F32)

    def mlstm_state_body(c, carry):
        rows = pl.ds(pl.multiple_of(c * CH, CH), CH)
        g_pre = sm_ref[rows, :] + gb_ref[...]
        rw = pltpu.roll(g_pre, 4, 1) - fc_ref[rows, :]
        cm_ref[rows, :] = jnp.where(_bwd_gate_lanes(), _cum_max(rw, True), _cum_max(rw, False))
        bop_ref[rows, :] = _bf(_dot(_pieces(rw), sp_ref[3 * LANES:, :]) + gl_ref[2 * H_A:2 * H_A + 1, :])
        totg = tot_ref[c][:, 0:LANES]
        g = totg + rw
        gm = jnp.max(g, axis=0, keepdims=True)
        gm_ref[c] = gm
        wgt = jnp.exp(g - gm).T
        y_ref[rows, D_A:D_MODEL] = _dot(h_ref[rows, :], win_ref[:, OFF_C:OFF_C + W_C])
        if not chained:
            m_new = jnp.maximum(totg, gm)
            b_row = jnp.exp(gm - m_new)
            om_ref[c] = m_new
            min_ref[c] = jnp.zeros((1, LANES), F32)
        for hh in range(H_A):
            kt = (pj_ref[rows, AK + hh * DH_A:AK + (hh + 1) * DH_A] * k_scale).T
            v_h = pj_ref[rows, AV + hh * DH_A:AV + (hh + 1) * DH_A]
            lhs = jnp.concatenate([kt * wgt[COL_FF + hh:COL_FF + hh + 1, :],
                                   kt * wgt[COL_FB + hh:COL_FB + hh + 1, :]], axis=0)
            u = _dot(_bf(lhs), _bf(jnp.concatenate([v_h, ones_cols], axis=1)))
            for d, col in ((0, COL_FF), (1, COL_FB)):
                u_d = u[d * DH_A:(d + 1) * DH_A, :]
                if chained:
                    uCN_ref[c, d, hh] = u_d
                else:
                    fin = b_row[:, col + hh:col + hh + 1] * u_d
                    oC_ref[c, d, hh] = fin[:, 0:DH_A]
                    on_ref[c, d, hh:hh + 1, :] = fin[:, DH_A:].T[0:1, :]
        return carry

    def chunk_loop(state_body, out_body, unroll=CHUNK_UNROLL):
        if chained:
            return lax.fori_loop(0, NCH, out_body, 0, unroll=unroll)
        return lax.fori_loop(0, NCH, lambda c, carry: out_body(c, state_body(c, carry)), 0, unroll=unroll)

    if chained:
        lax.fori_loop(0, NCH, mlstm_state_body, 0, unroll=CHUNK_UNROLL)

    if chained:
        for d, order, col in ((0, range(NCH), COL_FF), (1, range(NCH - 1, -1, -1), COL_FB)):
            m_run = sm_ref_in[...]
            cn_run = [jnp.concatenate([sC_ref[d, hh],
                                       jnp.broadcast_to(sn_ref[d, hh:hh + 1, :], (DH_A, DH_A)).T], axis=1)
                      for hh in range(H_A)]
            for idx, c in enumerate(order):
                last = idx == NCH - 1
                if not last:
                    totg, gm = tot_ref[c][:, 0:LANES], gm_ref[c]
                    m_new = jnp.maximum(totg + m_run, gm)
                    a_row, b_row = jnp.exp(totg + m_run - m_new), jnp.exp(gm - m_new)
                for hh in range(H_A):
                    u = uCN_ref[c, d, hh]
                    uCN_ref[c, d, hh] = cn_run[hh]
                    if not last:
                        cn_run[hh] = (a_row[:, col + hh:col + hh + 1] * cn_run[hh]
                                      + b_row[:, col + hh:col + hh + 1] * u)
                min_ref[c] = m_run if d == 0 else jnp.where(_bwd_gate_lanes(), m_run, min_ref[c])
                if not last:
                    m_run = m_new

    def mlstm_out_body(c, carry):
        rows = pl.ds(pl.multiple_of(c * CH, CH), CH)
        lower, upper = _causal_masks()
        m_prev = min_ref[c]
        cmx = jnp.maximum(cm_ref[rows, :], m_prev)
        mi = fc_ref[rows, :] + cmx
        a_op = _bf(_dot(_pieces(-cmx), sp_ref[0:3 * LANES, :]) + gl_ref[2 * H_A + 1:2 * H_A + 2, :])
        b_op = bop_ref[rows, :]
        floor_all = jnp.exp(-mi)
        wa_all = jnp.exp(m_prev - cmx)
        for hh in range(H_A):
            q16 = _bf(pj_ref[rows, AQ + hh * DH_A:AQ + (hh + 1) * DH_A])
            k_h = pj_ref[rows, AK + hh * DH_A:AK + (hh + 1) * DH_A] * k_scale
            v_h = pj_ref[rows, AV + hh * DH_A:AV + (hh + 1) * DH_A]
            o_h = pj_ref[rows, AO + hh * DH_A:AO + (hh + 1) * DH_A]
            s = _dot_nt(q16, _bf(k_h))
            qs = []
            for d, msk in ((0, lower), (1, upper)):
                hd = d * H_A + hh
                dm = _dot_nt(a_op * _bf(gl_ref[hd:hd + 1, :]), b_op)
                qs.append(_bf(s * jnp.exp(jnp.where(msk, dm, NEG))))
            nd = _dot(jnp.concatenate(qs, axis=0), _bf(jnp.concatenate([v_h, ones_cols], axis=1)))
            hs = None
            for d, col in ((0, COL_FF + hh), (1, COL_FB + hh)):
                num, den = nd[d * CH:(d + 1) * CH, 0:DH_A], nd[d * CH:(d + 1) * CH, DH_A:]
                if chained:
                    wa = jnp.broadcast_to(wa_all[:, col:col + 1], (CH, DH_A))
                    inter = _dot(q16, _bf(uCN_ref[c, d, hh]))
                    num = num + wa * inter[:, 0:DH_A]
                    den = den + wa * inter[:, DH_A:]
                floor = jnp.broadcast_to(floor_all[:, col:col + 1], (CH, DH_A))
                part = num / jnp.maximum(jnp.abs(den), floor)
                hs = part if hs is None else hs + part
            y_ref[rows, hh * DH_A:(hh + 1) * DH_A] = _sigmoid(o_h) * hs
        return carry

    chunk_loop(mlstm_state_body, mlstm_out_body)

    BQ, BK, BV = 0, D_B, 2 * D_B

    def ret_proj_body(c, carry):
        rows = pl.ds(pl.multiple_of(c * CH, CH), CH)
        pj_ref[rows, 0:W_B] = _dot(h_ref[rows, :], win_ref[:, OFF_B:OFF_B + W_B])
        return carry

    def rotary_body(c, carry):
        rows = pl.ds(pl.multiple_of(c * CH, CH), CH)
        first_half = (lax.broadcasted_iota(jnp.int32, (1, D_B), 1) & (DH_B - 1)) < DH_B // 2
        for off in (BQ, BK):
            t = pj_ref[rows, off:off + D_B]
            partner = jnp.where(first_half, pltpu.roll(t, D_B - DH_B // 2, 1), pltpu.roll(t, DH_B // 2, 1))
            pj_ref[rows, off:off + D_B] = t * cos_ref[rows, :] + partner * sin_ref[rows, :]
        return carry

    lgam = _log_sigmoid(lg_ref[...])
    lg_f, lg_b = lgam[0:1, :], lgam[1:2, :]
    pos = lax.broadcasted_iota(jnp.int32, (CH, 1), 0).astype(F32)
    lower, upper = _causal_masks()
    rel = (lax.broadcasted_iota(jnp.int32, (CH, CH), 0) - lax.broadcasted_iota(jnp.int32, (CH, CH), 1)).astype(F32)
    for hh in range(H_B):
        lf = lg_f[:, hh * DH_B:hh * DH_B + 1]
        lb = lg_b[:, hh * DH_B:hh * DH_B + 1]
        dec_ref[hh * CH:(hh + 1) * CH, :] = (jnp.where(lower, jnp.exp(jnp.maximum(rel, 0.0) * lf), 0.0)
                                             + jnp.where(upper, jnp.exp(jnp.maximum(-rel, 0.0) * lb), 0.0))
    bd_b = _block_diag_mask(D_B, D_B, DH_B, DH_B)
    hm_b = [_lane_block_mask(D_B, DH_B, hh) for hh in range(H_B)]
    ret_scale = DH_B ** -0.5

    def ret_state_body(c, carry):
        rows = pl.ds(pl.multiple_of(c * CH, CH), CH)
        k = pj_ref[rows, BK:BK + D_B] * ret_scale
        v = pj_ref[rows, BV:BV + D_B]
        kf = k * jnp.exp((CH - 1.0 - pos) * lg_f)
        kb = k * jnp.exp(pos * lg_b)
        u = _dot(_bf(jnp.concatenate([kf, kb], axis=1).T), _bf(v))
        for d in range(2):
            u_d = u[d * D_B:(d + 1) * D_B, :]
            if chained:
                uR_ref[c, d] = u_d * bd_b
            else:
                for hh in range(H_B):
                    oR_ref[c, d, hh] = u_d[hh * DH_B:(hh + 1) * DH_B, hh * DH_B:(hh + 1) * DH_B]
        return carry

    def ret_scan():
        for d, order, lg_row in ((0, range(NCH), lg_f), (1, range(NCH - 1, -1, -1), lg_b)):
            g_col = _row_to_col(jnp.exp(float(CH) * lg_row), D_B)
            s_run = _block_diag_value(sR_ref.at[d], H_B, DH_B, DH_B)
            for idx, c in enumerate(order):
                u = uR_ref[c, d]
                uR_ref[c, d] = s_run
                if idx != NCH - 1:
                    s_run = g_col * s_run + u

    def ret_out_body(c, carry):
        rows = pl.ds(pl.multiple_of(c * CH, CH), CH)
        q = pj_ref[rows, BQ:BQ + D_B]
        k = pj_ref[rows, BK:BK + D_B] * ret_scale
        v = pj_ref[rows, BV:BV + D_B]
        kb16 = _bf(k)
        ps = [_bf(_dot_nt(_bf(q * hm_b[hh]), kb16) * dec_ref[hh * CH:(hh + 1) * CH, :]) for hh in range(H_B)]
        vst = jnp.concatenate([_bf(v * hm_b[hh]) for hh in range(H_B)], axis=0)
        yb = _dot(jnp.concatenate(ps, axis=1), vst)
        if chained:
            qf = q * jnp.exp((pos + 1.0) * lg_f)
            qb = q * jnp.exp((float(CH) - pos) * lg_b)
            s_in = jnp.concatenate([uR_ref[c, 0], uR_ref[c, 1]], axis=0)
            yb = yb + _dot(_bf(jnp.concatenate([qf, qb], axis=1)), _bf(s_in))
        y_ref[rows, D_A:D_A + D_B] = yb
        return carry

    KC = H_C * DK_C
    CQ, CK, CV = D_A, D_A + KC, D_A + 2 * KC
    gla_scale = DK_C ** -0.5
    bd_c = _block_diag_mask(KC, D_C, DK_C, DV_C)
    hm_ck = [_lane_block_mask(KC, DK_C, hh) for hh in range(H_C)]
    hm_cv = [_lane_block_mask(D_C, DV_C, hh) for hh in range(H_C)]

    def gla_state_body(c, carry):
        rows = pl.ds(pl.multiple_of(c * CH, CH), CH)
        k = y_ref[rows, CK:CK + KC] * gla_scale
        v = y_ref[rows, CV:CV + D_C]
        tot = tot_ref[c]
        khf = k * jnp.exp(tot[:, LANES:2 * LANES] - bf_ref[rows, :])
        khb = k * jnp.exp(tot[:, 2 * LANES:] - bb_ref[rows, :])
        u = _dot(_bf(jnp.concatenate([khf, khb], axis=1).T), _bf(v))
        for d in range(2):
            u_d = u[d * KC:(d + 1) * KC, :]
            if chained:
                uG_ref[c, d] = u_d * bd_c
            else:
                for hh in range(H_C):
                    oG_ref[c, d, hh] = u_d[hh * DK_C:(hh + 1) * DK_C, hh * DV_C:(hh + 1) * DV_C]
        return carry

    def gla_state_ret_proj(c, carry):
        return gla_state_body(c, ret_proj_body(c, carry))

    if chained:
        lax.fori_loop(0, NCH, gla_state_ret_proj, 0, unroll=CHUNK_UNROLL)

    if chained:
        for d, order in ((0, range(NCH)), (1, range(NCH - 1, -1, -1))):
            s_run = _block_diag_value(sG_ref.at[d], H_C, DK_C, DV_C)
            for idx, c in enumerate(order):
                u = uG_ref[c, d]
                uG_ref[c, d] = s_run
                if idx != NCH - 1:
                    tot_row = tot_ref[c][:, (1 + d) * LANES:(2 + d) * LANES]
                    s_run = _row_to_col(jnp.exp(tot_row), KC) * s_run + u

    def gla_out_body(c, carry, factorised):
        rows = pl.ds(pl.multiple_of(c * CH, CH), CH)

        def operands():
            q = y_ref[rows, CQ:CQ + KC]
            k = y_ref[rows, CK:CK + KC] * gla_scale
            v = y_ref[rows, CV:CV + D_C]
            b_f, b_b = bf_ref[rows, :], bb_ref[rows, :]
            return q * jnp.exp(b_f), q * jnp.exp(b_b), k, v, b_f, b_b

        def inter_chunk(qf, qb):
            if not chained:
                return jnp.zeros((CH, D_C), F32)
            s_in = jnp.concatenate([uG_ref[c, 0], uG_ref[c, 1]], axis=0)
            return _dot(_bf(jnp.concatenate([qf, qb], axis=1)), _bf(s_in))

        qf, qb, k, v, b_f, b_b = operands()
        if rotary:
            rotary_body(c, 0)
        if factorised:
            lower, upper = _causal_masks()
            kf16, kb16 = _bf(k * jnp.exp(-b_f)), _bf(k * jnp.exp(-b_b))
            ps = []
            for hh in range(H_C):
                s_f = _dot_nt(_bf(qf * hm_ck[hh]), kf16)
                s_b = _dot_nt(_bf(qb * hm_ck[hh]), kb16)
                ps.append(_bf(jnp.where(lower, s_f, 0.0) + jnp.where(upper, s_b, 0.0)))
            vst = jnp.concatenate([_bf(v * hm_cv[hh]) for hh in range(H_C)], axis=0)
            yc = _dot(jnp.concatenate(ps, axis=1), vst)
            y_ref[rows, D_A + D_B:D_MODEL] = yc + inter_chunk(qf, qb) if chained else yc
            return carry

        y_ref[rows, D_A + D_B:D_MODEL] = inter_chunk(qf, qb)
        key = lax.broadcasted_iota(jnp.int32, (CH, 1), 0)
        sub = 8

        def query_body(g, carry2):
            grp = pl.ds(pl.multiple_of(c * CH + g * sub, sub), sub)
            bq_f, bq_b, qg = bf_ref[grp, :], bb_ref[grp, :], y_ref[grp, CQ:CQ + KC]
            outs = []
            for r in range(sub):
                i = g * sub + r
                w_f = jnp.exp(jnp.where(key <= i, bq_f[r:r + 1, :] - b_f, NEG))
                w_b = jnp.exp(jnp.where(key >= i, bq_b[r:r + 1, :] - b_b, NEG))
                t = _dot(_bf(qg[r:r + 1, :] * k * (w_f + w_b)), _bf(bd_c))
                outs.append(jnp.sum(t * v, axis=0, keepdims=True))
            y_ref[grp, D_A + D_B:D_MODEL] += jnp.concatenate(outs, axis=0)
            return carry2

        lax.fori_loop(0, CH // sub, query_body, 0)
        return carry

    all_safe = jnp.min(tot_ref[...][:, :, LANES:]) > GLA_SAFE_LOG

    @pl.when(all_safe)
    def _():
        chunk_loop(gla_state_ret_proj, functools.partial(gla_out_body, factorised=True))

    @pl.when(jnp.logical_not(all_safe))
    def _():
        chunk_loop(gla_state_ret_proj, functools.partial(gla_out_body, factorised=False), unroll=1)

    if chained:
        lax.fori_loop(0, NCH, ret_state_body, 0, unroll=CHUNK_UNROLL)
        ret_scan()
    chunk_loop(ret_state_body, ret_out_body)

    def gate_proj(c):
        rows = slice(c * CH, (c + 1) * CH)
        pj_ref[rows, 0:W_Z] = _dot(h_ref[rows, :], win_ref[:, OFF_Z:OFF_Z + W_Z])

    def head_norm_gate(c):
        rows = slice(c * CH, (c + 1) * CH)
        parts = []
        for hh in range(H_A):
            ya = y_ref[rows, hh * DH_A:(hh + 1) * DH_A]
            parts.append(ya * lax.rsqrt(jnp.mean(ya * ya, axis=1, keepdims=True) + EPS))
        ybc = y_ref[rows, D_A:D_MODEL]
        seg = _bf(_block_diag_mask(D_B + D_C, D_B + D_C, DH_B, DH_B) * (1.0 / DH_B))
        parts.append(ybc * lax.rsqrt(_exact_dot_r(ybc * ybc, seg) + EPS))
        z = pj_ref[rows, 0:W_Z]
        yg = jnp.concatenate(parts, axis=1) * hng_ref[...] * (z * _sigmoid(z))
        h_ref[rows, :] = _bf(yg)

    def out_proj_residual(c):
        rows = slice(c * CH, (c + 1) * CH)
        xo = x_rows(x_ref, c)[...] + mod_ref[:, 2 * D_MODEL:] * _dot(h_ref[rows, :], wout_ref[...])
        if final:
            xo = xo * lax.rsqrt(jnp.mean(xo * xo, axis=-1, keepdims=True) + EPS) * fg_ref[...]
        x_rows(out_ref, c)[...] = xo

    gate_proj(0)
    for c in range(NCH):
        if c + 1 < NCH:
            gate_proj(c + 1)
        head_norm_gate(c)
        if c > 0:
            out_proj_residual(c - 1)
    out_proj_residual(NCH - 1)


def _layer_call(l, x, mods, mod_row, norm_g, w_in_r, gate_b, lg_rows, w2_full, b2_full, hn_g, w_out_b, final_g,
                states, rot=None, chained=False, final=False):
    batch, seqlen, d = x.shape
    seqs = TOK // seqlen
    steps = batch // seqs
    assert seqs * seqlen == TOK and steps * seqs == batch and seqlen % CH == 0
    assert chained == (seqs == 1)
    assert chained or seqlen == CH

    def const(shape):
        nd = len(shape)
        return pl.BlockSpec(shape, lambda i, _nd=nd: (0,) * _nd)

    def layer_block(shape):
        nd = len(shape)
        return pl.BlockSpec((None,) + shape, lambda i, _nd=nd: (l,) + (0,) * _nd)

    in_specs = [
        pl.BlockSpec((seqs, seqlen, d), lambda i: (i, 0, 0)),
        pl.BlockSpec((None, None, 1, 3 * d),
                     (lambda i: (l, i, 0, 0)) if mod_row is None else (lambda i: (l, mod_row, 0, 0))),
        layer_block((1, d)),
        pl.BlockSpec((None, d, D_INR), lambda i: (l, 0, 0), pipeline_mode=pl.Buffered(1)),
        layer_block((1, LANES)),
        layer_block((2, D_B)),
        layer_block((LANES, 2 * LANES)),
        layer_block((1, 2 * LANES)),
        layer_block((1, d)),
        pl.BlockSpec((None, d, d), lambda i: (l, 0, 0), pipeline_mode=pl.Buffered(1)),
    ]
    args = [x, mods, norm_g.reshape(-1, 1, d), w_in_r, gate_b, lg_rows, w2_full, b2_full,
            hn_g.reshape(-1, 1, d), w_out_b]
    for cst in _gate_constants():
        in_specs.append(pl.BlockSpec(cst.shape, lambda i, _nd=cst.ndim: (0,) * _nd, pipeline_mode=pl.Buffered(1)))
        args.append(cst)
    if final:
        in_specs.append(const((1, d)))
        args.append(final_g.reshape(1, d))
    if rot is not None:
        in_specs += [pl.BlockSpec((seqlen, D_B), lambda i: (0, 0), pipeline_mode=pl.Buffered(1))] * 2
        args += list(rot)
    state_blocks = [(2, H_A, DH_A, DH_A), (2, H_A, DH_A), (1, LANES), (2, H_B, DH_B, DH_B), (2, H_C, DK_C, DV_C)]
    n_in = len(args)
    out_shape = [jax.ShapeDtypeStruct(x.shape, x.dtype)]
    out_specs = [pl.BlockSpec((seqs, seqlen, d), lambda i: (i, 0, 0))]
    aliases = {}
    for k, (blk, arr) in enumerate(zip(state_blocks, states)):
        zeros = (0,) * len(blk)
        if chained:
            in_specs.append(pl.BlockSpec((None, None) + blk, lambda i, _z=zeros: (i, l) + _z))
            args.append(arr)
            continue
        out_shape.append(jax.ShapeDtypeStruct(arr.shape, arr.dtype))
        out_specs.append(pl.BlockSpec((seqs, None) + blk, lambda i, _z=zeros: (i, l) + _z))
        if not isinstance(arr, jax.ShapeDtypeStruct):
            in_specs.append(pl.BlockSpec(memory_space=pl.ANY))
            args.append(arr)
            aliases[n_in + k] = 1 + k

    scratch = [
        pltpu.VMEM((TOK, d), BF16),
        pltpu.VMEM((TOK, W_A), F32),
        pltpu.VMEM((TOK, LANES), F32),
        pltpu.VMEM((TOK, LANES), F32),
        pltpu.VMEM((TOK, LANES), F32),
        pltpu.VMEM((TOK, LANES), F32),
        pltpu.VMEM((NCH, 1, 3 * LANES), F32),
        pltpu.VMEM((TOK, d), F32),
        pltpu.VMEM((NCH, 1, LANES), F32),
        pltpu.VMEM((NCH, 1, LANES), F32),
        pltpu.VMEM((TOK, LANES), F32),
        pltpu.VMEM((TOK, LANES), BF16),
    ]
    if chained:
        scratch += [
            pltpu.VMEM((NCH, 2, H_A, DH_A, 2 * DH_A), F32),
            pltpu.VMEM((NCH, 2, D_B, D_B), F32),
            pltpu.VMEM((NCH, 2, H_C * DK_C, D_C), F32),
        ]
    outs = pl.pallas_call(
        functools.partial(_layer_kernel, chained=chained, rotary=rot is not None, final=final,
                          state_inputs=len(args) > n_in),
        out_shape=out_shape,
        grid=(steps,),
        in_specs=in_specs,
        out_specs=out_specs,
        scratch_shapes=scratch,
        input_output_aliases=aliases,
        compiler_params=pltpu.CompilerParams(dimension_semantics=("arbitrary",), vmem_limit_bytes=VMEM_LIMIT),
        name=("latent" if chained else "context") + f"_layer{l}",
    )(*args)
    return outs


def _rotary_tables(seqlen):
    rows = seqlen // GRID_W
    r = jnp.repeat(jnp.arange(rows, dtype=F32), GRID_W)
    col = jnp.tile(jnp.arange(GRID_W, dtype=F32), rows)
    n_f = DH_B // 4
    freqs = ROPE_BASE ** (-jnp.arange(n_f, dtype=F32) / n_f)
    ang = jnp.concatenate([r[:, None] * freqs, col[:, None] * freqs], axis=-1)
    cos, sin = jnp.cos(ang), jnp.sin(ang)
    cos_l = jnp.tile(jnp.concatenate([cos, cos], axis=-1), (1, H_B))
    sin_l = jnp.tile(jnp.concatenate([-sin, sin], axis=-1), (1, H_B))
    return cos_l, sin_l


def _gate_lanes(m):
    z = jnp.zeros(m.shape[:-2] + (LANES,), m.dtype)
    z = z.at[..., COL_FF:COL_FF + H_A].set(m[..., 0, :]).at[..., COL_FB:COL_FB + H_A].set(m[..., 1, :])
    return z[..., None, :]


def _pack_segments():
    src = dict(zip("aq ak av ao az ag bq bk bv bz cq ck cv cz clr".split(),
                   zip(np.cumsum((0,) + IN_WIDTHS[:-1]).tolist(), IN_WIDTHS)))
    segs = []
    for names, dst in (("aq ak av ao", OFF_A), ("ag clr", OFF_S), ("bq bk bv", OFF_B), ("cq ck cv", OFF_C),
                       ("az bz cz", OFF_Z)):
        for name in names.split():
            s, w = src[name]
            segs.append((s, dst, w))
            dst += w
    return segs


def _pack_kernel(wt_ref, o_ref):
    tr = o_ref.shape[0]
    small = []
    for s, dst, w in _pack_segments():
        if w % LANES == 0:
            o_ref[:, dst:dst + w] = _bf(wt_ref[s:s + w, :].T)
        else:
            small.append(wt_ref[s:s + w, :])
    small.append(jnp.zeros((W_S - sum(t.shape[0] for t in small), tr), F32))
    o_ref[:, OFF_S:OFF_S + W_S] = _bf(jnp.concatenate(small, axis=0).T)


def _pack_w_in(w_in):
    depth, d, d_in = w_in.shape
    tr = 256
    return pl.pallas_call(
        _pack_kernel,
        out_shape=jax.ShapeDtypeStruct((depth, d, D_INR), BF16),
        grid=(depth, d // tr),
        in_specs=[pl.BlockSpec((None, d_in, tr), lambda l, r: (l, 0, r))],
        out_specs=pl.BlockSpec((None, tr, D_INR), lambda l, r: (l, r, 0)),
        name="pack_w_in",
    )(jnp.swapaxes(w_in, 1, 2))


def kernel(x_prompt, x_sample, state_mlstm_C, state_mlstm_n, state_mlstm_m, state_ret, state_gla, c, c_ctx,
           norm_g, w_ada, b_ada, w_in, mlstm_gate_b, ret_decay_logit, gla_w2, gla_b2, headnorm_g, w_out, final_g):
    depth = w_in.shape[0]
    dec_batch = c.shape[0]

    w_in_r = _pack_w_in(w_in)
    w_out_b = w_out.astype(BF16)
    n_g = mlstm_gate_b.shape[-1]
    gate_b = jnp.pad(mlstm_gate_b, ((0, 0), (0, LANES - n_g)))[:, None, :]
    lg_rows = jnp.repeat(ret_decay_logit, DH_B, axis=-1)
    w2_full = jnp.zeros((depth, LANES, 2 * LANES), F32)
    w2_full = w2_full.at[:, n_g:n_g + GLA_RANK, 0:LANES].set(gla_w2[:, 0])
    w2_full = w2_full.at[:, n_g + GLA_RANK:n_g + 2 * GLA_RANK, LANES:].set(gla_w2[:, 1]).astype(BF16)
    b2_full = gla_b2.reshape(depth, 1, 2 * LANES)

    cstack = jnp.zeros((16, D_MODEL), F32).at[0:dec_batch].set(c).at[dec_batch].set(c_ctx)
    mods = _modulation(cstack, w_ada, b_ada)[:, :, None, :]

    rot = _rotary_tables(x_sample.shape[1])
    cache = (state_mlstm_C, state_mlstm_n, _gate_lanes(state_mlstm_m), state_ret, state_gla)
    bp = x_prompt.shape[0]
    new = tuple(jax.ShapeDtypeStruct((bp,) + s.shape[1:], F32) for s in cache)

    xp, xs = x_prompt, x_sample
    for l in range(depth):
        final = l == depth - 1
        common = (norm_g, w_in_r, gate_b, lg_rows, w2_full, b2_full, headnorm_g, w_out_b, final_g)
        outs = _layer_call(l, xp, mods, dec_batch, *common, states=new, final=final)
        xp, new = outs[0], tuple(outs[1:])
        xs = _layer_call(l, xs, mods, None, *common, states=cache, rot=rot, chained=True, final=final)[0]

    new_c, new_n, m_l, new_r, new_g = new
    new_m = jnp.stack([m_l[:, :, 0, COL_FF:COL_FF + H_A], m_l[:, :, 0, COL_FB:COL_FB + H_A]], axis=2)
    return (xp, xs, new_c, new_n, new_m, new_r, new_g)
```

```python
import functools

import jax
import jax.numpy as jnp
import numpy as np
from jax import lax
from jax.experimental import pallas as pl
from jax.experimental.pallas import tpu as pltpu

F32 = jnp.float32
BF16 = jnp.bfloat16

D_MODEL = 1024
H_A, DH_A = 4, 128
H_B, DH_B = 4, 64
H_C, DK_C, DV_C = 4, 32, 64
D_A, D_B, D_C = H_A * DH_A, H_B * DH_B, H_C * DV_C
GLA_RANK = 16
GLA_TAU = 16.0
GRID_W = 64
ROPE_BASE = 10000.0
EPS = 1e-6
IN_WIDTHS = (D_A, D_A, D_A, D_A, D_A, 4 * H_A, D_B, D_B, D_B, D_B, H_C * DK_C, H_C * DK_C, D_C, D_C, 2 * GLA_RANK)

LANES = 128
CH = 256
TOK = 1024
NCH = TOK // CH
CHUNK_UNROLL = 4
VMEM_LIMIT = 62 * 1024 * 1024
NEG = -1e30
GLA_SAFE_LOG = -80.0

OFF_A, W_A = 0, 4 * D_A
OFF_S, W_S = OFF_A + W_A, LANES
OFF_B, W_B = OFF_S + W_S, 3 * D_B
OFF_C, W_C = OFF_B + W_B, 2 * H_C * DK_C + D_C
OFF_Z, W_Z = OFF_C + W_C, D_MODEL
D_INR = OFF_Z + W_Z
COL_FF, COL_FB = 4, 12


def _dot(a, b):
    return jnp.dot(a, b, preferred_element_type=F32)


def _dot_nt(a, b):
    return lax.dot_general(a, b, (((1,), (1,)), ((), ())), preferred_element_type=F32)


def _bf(x):
    return x.astype(BF16)


def _split3(x):
    hi = _bf(x)
    r1 = x - hi.astype(F32)
    mid = _bf(r1)
    lo = _bf(r1 - mid.astype(F32))
    return hi, mid, lo


def _exact_dot_l(m_bf, x):
    hi, mid, lo = _split3(x)
    return (_dot(m_bf, lo) + _dot(m_bf, mid)) + _dot(m_bf, hi)


def _exact_dot_r(x, m_bf):
    hi, mid, lo = _split3(x)
    return (_dot(lo, m_bf) + _dot(mid, m_bf)) + _dot(hi, m_bf)


def _log_sigmoid(x):
    return jnp.minimum(x, 0.0) - jnp.log(1.0 + jnp.exp(-jnp.abs(x)))


def _sigmoid(x):
    return 1.0 / (1.0 + jnp.exp(-x))


def _row_to_col(row, n):
    eye = lax.broadcasted_iota(jnp.int32, (n, n), 0) == lax.broadcasted_iota(jnp.int32, (n, n), 1)
    return jnp.sum(jnp.where(eye, row, 0.0), axis=1, keepdims=True)


def _lane_block_mask(width, block, h):
    lane = lax.broadcasted_iota(jnp.int32, (1, width), 1)
    return ((lane >= h * block) & (lane < (h + 1) * block)).astype(F32)


def _log2(n):
    assert n & (n - 1) == 0
    return n.bit_length() - 1


def _block_diag_mask(rows, cols, rblock, cblock):
    r = lax.broadcasted_iota(jnp.int32, (rows, cols), 0) >> _log2(rblock)
    c = lax.broadcasted_iota(jnp.int32, (rows, cols), 1) >> _log2(cblock)
    return (r == c).astype(F32)


def _causal_masks():
    ri = lax.broadcasted_iota(jnp.int32, (CH, CH), 0)
    cj = lax.broadcasted_iota(jnp.int32, (CH, CH), 1)
    return cj <= ri, cj >= ri


def _bwd_gate_lanes():
    lane = lax.broadcasted_iota(jnp.int32, (1, LANES), 1)
    return (lane >= COL_FB) & (lane < COL_FB + H_A)


def _block_diag_value(ref, hn, a, b):
    rows = []
    for h in range(hn):
        wide = jnp.concatenate([ref[h], jnp.zeros((a, (hn - 1) * b), F32)], axis=1)
        rows.append(wide if h == 0 else pltpu.roll(wide, h * b, 1))
    return jnp.concatenate(rows, axis=0)


def _pieces(x):
    return jnp.concatenate(_split3(x), axis=1)


def _cum_max(x, reverse):
    n = x.shape[0]
    row = lax.broadcasted_iota(jnp.int32, x.shape, 0)
    s = 1
    while s < n:
        if reverse:
            shifted = jnp.where(row < n - s, pltpu.roll(x, n - s, 0), NEG)
        else:
            shifted = jnp.where(row >= s, pltpu.roll(x, s, 0), NEG)
        x = jnp.maximum(x, shifted)
        s *= 2
    return x


GATE_BLOCK = 16
N_PIECES = 3


def _gate_constants():
    half = N_PIECES * GATE_BLOCK
    cols = [COL_FF + h for h in range(H_A)] + [COL_FB + h for h in range(H_A)]
    sp = np.zeros((2 * N_PIECES * LANES, LANES), np.float32)
    for part in range(2):
        for p in range(N_PIECES):
            for k in cols:
                sp[(part * N_PIECES + p) * LANES + k, part * half + p * GATE_BLOCK + k] = 1.0
    lane = np.arange(LANES)
    gl = np.zeros((16, LANES), np.float32)
    for hd, k in enumerate(cols):
        gl[hd] = (lane % GATE_BLOCK == k) & (lane < 2 * half)
    valid = np.isin(lane % GATE_BLOCK, cols)
    gl[2 * H_A] = valid & (lane < half)
    gl[2 * H_A + 1] = valid & (lane >= half) & (lane < 2 * half)
    return jnp.asarray(sp, BF16), jnp.asarray(gl, F32)


def _modulation_kernel(c_ref, w_ref, b_ref, o_ref):
    cv = c_ref[...]
    s = cv * _sigmoid(cv)
    o_ref[...] = _dot(_bf(s), _bf(w_ref[...])) + b_ref[...]


def _modulation(cstack, w_ada, b_ada):
    depth, d, d3 = w_ada.shape
    rows = cstack.shape[0]
    tn = 1024
    return pl.pallas_call(
        _modulation_kernel,
        out_shape=jax.ShapeDtypeStruct((depth, rows, d3), F32),
        grid=(depth, d3 // tn),
        in_specs=[
            pl.BlockSpec((rows, d), lambda l, j: (0, 0)),
            pl.BlockSpec((None, d, tn), lambda l, j: (l, 0, j)),
            pl.BlockSpec((None, 1, tn), lambda l, j: (l, 0, j)),
        ],
        out_specs=pl.BlockSpec((None, rows, tn), lambda l, j: (l, 0, j)),
        name="adaln_modulation",
    )(cstack, w_ada, b_ada.reshape(depth, 1, d3))


def _layer_kernel(*refs, chained, rotary, final):
    it = iter(refs)
    x_ref, mod_ref, ng_ref, win_ref, gb_ref, lg_ref, w2_ref, b2_ref, hng_ref, wout_ref = (next(it) for _ in range(10))
    sp_ref, gl_ref = next(it), next(it)
    fg_ref = next(it) if final else None
    if rotary:
        cos_ref, sin_ref = next(it), next(it)
    if chained:
        sC_ref, sn_ref, sm_ref_in, sR_ref, sG_ref = (next(it) for _ in range(5))
    out_ref = next(it)
    if not chained:
        oC_ref, on_ref, om_ref, oR_ref, oG_ref = (next(it) for _ in range(5))
    (h_ref, pj_ref, sm_ref, fc_ref, bf_ref, bb_ref, tot_ref, y_ref,
     gm_ref, min_ref, cm_ref, bop_ref) = (next(it) for _ in range(12))
    dec_ref = pj_ref.at[:, W_B:W_B + CH]
    if chained:
        uCN_ref, uR_ref, uG_ref = (next(it) for _ in range(3))

    seqs, seqlen, _ = x_ref.shape

    def x_rows(ref, c):
        if isinstance(c, int):
            r0 = c * CH
            return ref.at[r0 // seqlen, r0 % seqlen:r0 % seqlen + CH, :]
        r0 = pl.multiple_of(c * CH, CH)
        return ref.at[r0 >> _log2(seqlen), pl.ds(pl.multiple_of(r0 & (seqlen - 1), CH), CH), :]

    def norm_rows(c):
        x = x_rows(x_ref, c)[...]
        shift, scale = mod_ref[:, 0:D_MODEL], mod_ref[:, D_MODEL:2 * D_MODEL]
        xn = x * lax.rsqrt(jnp.mean(x * x, axis=-1, keepdims=True) + EPS) * ng_ref[...]
        h_ref[c * CH:(c + 1) * CH, :] = _bf(xn * (1.0 + scale) + shift)

    def front_rows(c):
        rows = slice(c * CH, (c + 1) * CH)
        proj = _dot(h_ref[rows, :], win_ref[:, OFF_A:OFF_S + W_S])
        pj_ref[rows, 0:W_A] = proj[:, 0:W_A]
        sm = proj[:, W_A:]
        sm_ref[rows, :] = sm
        la = _log_sigmoid(_dot(_bf(sm), w2_ref[...]) + b2_ref[...]) * (1.0 / GLA_TAU)
        tri = _bf(_causal_masks()[0].astype(F32))
        ls = _log_sigmoid(sm + gb_ref[...])
        ps = _exact_dot_l(tri, jnp.concatenate([ls, la], axis=1))
        tot = ps[CH - 1:CH, :]
        fc_ref[rows, :] = jnp.where(_bwd_gate_lanes(), tot[:, 0:LANES] - ps[:, 0:LANES] + ls, ps[:, 0:LANES])
        bf_ref[rows, :] = ps[:, LANES:2 * LANES]
        bb_ref[rows, :] = tot[:, 2 * LANES:] - ps[:, 2 * LANES:] + la[:, LANES:]
        tot_ref[c] = tot

    norm_rows(0)
    for c in range(NCH):
        if c + 1 < NCH:
            norm_rows(c + 1)
        front_rows(c)

    AQ, AK, AV, AO = 0, D_A, 2 * D_A, 3 * D_A
    k_scale = DH_A ** -0.5

    ones_cols = jnp.ones((CH, DH_A), F32)

    def mlstm_state_body(c, carry):
        rows = pl.ds(pl.multiple_of(c * CH, CH), CH)
        g_pre = sm_ref[rows, :] + gb_ref[...]
        rw = pltpu.roll(g_pre, 4, 1) - fc_ref[rows, :]
        cm_ref[rows, :] = jnp.where(_bwd_gate_lanes(), _cum_max(rw, True), _cum_max(rw, False))
        bop_ref[rows, :] = _bf(_dot(_pieces(rw), sp_ref[3 * LANES:, :]) + gl_ref[2 * H_A:2 * H_A + 1, :])
        totg = tot_ref[c][:, 0:LANES]
        g = totg + rw
        gm = jnp.max(g, axis=0, keepdims=True)
        gm_ref[c] = gm
        wgt = jnp.exp(g - gm).T
        y_ref[rows, D_A:D_MODEL] = _dot(h_ref[rows, :], win_ref[:, OFF_C:OFF_C + W_C])
        if not chained:
            m_new = jnp.maximum(totg, gm)
            b_row = jnp.exp(gm - m_new)
            om_ref[c] = m_new
            min_ref[c] = jnp.zeros((1, LANES), F32)
        for hh in range(H_A):
            kt = (pj_ref[rows, AK + hh * DH_A:AK + (hh + 1) * DH_A] * k_scale).T
            v_h = pj_ref[rows, AV + hh * DH_A:AV + (hh + 1) * DH_A]
            lhs = jnp.concatenate([kt * wgt[COL_FF + hh:COL_FF + hh + 1, :],
                                   kt * wgt[COL_FB + hh:COL_FB + hh + 1, :]], axis=0)
            u = _dot(_bf(lhs), _bf(jnp.concatenate([v_h, ones_cols], axis=1)))
            for d, col in ((0, COL_FF), (1, COL_FB)):
                u_d = u[d * DH_A:(d + 1) * DH_A, :]
                if chained:
                    uCN_ref[c, d, hh] = u_d
                else:
                    fin = b_row[:, col + hh:col + hh + 1] * u_d
                    oC_ref[c, d, hh] = fin[:, 0:DH_A]
                    on_ref[c, d, hh:hh + 1, :] = fin[:, DH_A:].T[0:1, :]
        return carry

    def chunk_loop(state_body, out_body, unroll=CHUNK_UNROLL):
        if chained:
            return lax.fori_loop(0, NCH, out_body, 0, unroll=unroll)
        return lax.fori_loop(0, NCH, lambda c, carry: out_body(c, state_body(c, carry)), 0, unroll=unroll)

    if chained:
        lax.fori_loop(0, NCH, mlstm_state_body, 0, unroll=CHUNK_UNROLL)

    if chained:
        for d, order, col in ((0, range(NCH), COL_FF), (1, range(NCH - 1, -1, -1), COL_FB)):
            m_run = sm_ref_in[...]
            cn_run = [jnp.concatenate([sC_ref[d, hh],
                                       jnp.broadcast_to(sn_ref[d, hh:hh + 1, :], (DH_A, DH_A)).T], axis=1)
                      for hh in range(H_A)]
            for idx, c in enumerate(order):
                last = idx == NCH - 1
                if not last:
                    totg, gm = tot_ref[c][:, 0:LANES], gm_ref[c]
                    m_new = jnp.maximum(totg + m_run, gm)
                    a_row, b_row = jnp.exp(totg + m_run - m_new), jnp.exp(gm - m_new)
                for hh in range(H_A):
                    u = uCN_ref[c, d, hh]
                    uCN_ref[c, d, hh] = cn_run[hh]
                    if not last:
                        cn_run[hh] = (a_row[:, col + hh:col + hh + 1] * cn_run[hh]
                                      + b_row[:, col + hh:col + hh + 1] * u)
                min_ref[c] = m_run if d == 0 else jnp.where(_bwd_gate_lanes(), m_run, min_ref[c])
                if not last:
                    m_run = m_new

    def mlstm_out_body(c, carry):
        rows = pl.ds(pl.multiple_of(c * CH, CH), CH)
        lower, upper = _causal_masks()
        m_prev = min_ref[c]
        cmx = jnp.maximum(cm_ref[rows, :], m_prev)
        mi = fc_ref[rows, :] + cmx
        a_op = _bf(_dot(_pieces(-cmx), sp_ref[0:3 * LANES, :]) + gl_ref[2 * H_A + 1:2 * H_A + 2, :])
        b_op = bop_ref[rows, :]
        floor_all = jnp.exp(-mi)
        wa_all = jnp.exp(m_prev - cmx)
        for hh in range(H_A):
            q16 = _bf(pj_ref[rows, AQ + hh * DH_A:AQ + (hh + 1) * DH_A])
            k_h = pj_ref[rows, AK + hh * DH_A:AK + (hh + 1) * DH_A] * k_scale
            v_h = pj_ref[rows, AV + hh * DH_A:AV + (hh + 1) * DH_A]
            o_h = pj_ref[rows, AO + hh * DH_A:AO + (hh + 1) * DH_A]
            s = _dot_nt(q16, _bf(k_h))
            qs = []
            for d, msk in ((0, lower), (1, upper)):
                hd = d * H_A + hh
                dm = _dot_nt(a_op * _bf(gl_ref[hd:hd + 1, :]), b_op)
                qs.append(_bf(s * jnp.exp(jnp.where(msk, dm, NEG))))
            nd = _dot(jnp.concatenate(qs, axis=0), _bf(jnp.concatenate([v_h, ones_cols], axis=1)))
            hs = None
            for d, col in ((0, COL_FF + hh), (1, COL_FB + hh)):
                num, den = nd[d * CH:(d + 1) * CH, 0:DH_A], nd[d * CH:(d + 1) * CH, DH_A:]
                if chained:
                    wa = jnp.broadcast_to(wa_all[:, col:col + 1], (CH, DH_A))
                    inter = _dot(q16, _bf(uCN_ref[c, d, hh]))
                    num = num + wa * inter[:, 0:DH_A]
                    den = den + wa * inter[:, DH_A:]
                floor = jnp.broadcast_to(floor_all[:, col:col + 1], (CH, DH_A))
                part = num / jnp.maximum(jnp.abs(den), floor)
                hs = part if hs is None else hs + part
            y_ref[rows, hh * DH_A:(hh + 1) * DH_A] = _sigmoid(o_h) * hs
        return carry

    chunk_loop(mlstm_state_body, mlstm_out_body)

    BQ, BK, BV = 0, D_B, 2 * D_B

    def ret_proj_body(c, carry):
        rows = pl.ds(pl.multiple_of(c * CH, CH), CH)
        pj_ref[rows, 0:W_B] = _dot(h_ref[rows, :], win_ref[:, OFF_B:OFF_B + W_B])
        return carry

    def rotary_body(c, carry):
        rows = pl.ds(pl.multiple_of(c * CH, CH), CH)
        first_half = (lax.broadcasted_iota(jnp.int32, (1, D_B), 1) & (DH_B - 1)) < DH_B // 2
        for off in (BQ, BK):
            t = pj_ref[rows, off:off + D_B]
            partner = jnp.where(first_half, pltpu.roll(t, D_B - DH_B // 2, 1), pltpu.roll(t, DH_B // 2, 1))
            pj_ref[rows, off:off + D_B] = t * cos_ref[rows, :] + partner * sin_ref[rows, :]
        return carry

    lgam = _log_sigmoid(lg_ref[...])
    lg_f, lg_b = lgam[0:1, :], lgam[1:2, :]
    pos = lax.broadcasted_iota(jnp.int32, (CH, 1), 0).astype(F32)
    lower, upper = _causal_masks()
    rel = (lax.broadcasted_iota(jnp.int32, (CH, CH), 0) - lax.broadcasted_iota(jnp.int32, (CH, CH), 1)).astype(F32)
    for hh in range(H_B):
        lf = lg_f[:, hh * DH_B:hh * DH_B + 1]
        lb = lg_b[:, hh * DH_B:hh * DH_B + 1]
        dec_ref[hh * CH:(hh + 1) * CH, :] = (jnp.where(lower, jnp.exp(jnp.maximum(rel, 0.0) * lf), 0.0)
                                             + jnp.where(upper, jnp.exp(jnp.maximum(-rel, 0.0) * lb), 0.0))
    bd_b = _block_diag_mask(D_B, D_B, DH_B, DH_B)
    hm_b = [_lane_block_mask(D_B, DH_B, hh) for hh in range(H_B)]
    ret_scale = DH_B ** -0.5

    def ret_state_body(c, carry):
        rows = pl.ds(pl.multiple_of(c * CH, CH), CH)
        k = pj_ref[rows, BK:BK + D_B] * ret_scale
        v = pj_ref[rows, BV:BV + D_B]
        kf = k * jnp.exp((CH - 1.0 - pos) * lg_f)
        kb = k * jnp.exp(pos * lg_b)
        u = _dot(_bf(jnp.concatenate([kf, kb], axis=1).T), _bf(v))
        for d in range(2):
            u_d = u[d * D_B:(d + 1) * D_B, :]
            if chained:
                uR_ref[c, d] = u_d * bd_b
            else:
                for hh in range(H_B):
                    oR_ref[c, d, hh] = u_d[hh * DH_B:(hh + 1) * DH_B, hh * DH_B:(hh + 1) * DH_B]
        return carry

    def ret_scan():
        for d, order, lg_row in ((0, range(NCH), lg_f), (1, range(NCH - 1, -1, -1), lg_b)):
            g_col = _row_to_col(jnp.exp(float(CH) * lg_row), D_B)
            s_run = _block_diag_value(sR_ref.at[d], H_B, DH_B, DH_B)
            for idx, c in enumerate(order):
                u = uR_ref[c, d]
                uR_ref[c, d] = s_run
                if idx != NCH - 1:
                    s_run = g_col * s_run + u

    def ret_out_body(c, carry):
        rows = pl.ds(pl.multiple_of(c * CH, CH), CH)
        q = pj_ref[rows, BQ:BQ + D_B]
        k = pj_ref[rows, BK:BK + D_B] * ret_scale
        v = pj_ref[rows, BV:BV + D_B]
        kb16 = _bf(k)
        ps = [_bf(_dot_nt(_bf(q * hm_b[hh]), kb16) * dec_ref[hh * CH:(hh + 1) * CH, :]) for hh in range(H_B)]
        vst = jnp.concatenate([_bf(v * hm_b[hh]) for hh in range(H_B)], axis=0)
        yb = _dot(jnp.concatenate(ps, axis=1), vst)
        if chained:
            qf = q * jnp.exp((pos + 1.0) * lg_f)
            qb = q * jnp.exp((float(CH) - pos) * lg_b)
            s_in = jnp.concatenate([uR_ref[c, 0], uR_ref[c, 1]], axis=0)
            yb = yb + _dot(_bf(jnp.concatenate([qf, qb], axis=1)), _bf(s_in))
        y_ref[rows, D_A:D_A + D_B] = yb
        return carry

    KC = H_C * DK_C
    CQ, CK, CV = D_A, D_A + KC, D_A + 2 * KC
    gla_scale = DK_C ** -0.5
    bd_c = _block_diag_mask(KC, D_C, DK_C, DV_C)
    hm_ck = [_lane_block_mask(KC, DK_C, hh) for hh in range(H_C)]
    hm_cv = [_lane_block_mask(D_C, DV_C, hh) for hh in range(H_C)]

    def gla_state_body(c, carry):
        rows = pl.ds(pl.multiple_of(c * CH, CH), CH)
        k = y_ref[rows, CK:CK + KC] * gla_scale
        v = y_ref[rows, CV:CV + D_C]
        tot = tot_ref[c]
        khf = k * jnp.exp(tot[:, LANES:2 * LANES] - bf_ref[rows, :])
        khb = k * jnp.exp(tot[:, 2 * LANES:] - bb_ref[rows, :])
        u = _dot(_bf(jnp.concatenate([khf, khb], axis=1).T), _bf(v))
        for d in range(2):
            u_d = u[d * KC:(d + 1) * KC, :]
            if chained:
                uG_ref[c, d] = u_d * bd_c
            else:
                for hh in range(H_C):
                    oG_ref[c, d, hh] = u_d[hh * DK_C:(hh + 1) * DK_C, hh * DV_C:(hh + 1) * DV_C]
        return carry

    def gla_state_ret_proj(c, carry):
        return gla_state_body(c, ret_proj_body(c, carry))

    if chained:
        lax.fori_loop(0, NCH, gla_state_ret_proj, 0, unroll=CHUNK_UNROLL)

    if chained:
        for d, order in ((0, range(NCH)), (1, range(NCH - 1, -1, -1))):
            s_run = _block_diag_value(sG_ref.at[d], H_C, DK_C, DV_C)
            for idx, c in enumerate(order):
                u = uG_ref[c, d]
                uG_ref[c, d] = s_run
                if idx != NCH - 1:
                    tot_row = tot_ref[c][:, (1 + d) * LANES:(2 + d) * LANES]
                    s_run = _row_to_col(jnp.exp(tot_row), KC) * s_run + u

    def gla_out_body(c, carry, factorised):
        rows = pl.ds(pl.multiple_of(c * CH, CH), CH)

        def operands():
            q = y_ref[rows, CQ:CQ + KC]
            k = y_ref[rows, CK:CK + KC] * gla_scale
            v = y_ref[rows, CV:CV + D_C]
            b_f, b_b = bf_ref[rows, :], bb_ref[rows, :]
            return q * jnp.exp(b_f), q * jnp.exp(b_b), k, v, b_f, b_b

        def inter_chunk(qf, qb):
            if not chained:
                return jnp.zeros((CH, D_C), F32)
            s_in = jnp.concatenate([uG_ref[c, 0], uG_ref[c, 1]], axis=0)
            return _dot(_bf(jnp.concatenate([qf, qb], axis=1)), _bf(s_in))

        qf, qb, k, v, b_f, b_b = operands()
        if rotary:
            rotary_body(c, 0)
        if factorised:
            lower, upper = _causal_masks()
            kf16, kb16 = _bf(k * jnp.exp(-b_f)), _bf(k * jnp.exp(-b_b))
            ps = []
            for hh in range(H_C):
                s_f = _dot_nt(_bf(qf * hm_ck[hh]), kf16)
                s_b = _dot_nt(_bf(qb * hm_ck[hh]), kb16)
                ps.append(_bf(jnp.where(lower, s_f, 0.0) + jnp.where(upper, s_b, 0.0)))
            vst = jnp.concatenate([_bf(v * hm_cv[hh]) for hh in range(H_C)], axis=0)
            yc = _dot(jnp.concatenate(ps, axis=1), vst)
            y_ref[rows, D_A + D_B:D_MODEL] = yc + inter_chunk(qf, qb) if chained else yc
            return carry

        y_ref[rows, D_A + D_B:D_MODEL] = inter_chunk(qf, qb)
        key = lax.broadcasted_iota(jnp.int32, (CH, 1), 0)
        sub = 8

        def query_body(g, carry2):
            grp = pl.ds(pl.multiple_of(c * CH + g * sub, sub), sub)
            bq_f, bq_b, qg = bf_ref[grp, :], bb_ref[grp, :], y_ref[grp, CQ:CQ + KC]
            outs = []
            for r in range(sub):
                i = g * sub + r
                w_f = jnp.exp(jnp.where(key <= i, bq_f[r:r + 1, :] - b_f, NEG))
                w_b = jnp.exp(jnp.where(key >= i, bq_b[r:r + 1, :] - b_b, NEG))
                t = _dot(_bf(qg[r:r + 1, :] * k * (w_f + w_b)), _bf(bd_c))
                outs.append(jnp.sum(t * v, axis=0, keepdims=True))
            y_ref[grp, D_A + D_B:D_MODEL] += jnp.concatenate(outs, axis=0)
            return carry2

        lax.fori_loop(0, CH // sub, query_body, 0)
        return carry

    all_safe = jnp.min(tot_ref[...][:, :, LANES:]) > GLA_SAFE_LOG

    @pl.when(all_safe)
    def _():
        chunk_loop(gla_state_ret_proj, functools.partial(gla_out_body, factorised=True))

    @pl.when(jnp.logical_not(all_safe))
    def _():
        chunk_loop(gla_state_ret_proj, functools.partial(gla_out_body, factorised=False), unroll=1)

    if chained:
        lax.fori_loop(0, NCH, ret_state_body, 0, unroll=CHUNK_UNROLL)
        ret_scan()
    chunk_loop(ret_state_body, ret_out_body)

    def gate_proj(c):
        rows = slice(c * CH, (c + 1) * CH)
        pj_ref[rows, 0:W_Z] = _dot(h_ref[rows, :], win_ref[:, OFF_Z:OFF_Z + W_Z])

    def head_norm_gate(c):
        rows = slice(c * CH, (c + 1) * CH)
        parts = []
        for hh in range(H_A):
            ya = y_ref[rows, hh * DH_A:(hh + 1) * DH_A]
            parts.append(ya * lax.rsqrt(jnp.mean(ya * ya, axis=1, keepdims=True) + EPS))
        ybc = y_ref[rows, D_A:D_MODEL]
        seg = _bf(_block_diag_mask(D_B + D_C, D_B + D_C, DH_B, DH_B) * (1.0 / DH_B))
        parts.append(ybc * lax.rsqrt(_exact_dot_r(ybc * ybc, seg) + EPS))
        z = pj_ref[rows, 0:W_Z]
        yg = jnp.concatenate(parts, axis=1) * hng_ref[...] * (z * _sigmoid(z))
        h_ref[rows, :] = _bf(yg)

    def out_proj_residual(c):
        rows = slice(c * CH, (c + 1) * CH)
        xo = x_rows(x_ref, c)[...] + mod_ref[:, 2 * D_MODEL:] * _dot(h_ref[rows, :], wout_ref[...])
        if final:
            xo = xo * lax.rsqrt(jnp.mean(xo * xo, axis=-1, keepdims=True) + EPS) * fg_ref[...]
        x_rows(out_ref, c)[...] = xo

    gate_proj(0)
    for c in range(NCH):
        if c + 1 < NCH:
            gate_proj(c + 1)
        head_norm_gate(c)
        if c > 0:
            out_proj_residual(c - 1)
    out_proj_residual(NCH - 1)


def _layer_call(l, x, mods, mod_row, norm_g, w_in_r, gate_b, lg_rows, w2_full, b2_full, hn_g, w_out_b, final_g,
                states=None, rot=None, final=False):
    batch, seqlen, d = x.shape
    chained = states is not None
    seqs = TOK // seqlen
    steps = batch // seqs
    assert seqs * seqlen == TOK and steps * seqs == batch and seqlen % CH == 0
    assert chained == (seqs == 1)
    assert chained or seqlen == CH

    def const(shape):
        nd = len(shape)
        return pl.BlockSpec(shape, lambda i, _nd=nd: (0,) * _nd)

    def layer_block(shape):
        nd = len(shape)
        return pl.BlockSpec((None,) + shape, lambda i, _nd=nd: (l,) + (0,) * _nd)

    in_specs = [
        pl.BlockSpec((seqs, seqlen, d), lambda i: (i, 0, 0)),
        pl.BlockSpec((None, None, 1, 3 * d),
                     (lambda i: (l, i, 0, 0)) if mod_row is None else (lambda i: (l, mod_row, 0, 0))),
        layer_block((1, d)),
        pl.BlockSpec((None, d, D_INR), lambda i: (l, 0, 0), pipeline_mode=pl.Buffered(1)),
        layer_block((1, LANES)),
        layer_block((2, D_B)),
        layer_block((LANES, 2 * LANES)),
        layer_block((1, 2 * LANES)),
        layer_block((1, d)),
        pl.BlockSpec((None, d, d), lambda i: (l, 0, 0), pipeline_mode=pl.Buffered(1)),
    ]
    args = [x, mods, norm_g.reshape(-1, 1, d), w_in_r, gate_b, lg_rows, w2_full, b2_full,
            hn_g.reshape(-1, 1, d), w_out_b]
    for cst in _gate_constants():
        in_specs.append(pl.BlockSpec(cst.shape, lambda i, _nd=cst.ndim: (0,) * _nd, pipeline_mode=pl.Buffered(1)))
        args.append(cst)
    if final:
        in_specs.append(const((1, d)))
        args.append(final_g.reshape(1, d))
    if rot is not None:
        in_specs += [pl.BlockSpec((seqlen, D_B), lambda i: (0, 0), pipeline_mode=pl.Buffered(1))] * 2
        args += list(rot)
    state_blocks = [(2, H_A, DH_A, DH_A), (2, H_A, DH_A), (1, LANES), (2, H_B, DH_B, DH_B), (2, H_C, DK_C, DV_C)]
    out_shape = [jax.ShapeDtypeStruct(x.shape, x.dtype)]
    out_specs = [pl.BlockSpec((seqs, seqlen, d), lambda i: (i, 0, 0))]
    for k, blk in enumerate(state_blocks):
        zeros = (0,) * len(blk)
        if chained:
            in_specs.append(pl.BlockSpec((None, None) + blk, lambda i, _z=zeros: (i, l) + _z))
            args.append(states[k])
        else:
            out_shape.append(jax.ShapeDtypeStruct((batch,) + blk, F32))
            out_specs.append(pl.BlockSpec((seqs,) + blk, lambda i, _z=zeros: (i,) + _z))

    scratch = [
        pltpu.VMEM((TOK, d), BF16),
        pltpu.VMEM((TOK, W_A), F32),
        pltpu.VMEM((TOK, LANES), F32),
        pltpu.VMEM((TOK, LANES), F32),
        pltpu.VMEM((TOK, LANES), F32),
        pltpu.VMEM((TOK, LANES), F32),
        pltpu.VMEM((NCH, 1, 3 * LANES), F32),
        pltpu.VMEM((TOK, d), F32),
        pltpu.VMEM((NCH, 1, LANES), F32),
        pltpu.VMEM((NCH, 1, LANES), F32),
        pltpu.VMEM((TOK, LANES), F32),
        pltpu.VMEM((TOK, LANES), BF16),
    ]
    if chained:
        scratch += [
            pltpu.VMEM((NCH, 2, H_A, DH_A, 2 * DH_A), F32),
            pltpu.VMEM((NCH, 2, D_B, D_B), F32),
            pltpu.VMEM((NCH, 2, H_C * DK_C, D_C), F32),
        ]
    outs = pl.pallas_call(
        functools.partial(_layer_kernel, chained=chained, rotary=rot is not None, final=final),
        out_shape=out_shape,
        grid=(steps,),
        in_specs=in_specs,
        out_specs=out_specs,
        scratch_shapes=scratch,
        compiler_params=pltpu.CompilerParams(dimension_semantics=("arbitrary",), vmem_limit_bytes=VMEM_LIMIT),
        name=("latent" if chained else "context") + f"_layer{l}",
    )(*args)
    return outs


def _rotary_tables(seqlen):
    rows = seqlen // GRID_W
    r = jnp.repeat(jnp.arange(rows, dtype=F32), GRID_W)
    col = jnp.tile(jnp.arange(GRID_W, dtype=F32), rows)
    n_f = DH_B // 4
    freqs = ROPE_BASE ** (-jnp.arange(n_f, dtype=F32) / n_f)
    ang = jnp.concatenate([r[:, None] * freqs, col[:, None] * freqs], axis=-1)
    cos, sin = jnp.cos(ang), jnp.sin(ang)
    cos_l = jnp.tile(jnp.concatenate([cos, cos], axis=-1), (1, H_B))
    sin_l = jnp.tile(jnp.concatenate([-sin, sin], axis=-1), (1, H_B))
    return cos_l, sin_l


def _gate_lanes(m):
    z = jnp.zeros(m.shape[:-2] + (LANES,), m.dtype)
    z = z.at[..., COL_FF:COL_FF + H_A].set(m[..., 0, :]).at[..., COL_FB:COL_FB + H_A].set(m[..., 1, :])
    return z[..., None, :]


def _pack_segments():
    src = dict(zip("aq ak av ao az ag bq bk bv bz cq ck cv cz clr".split(),
                   zip(np.cumsum((0,) + IN_WIDTHS[:-1]).tolist(), IN_WIDTHS)))
    segs = []
    for names, dst in (("aq ak av ao", OFF_A), ("ag clr", OFF_S), ("bq bk bv", OFF_B), ("cq ck cv", OFF_C),
                       ("az bz cz", OFF_Z)):
        for name in names.split():
            s, w = src[name]
            segs.append((s, dst, w))
            dst += w
    return segs


def _pack_kernel(wt_ref, o_ref):
    tr = o_ref.shape[0]
    small = []
    for s, dst, w in _pack_segments():
        if w % LANES == 0:
            o_ref[:, dst:dst + w] = _bf(wt_ref[s:s + w, :].T)
        else:
            small.append(wt_ref[s:s + w, :])
    small.append(jnp.zeros((W_S - sum(t.shape[0] for t in small), tr), F32))
    o_ref[:, OFF_S:OFF_S + W_S] = _bf(jnp.concatenate(small, axis=0).T)


def _pack_w_in(w_in):
    depth, d, d_in = w_in.shape
    tr = 256
    return pl.pallas_call(
        _pack_kernel,
        out_shape=jax.ShapeDtypeStruct((depth, d, D_INR), BF16),
        grid=(depth, d // tr),
        in_specs=[pl.BlockSpec((None, d_in, tr), lambda l, r: (l, 0, r))],
        out_specs=pl.BlockSpec((None, tr, D_INR), lambda l, r: (l, r, 0)),
        name="pack_w_in",
    )(jnp.swapaxes(w_in, 1, 2))


def kernel(x_prompt, x_sample, state_mlstm_C, state_mlstm_n, state_mlstm_m, state_ret, state_gla, c, c_ctx,
           norm_g, w_ada, b_ada, w_in, mlstm_gate_b, ret_decay_logit, gla_w2, gla_b2, headnorm_g, w_out, final_g):
    depth = w_in.shape[0]
    dec_batch = c.shape[0]

    w_in_r = _pack_w_in(w_in)
    w_out_b = w_out.astype(BF16)
    n_g = mlstm_gate_b.shape[-1]
    gate_b = jnp.pad(mlstm_gate_b, ((0, 0), (0, LANES - n_g)))[:, None, :]
    lg_rows = jnp.repeat(ret_decay_logit, DH_B, axis=-1)
    w2_full = jnp.zeros((depth, LANES, 2 * LANES), F32)
    w2_full = w2_full.at[:, n_g:n_g + GLA_RANK, 0:LANES].set(gla_w2[:, 0])
    w2_full = w2_full.at[:, n_g + GLA_RANK:n_g + 2 * GLA_RANK, LANES:].set(gla_w2[:, 1]).astype(BF16)
    b2_full = gla_b2.reshape(depth, 1, 2 * LANES)

    cstack = jnp.zeros((16, D_MODEL), F32).at[0:dec_batch].set(c).at[dec_batch].set(c_ctx)
    mods = _modulation(cstack, w_ada, b_ada)[:, :, None, :]

    rot = _rotary_tables(x_sample.shape[1])
    cache = (state_mlstm_C, state_mlstm_n, _gate_lanes(state_mlstm_m), state_ret, state_gla)

    xp, xs = x_prompt, x_sample
    new = []
    for l in range(depth):
        final = l == depth - 1
        common = (norm_g, w_in_r, gate_b, lg_rows, w2_full, b2_full, headnorm_g, w_out_b, final_g)
        xp, *new_l = _layer_call(l, xp, mods, dec_batch, *common, final=final)
        new.append(new_l)
        xs = _layer_call(l, xs, mods, None, *common, states=cache, rot=rot, final=final)[0]

    new_c, new_n, m_l, new_r, new_g = (jnp.stack(per_layer, axis=1) for per_layer in zip(*new))
    new_m = jnp.stack([m_l[:, :, 0, COL_FF:COL_FF + H_A], m_l[:, :, 0, COL_FB:COL_FB + H_A]], axis=2)
    return (xp, xs, new_c, new_n, new_m, new_r, new_g)
```

```python
import functools

import jax
import jax.numpy as jnp
import numpy as np
from jax import lax
from jax.experimental import pallas as pl
from jax.experimental.pallas import tpu as pltpu

F32 = jnp.float32
BF16 = jnp.bfloat16

D_MODEL = 1024
H_A, DH_A = 4, 128
H_B, DH_B = 4, 64
H_C, DK_C, DV_C = 4, 32, 64
D_A, D_B, D_C = H_A * DH_A, H_B * DH_B, H_C * DV_C
GLA_RANK = 16
GLA_TAU = 16.0
GRID_W = 64
ROPE_BASE = 10000.0
EPS = 1e-6
IN_WIDTHS = (D_A, D_A, D_A, D_A, D_A, 4 * H_A, D_B, D_B, D_B, D_B, H_C * DK_C, H_C * DK_C, D_C, D_C, 2 * GLA_RANK)

LANES = 128
CH = 256
TOK = 1024
NCH = TOK // CH
CHUNK_UNROLL = 4
VMEM_LIMIT = 62 * 1024 * 1024
NEG = -1e30
GLA_SAFE_LOG = -80.0

OFF_A, W_A = 0, 4 * D_A
OFF_S, W_S = OFF_A + W_A, LANES
OFF_B, W_B = OFF_S + W_S, 3 * D_B
OFF_C, W_C = OFF_B + W_B, 2 * H_C * DK_C + D_C
OFF_Z, W_Z = OFF_C + W_C, D_MODEL
D_INR = OFF_Z + W_Z
COL_FF, COL_FB = 4, 12


def _dot(a, b):
    return jnp.dot(a, b, preferred_element_type=F32)


def _dot_nt(a, b):
    return lax.dot_general(a, b, (((1,), (1,)), ((), ())), preferred_element_type=F32)


def _bf(x):
    return x.astype(BF16)


def _split3(x):
    hi = _bf(x)
    r1 = x - hi.astype(F32)
    mid = _bf(r1)
    lo = _bf(r1 - mid.astype(F32))
    return hi, mid, lo


def _exact_dot_l(m_bf, x):
    hi, mid, lo = _split3(x)
    return (_dot(m_bf, lo) + _dot(m_bf, mid)) + _dot(m_bf, hi)


def _exact_dot_r(x, m_bf):
    hi, mid, lo = _split3(x)
    return (_dot(lo, m_bf) + _dot(mid, m_bf)) + _dot(hi, m_bf)


def _log_sigmoid(x):
    return jnp.minimum(x, 0.0) - jnp.log(1.0 + jnp.exp(-jnp.abs(x)))


def _sigmoid(x):
    return 1.0 / (1.0 + jnp.exp(-x))


def _row_to_col(row, n):
    eye = lax.broadcasted_iota(jnp.int32, (n, n), 0) == lax.broadcasted_iota(jnp.int32, (n, n), 1)
    return jnp.sum(jnp.where(eye, row, 0.0), axis=1, keepdims=True)


def _lane_block_mask(width, block, h):
    lane = lax.broadcasted_iota(jnp.int32, (1, width), 1)
    return ((lane >= h * block) & (lane < (h + 1) * block)).astype(F32)


def _log2(n):
    assert n & (n - 1) == 0
    return n.bit_length() - 1


def _block_diag_mask(rows, cols, rblock, cblock):
    r = lax.broadcasted_iota(jnp.int32, (rows, cols), 0) >> _log2(rblock)
    c = lax.broadcasted_iota(jnp.int32, (rows, cols), 1) >> _log2(cblock)
    return (r == c).astype(F32)


def _causal_masks():
    ri = lax.broadcasted_iota(jnp.int32, (CH, CH), 0)
    cj = lax.broadcasted_iota(jnp.int32, (CH, CH), 1)
    return cj <= ri, cj >= ri


def _bwd_gate_lanes():
    lane = lax.broadcasted_iota(jnp.int32, (1, LANES), 1)
    return (lane >= COL_FB) & (lane < COL_FB + H_A)


def _block_diag_value(ref, hn, a, b):
    rows = []
    for h in range(hn):
        wide = jnp.concatenate([ref[h], jnp.zeros((a, (hn - 1) * b), F32)], axis=1)
        rows.append(wide if h == 0 else pltpu.roll(wide, h * b, 1))
    return jnp.concatenate(rows, axis=0)


def _pieces(x):
    return jnp.concatenate(_split3(x), axis=1)


def _cum_max(x, reverse):
    n = x.shape[0]
    row = lax.broadcasted_iota(jnp.int32, x.shape, 0)
    s = 1
    while s < n:
        if reverse:
            shifted = jnp.where(row < n - s, pltpu.roll(x, n - s, 0), NEG)
        else:
            shifted = jnp.where(row >= s, pltpu.roll(x, s, 0), NEG)
        x = jnp.maximum(x, shifted)
        s *= 2
    return x


GATE_BLOCK = 16
N_PIECES = 3


def _gate_constants():
    half = N_PIECES * GATE_BLOCK
    cols = [COL_FF + h for h in range(H_A)] + [COL_FB + h for h in range(H_A)]
    sp = np.zeros((2 * N_PIECES * LANES, LANES), np.float32)
    for part in range(2):
        for p in range(N_PIECES):
            for k in cols:
                sp[(part * N_PIECES + p) * LANES + k, part * half + p * GATE_BLOCK + k] = 1.0
    lane = np.arange(LANES)
    gl = np.zeros((16, LANES), np.float32)
    for hd, k in enumerate(cols):
        gl[hd] = (lane % GATE_BLOCK == k) & (lane < 2 * half)
    valid = np.isin(lane % GATE_BLOCK, cols)
    gl[2 * H_A] = valid & (lane < half)
    gl[2 * H_A + 1] = valid & (lane >= half) & (lane < 2 * half)
    return jnp.asarray(sp, BF16), jnp.asarray(gl, F32)


def _modulation_kernel(c_ref, w_ref, b_ref, o_ref):
    cv = c_ref[...]
    s = cv * _sigmoid(cv)
    o_ref[...] = _dot(_bf(s), _bf(w_ref[...])) + b_ref[...]


def _modulation(cstack, w_ada, b_ada):
    depth, d, d3 = w_ada.shape
    rows = cstack.shape[0]
    tn = 1024
    return pl.pallas_call(
        _modulation_kernel,
        out_shape=jax.ShapeDtypeStruct((depth, rows, d3), F32),
        grid=(depth, d3 // tn),
        in_specs=[
            pl.BlockSpec((rows, d), lambda l, j: (0, 0)),
            pl.BlockSpec((None, d, tn), lambda l, j: (l, 0, j)),
            pl.BlockSpec((None, 1, tn), lambda l, j: (l, 0, j)),
        ],
        out_specs=pl.BlockSpec((None, rows, tn), lambda l, j: (l, 0, j)),
        name="adaln_modulation",
    )(cstack, w_ada, b_ada.reshape(depth, 1, d3))


def _layer_kernel(*refs, chained, rotary, final):
    it = iter(refs)
    x_ref, mod_ref, ng_ref, win_ref, gb_ref, lg_ref, w2_ref, b2_ref, hng_ref, wout_ref = (next(it) for _ in range(10))
    sp_ref, gl_ref = next(it), next(it)
    fg_ref = next(it) if final else None
    if rotary:
        cos_ref, sin_ref = next(it), next(it)
    sC_ref, sn_ref, sm_ref_in, sR_ref, sG_ref = (next(it) for _ in range(5))
    out_ref = next(it)
    if not chained:
        oC_ref, on_ref, om_ref, oR_ref, oG_ref = (next(it) for _ in range(5))
    (h_ref, pj_ref, sm_ref, fc_ref, bf_ref, bb_ref, tot_ref, y_ref,
     gm_ref, min_ref, cm_ref, bop_ref) = (next(it) for _ in range(12))
    dec_ref = pj_ref.at[:, W_B:W_B + CH]
    if chained:
        uCN_ref, uR_ref, uG_ref = (next(it) for _ in range(3))

    seqs, seqlen, _ = x_ref.shape

    def x_rows(ref, c):
        if isinstance(c, int):
            r0 = c * CH
            return ref.at[r0 // seqlen, r0 % seqlen:r0 % seqlen + CH, :]
        r0 = pl.multiple_of(c * CH, CH)
        return ref.at[r0 >> _log2(seqlen), pl.ds(pl.multiple_of(r0 & (seqlen - 1), CH), CH), :]

    def norm_rows(c):
        x = x_rows(x_ref, c)[...]
        shift, scale = mod_ref[:, 0:D_MODEL], mod_ref[:, D_MODEL:2 * D_MODEL]
        xn = x * lax.rsqrt(jnp.mean(x * x, axis=-1, keepdims=True) + EPS) * ng_ref[...]
        h_ref[c * CH:(c + 1) * CH, :] = _bf(xn * (1.0 + scale) + shift)

    def front_rows(c):
        rows = slice(c * CH, (c + 1) * CH)
        proj = _dot(h_ref[rows, :], win_ref[:, OFF_A:OFF_S + W_S])
        pj_ref[rows, 0:W_A] = proj[:, 0:W_A]
        sm = proj[:, W_A:]
        sm_ref[rows, :] = sm
        la = _log_sigmoid(_dot(_bf(sm), w2_ref[...]) + b2_ref[...]) * (1.0 / GLA_TAU)
        tri = _bf(_causal_masks()[0].astype(F32))
        ls = _log_sigmoid(sm + gb_ref[...])
        ps = _exact_dot_l(tri, jnp.concatenate([ls, la], axis=1))
        tot = ps[CH - 1:CH, :]
        fc_ref[rows, :] = jnp.where(_bwd_gate_lanes(), tot[:, 0:LANES] - ps[:, 0:LANES] + ls, ps[:, 0:LANES])
        bf_ref[rows, :] = ps[:, LANES:2 * LANES]
        bb_ref[rows, :] = tot[:, 2 * LANES:] - ps[:, 2 * LANES:] + la[:, LANES:]
        tot_ref[c] = tot

    norm_rows(0)
    for c in range(NCH):
        if c + 1 < NCH:
            norm_rows(c + 1)
        front_rows(c)

    AQ, AK, AV, AO = 0, D_A, 2 * D_A, 3 * D_A
    k_scale = DH_A ** -0.5

    ones_cols = jnp.ones((CH, DH_A), F32)

    def mlstm_state_body(c, carry):
        rows = pl.ds(pl.multiple_of(c * CH, CH), CH)
        g_pre = sm_ref[rows, :] + gb_ref[...]
        rw = pltpu.roll(g_pre, 4, 1) - fc_ref[rows, :]
        cm_ref[rows, :] = jnp.where(_bwd_gate_lanes(), _cum_max(rw, True), _cum_max(rw, False))
        bop_ref[rows, :] = _bf(_dot(_pieces(rw), sp_ref[3 * LANES:, :]) + gl_ref[2 * H_A:2 * H_A + 1, :])
        totg = tot_ref[c][:, 0:LANES]
        g = totg + rw
        gm = jnp.max(g, axis=0, keepdims=True)
        gm_ref[c] = gm
        wgt = jnp.exp(g - gm).T
        y_ref[rows, D_A:D_MODEL] = _dot(h_ref[rows, :], win_ref[:, OFF_C:OFF_C + W_C])
        if not chained:
            m_new = jnp.maximum(totg, gm)
            b_row = jnp.exp(gm - m_new)
            om_ref[c] = m_new
            min_ref[c] = jnp.zeros((1, LANES), F32)
        for hh in range(H_A):
            kt = (pj_ref[rows, AK + hh * DH_A:AK + (hh + 1) * DH_A] * k_scale).T
            v_h = pj_ref[rows, AV + hh * DH_A:AV + (hh + 1) * DH_A]
            lhs = jnp.concatenate([kt * wgt[COL_FF + hh:COL_FF + hh + 1, :],
                                   kt * wgt[COL_FB + hh:COL_FB + hh + 1, :]], axis=0)
            u = _dot(_bf(lhs), _bf(jnp.concatenate([v_h, ones_cols], axis=1)))
            for d, col in ((0, COL_FF), (1, COL_FB)):
                u_d = u[d * DH_A:(d + 1) * DH_A, :]
                if chained:
                    uCN_ref[c, d, hh] = u_d
                else:
                    fin = b_row[:, col + hh:col + hh + 1] * u_d
                    oC_ref[c, d, hh] = fin[:, 0:DH_A]
                    on_ref[c, d, hh:hh + 1, :] = fin[:, DH_A:].T[0:1, :]
        return carry

    def chunk_loop(state_body, out_body, unroll=CHUNK_UNROLL):
        if chained:
            return lax.fori_loop(0, NCH, out_body, 0, unroll=unroll)
        return lax.fori_loop(0, NCH, lambda c, carry: out_body(c, state_body(c, carry)), 0, unroll=unroll)

    if chained:
        lax.fori_loop(0, NCH, mlstm_state_body, 0, unroll=CHUNK_UNROLL)

    if chained:
        for d, order, col in ((0, range(NCH), COL_FF), (1, range(NCH - 1, -1, -1), COL_FB)):
            m_run = sm_ref_in[...]
            cn_run = [jnp.concatenate([sC_ref[d, hh],
                                       jnp.broadcast_to(sn_ref[d, hh:hh + 1, :], (DH_A, DH_A)).T], axis=1)
                      for hh in range(H_A)]
            for idx, c in enumerate(order):
                last = idx == NCH - 1
                if not last:
                    totg, gm = tot_ref[c][:, 0:LANES], gm_ref[c]
                    m_new = jnp.maximum(totg + m_run, gm)
                    a_row, b_row = jnp.exp(totg + m_run - m_new), jnp.exp(gm - m_new)
                for hh in range(H_A):
                    u = uCN_ref[c, d, hh]
                    uCN_ref[c, d, hh] = cn_run[hh]
                    if not last:
                        cn_run[hh] = (a_row[:, col + hh:col + hh + 1] * cn_run[hh]
                                      + b_row[:, col + hh:col + hh + 1] * u)
                min_ref[c] = m_run if d == 0 else jnp.where(_bwd_gate_lanes(), m_run, min_ref[c])
                if not last:
                    m_run = m_new

    def mlstm_out_body(c, carry):
        rows = pl.ds(pl.multiple_of(c * CH, CH), CH)
        lower, upper = _causal_masks()
        m_prev = min_ref[c]
        cmx = jnp.maximum(cm_ref[rows, :], m_prev)
        mi = fc_ref[rows, :] + cmx
        a_op = _bf(_dot(_pieces(-cmx), sp_ref[0:3 * LANES, :]) + gl_ref[2 * H_A + 1:2 * H_A + 2, :])
        b_op = bop_ref[rows, :]
        floor_all = jnp.exp(-mi)
        wa_all = jnp.exp(m_prev - cmx)
        for hh in range(H_A):
            q16 = _bf(pj_ref[rows, AQ + hh * DH_A:AQ + (hh + 1) * DH_A])
            k_h = pj_ref[rows, AK + hh * DH_A:AK + (hh + 1) * DH_A] * k_scale
            v_h = pj_ref[rows, AV + hh * DH_A:AV + (hh + 1) * DH_A]
            o_h = pj_ref[rows, AO + hh * DH_A:AO + (hh + 1) * DH_A]
            s = _dot_nt(q16, _bf(k_h))
            qs = []
            for d, msk in ((0, lower), (1, upper)):
                hd = d * H_A + hh
                dm = _dot_nt(a_op * _bf(gl_ref[hd:hd + 1, :]), b_op)
                qs.append(_bf(s * jnp.exp(jnp.where(msk, dm, NEG))))
            nd = _dot(jnp.concatenate(qs, axis=0), _bf(jnp.concatenate([v_h, ones_cols], axis=1)))
            hs = None
            for d, col in ((0, COL_FF + hh), (1, COL_FB + hh)):
                num, den = nd[d * CH:(d + 1) * CH, 0:DH_A], nd[d * CH:(d + 1) * CH, DH_A:]
                if chained:
                    wa = jnp.broadcast_to(wa_all[:, col:col + 1], (CH, DH_A))
                    inter = _dot(q16, _bf(uCN_ref[c, d, hh]))
                    num = num + wa * inter[:, 0:DH_A]
                    den = den + wa * inter[:, DH_A:]
                floor = jnp.broadcast_to(floor_all[:, col:col + 1], (CH, DH_A))
                part = num / jnp.maximum(jnp.abs(den), floor)
                hs = part if hs is None else hs + part
            y_ref[rows, hh * DH_A:(hh + 1) * DH_A] = _sigmoid(o_h) * hs
        return carry

    chunk_loop(mlstm_state_body, mlstm_out_body)

    BQ, BK, BV = 0, D_B, 2 * D_B

    def ret_proj_body(c, carry):
        rows = pl.ds(pl.multiple_of(c * CH, CH), CH)
        pj_ref[rows, 0:W_B] = _dot(h_ref[rows, :], win_ref[:, OFF_B:OFF_B + W_B])
        return carry

    def rotary_body(c, carry):
        rows = pl.ds(pl.multiple_of(c * CH, CH), CH)
        first_half = (lax.broadcasted_iota(jnp.int32, (1, D_B), 1) & (DH_B - 1)) < DH_B // 2
        for off in (BQ, BK):
            t = pj_ref[rows, off:off + D_B]
            partner = jnp.where(first_half, pltpu.roll(t, D_B - DH_B // 2, 1), pltpu.roll(t, DH_B // 2, 1))
            pj_ref[rows, off:off + D_B] = t * cos_ref[rows, :] + partner * sin_ref[rows, :]
        return carry

    lgam = _log_sigmoid(lg_ref[...])
    lg_f, lg_b = lgam[0:1, :], lgam[1:2, :]
    pos = lax.broadcasted_iota(jnp.int32, (CH, 1), 0).astype(F32)
    lower, upper = _causal_masks()
    rel = (lax.broadcasted_iota(jnp.int32, (CH, CH), 0) - lax.broadcasted_iota(jnp.int32, (CH, CH), 1)).astype(F32)
    for hh in range(H_B):
        lf = lg_f[:, hh * DH_B:hh * DH_B + 1]
        lb = lg_b[:, hh * DH_B:hh * DH_B + 1]
        dec_ref[hh * CH:(hh + 1) * CH, :] = (jnp.where(lower, jnp.exp(jnp.maximum(rel, 0.0) * lf), 0.0)
                                             + jnp.where(upper, jnp.exp(jnp.maximum(-rel, 0.0) * lb), 0.0))
    bd_b = _block_diag_mask(D_B, D_B, DH_B, DH_B)
    hm_b = [_lane_block_mask(D_B, DH_B, hh) for hh in range(H_B)]
    ret_scale = DH_B ** -0.5

    def ret_state_body(c, carry):
        rows = pl.ds(pl.multiple_of(c * CH, CH), CH)
        k = pj_ref[rows, BK:BK + D_B] * ret_scale
        v = pj_ref[rows, BV:BV + D_B]
        kf = k * jnp.exp((CH - 1.0 - pos) * lg_f)
        kb = k * jnp.exp(pos * lg_b)
        u = _dot(_bf(jnp.concatenate([kf, kb], axis=1).T), _bf(v))
        for d in range(2):
            u_d = u[d * D_B:(d + 1) * D_B, :]
            if chained:
                uR_ref[c, d] = u_d * bd_b
            else:
                for hh in range(H_B):
                    oR_ref[c, d, hh] = u_d[hh * DH_B:(hh + 1) * DH_B, hh * DH_B:(hh + 1) * DH_B]
        return carry

    def ret_scan():
        for d, order, lg_row in ((0, range(NCH), lg_f), (1, range(NCH - 1, -1, -1), lg_b)):
            g_col = _row_to_col(jnp.exp(float(CH) * lg_row), D_B)
            s_run = _block_diag_value(sR_ref.at[d], H_B, DH_B, DH_B)
            for idx, c in enumerate(order):
                u = uR_ref[c, d]
                uR_ref[c, d] = s_run
                if idx != NCH - 1:
                    s_run = g_col * s_run + u

    def ret_out_body(c, carry):
        rows = pl.ds(pl.multiple_of(c * CH, CH), CH)
        q = pj_ref[rows, BQ:BQ + D_B]
        k = pj_ref[rows, BK:BK + D_B] * ret_scale
        v = pj_ref[rows, BV:BV + D_B]
        kb16 = _bf(k)
        ps = [_bf(_dot_nt(_bf(q * hm_b[hh]), kb16) * dec_ref[hh * CH:(hh + 1) * CH, :]) for hh in range(H_B)]
        vst = jnp.concatenate([_bf(v * hm_b[hh]) for hh in range(H_B)], axis=0)
        yb = _dot(jnp.concatenate(ps, axis=1), vst)
        if chained:
            qf = q * jnp.exp((pos + 1.0) * lg_f)
            qb = q * jnp.exp((float(CH) - pos) * lg_b)
            s_in = jnp.concatenate([uR_ref[c, 0], uR_ref[c, 1]], axis=0)
            yb = yb + _dot(_bf(jnp.concatenate([qf, qb], axis=1)), _bf(s_in))
        y_ref[rows, D_A:D_A + D_B] = yb
        return carry

    KC = H_C * DK_C
    CQ, CK, CV = D_A, D_A + KC, D_A + 2 * KC
    gla_scale = DK_C ** -0.5
    bd_c = _block_diag_mask(KC, D_C, DK_C, DV_C)
    hm_ck = [_lane_block_mask(KC, DK_C, hh) for hh in range(H_C)]
    hm_cv = [_lane_block_mask(D_C, DV_C, hh) for hh in range(H_C)]

    def gla_state_body(c, carry):
        rows = pl.ds(pl.multiple_of(c * CH, CH), CH)
        k = y_ref[rows, CK:CK + KC] * gla_scale
        v = y_ref[rows, CV:CV + D_C]
        tot = tot_ref[c]
        khf = k * jnp.exp(tot[:, LANES:2 * LANES] - bf_ref[rows, :])
        khb = k * jnp.exp(tot[:, 2 * LANES:] - bb_ref[rows, :])
        u = _dot(_bf(jnp.concatenate([khf, khb], axis=1).T), _bf(v))
        for d in range(2):
            u_d = u[d * KC:(d + 1) * KC, :]
            if chained:
                uG_ref[c, d] = u_d * bd_c
            else:
                for hh in range(H_C):
                    oG_ref[c, d, hh] = u_d[hh * DK_C:(hh + 1) * DK_C, hh * DV_C:(hh + 1) * DV_C]
        return carry

    def gla_state_ret_proj(c, carry):
        return gla_state_body(c, ret_proj_body(c, carry))

    if chained:
        lax.fori_loop(0, NCH, gla_state_ret_proj, 0, unroll=CHUNK_UNROLL)

    if chained:
        for d, order in ((0, range(NCH)), (1, range(NCH - 1, -1, -1))):
            s_run = _block_diag_value(sG_ref.at[d], H_C, DK_C, DV_C)
            for idx, c in enumerate(order):
                u = uG_ref[c, d]
                uG_ref[c, d] = s_run
                if idx != NCH - 1:
                    tot_row = tot_ref[c][:, (1 + d) * LANES:(2 + d) * LANES]
                    s_run = _row_to_col(jnp.exp(tot_row), KC) * s_run + u

    def gla_out_body(c, carry, factorised):
        rows = pl.ds(pl.multiple_of(c * CH, CH), CH)

        def operands():
            q = y_ref[rows, CQ:CQ + KC]
            k = y_ref[rows, CK:CK + KC] * gla_scale
            v = y_ref[rows, CV:CV + D_C]
            b_f, b_b = bf_ref[rows, :], bb_ref[rows, :]
            return q * jnp.exp(b_f), q * jnp.exp(b_b), k, v, b_f, b_b

        def inter_chunk(qf, qb):
            if not chained:
                return jnp.zeros((CH, D_C), F32)
            s_in = jnp.concatenate([uG_ref[c, 0], uG_ref[c, 1]], axis=0)
            return _dot(_bf(jnp.concatenate([qf, qb], axis=1)), _bf(s_in))

        qf, qb, k, v, b_f, b_b = operands()
        if rotary:
            rotary_body(c, 0)
        if factorised:
            lower, upper = _causal_masks()
            kf16, kb16 = _bf(k * jnp.exp(-b_f)), _bf(k * jnp.exp(-b_b))
            ps = []
            for hh in range(H_C):
                s_f = _dot_nt(_bf(qf * hm_ck[hh]), kf16)
                s_b = _dot_nt(_bf(qb * hm_ck[hh]), kb16)
                ps.append(_bf(jnp.where(lower, s_f, 0.0) + jnp.where(upper, s_b, 0.0)))
            vst = jnp.concatenate([_bf(v * hm_cv[hh]) for hh in range(H_C)], axis=0)
            yc = _dot(jnp.concatenate(ps, axis=1), vst)
            y_ref[rows, D_A + D_B:D_MODEL] = yc + inter_chunk(qf, qb) if chained else yc
            return carry

        y_ref[rows, D_A + D_B:D_MODEL] = inter_chunk(qf, qb)
        key = lax.broadcasted_iota(jnp.int32, (CH, 1), 0)
        sub = 8

        def query_body(g, carry2):
            grp = pl.ds(pl.multiple_of(c * CH + g * sub, sub), sub)
            bq_f, bq_b, qg = bf_ref[grp, :], bb_ref[grp, :], y_ref[grp, CQ:CQ + KC]
            outs = []
            for r in range(sub):
                i = g * sub + r
                w_f = jnp.exp(jnp.where(key <= i, bq_f[r:r + 1, :] - b_f, NEG))
                w_b = jnp.exp(jnp.where(key >= i, bq_b[r:r + 1, :] - b_b, NEG))
                t = _dot(_bf(qg[r:r + 1, :] * k * (w_f + w_b)), _bf(bd_c))
                outs.append(jnp.sum(t * v, axis=0, keepdims=True))
            y_ref[grp, D_A + D_B:D_MODEL] += jnp.concatenate(outs, axis=0)
            return carry2

        lax.fori_loop(0, CH // sub, query_body, 0)
        return carry

    all_safe = jnp.min(tot_ref[...][:, :, LANES:]) > GLA_SAFE_LOG

    @pl.when(all_safe)
    def _():
        chunk_loop(gla_state_ret_proj, functools.partial(gla_out_body, factorised=True))

    @pl.when(jnp.logical_not(all_safe))
    def _():
        chunk_loop(gla_state_ret_proj, functools.partial(gla_out_body, factorised=False), unroll=1)

    if chained:
        lax.fori_loop(0, NCH, ret_state_body, 0, unroll=CHUNK_UNROLL)
        ret_scan()
    chunk_loop(ret_state_body, ret_out_body)

    def gate_proj(c):
        rows = slice(c * CH, (c + 1) * CH)
        pj_ref[rows, 0:W_Z] = _dot(h_ref[rows, :], win_ref[:, OFF_Z:OFF_Z + W_Z])

    def head_norm_gate(c):
        rows = slice(c * CH, (c + 1) * CH)
        parts = []
        for hh in range(H_A):
            ya = y_ref[rows, hh * DH_A:(hh + 1) * DH_A]
            parts.append(ya * lax.rsqrt(jnp.mean(ya * ya, axis=1, keepdims=True) + EPS))
        ybc = y_ref[rows, D_A:D_MODEL]
        seg = _bf(_block_diag_mask(D_B + D_C, D_B + D_C, DH_B, DH_B) * (1.0 / DH_B))
        parts.append(ybc * lax.rsqrt(_exact_dot_r(ybc * ybc, seg) + EPS))
        z = pj_ref[rows, 0:W_Z]
        yg = jnp.concatenate(parts, axis=1) * hng_ref[...] * (z * _sigmoid(z))
        h_ref[rows, :] = _bf(yg)

    def out_proj_residual(c):
        rows = slice(c * CH, (c + 1) * CH)
        xo = x_rows(x_ref, c)[...] + mod_ref[:, 2 * D_MODEL:] * _dot(h_ref[rows, :], wout_ref[...])
        if final:
            xo = xo * lax.rsqrt(jnp.mean(xo * xo, axis=-1, keepdims=True) + EPS) * fg_ref[...]
        x_rows(out_ref, c)[...] = xo

    gate_proj(0)
    for c in range(NCH):
        if c + 1 < NCH:
            gate_proj(c + 1)
        head_norm_gate(c)
        if c > 0:
            out_proj_residual(c - 1)
    out_proj_residual(NCH - 1)


def _layer_call(l, x, mods, mod_row, norm_g, w_in_r, gate_b, lg_rows, w2_full, b2_full, hn_g, w_out_b, final_g,
                states, rot=None, chained=False, final=False):
    batch, seqlen, d = x.shape
    seqs = TOK // seqlen
    steps = batch // seqs
    assert seqs * seqlen == TOK and steps * seqs == batch and seqlen % CH == 0
    assert chained == (seqs == 1)
    assert chained or seqlen == CH

    def const(shape):
        nd = len(shape)
        return pl.BlockSpec(shape, lambda i, _nd=nd: (0,) * _nd)

    def layer_block(shape):
        nd = len(shape)
        return pl.BlockSpec((None,) + shape, lambda i, _nd=nd: (l,) + (0,) * _nd)

    in_specs = [
        pl.BlockSpec((seqs, seqlen, d), lambda i: (i, 0, 0)),
        pl.BlockSpec((None, None, 1, 3 * d),
                     (lambda i: (l, i, 0, 0)) if mod_row is None else (lambda i: (l, mod_row, 0, 0))),
        layer_block((1, d)),
        pl.BlockSpec((None, d, D_INR), lambda i: (l, 0, 0), pipeline_mode=pl.Buffered(1)),
        layer_block((1, LANES)),
        layer_block((2, D_B)),
        layer_block((LANES, 2 * LANES)),
        layer_block((1, 2 * LANES)),
        layer_block((1, d)),
        pl.BlockSpec((None, d, d), lambda i: (l, 0, 0), pipeline_mode=pl.Buffered(1)),
    ]
    args = [x, mods, norm_g.reshape(-1, 1, d), w_in_r, gate_b, lg_rows, w2_full, b2_full,
            hn_g.reshape(-1, 1, d), w_out_b]
    for cst in _gate_constants():
        in_specs.append(pl.BlockSpec(cst.shape, lambda i, _nd=cst.ndim: (0,) * _nd, pipeline_mode=pl.Buffered(1)))
        args.append(cst)
    if final:
        in_specs.append(const((1, d)))
        args.append(final_g.reshape(1, d))
    if rot is not None:
        in_specs += [pl.BlockSpec((seqlen, D_B), lambda i: (0, 0), pipeline_mode=pl.Buffered(1))] * 2
        args += list(rot)
    state_blocks = [(2, H_A, DH_A, DH_A), (2, H_A, DH_A), (1, LANES), (2, H_B, DH_B, DH_B), (2, H_C, DK_C, DV_C)]
    out_shape = [jax.ShapeDtypeStruct(x.shape, x.dtype)]
    out_specs = [pl.BlockSpec((seqs, seqlen, d), lambda i: (i, 0, 0))]
    aliases = {}
    for blk, arr in zip(state_blocks, states):
        zeros = (0,) * len(blk)
        if chained:
            in_specs.append(pl.BlockSpec((None, None) + blk, lambda i, _z=zeros: (i, l) + _z))
        else:
            in_specs.append(pl.BlockSpec(memory_space=pl.ANY))
            aliases[len(args)] = len(out_shape)
            out_shape.append(jax.ShapeDtypeStruct(arr.shape, arr.dtype))
            out_specs.append(pl.BlockSpec((seqs, None) + blk, lambda i, _z=zeros: (i, l) + _z))
        args.append(arr)

    scratch = [
        pltpu.VMEM((TOK, d), BF16),
        pltpu.VMEM((TOK, W_A), F32),
        pltpu.VMEM((TOK, LANES), F32),
        pltpu.VMEM((TOK, LANES), F32),
        pltpu.VMEM((TOK, LANES), F32),
        pltpu.VMEM((TOK, LANES), F32),
        pltpu.VMEM((NCH, 1, 3 * LANES), F32),
        pltpu.VMEM((TOK, d), F32),
        pltpu.VMEM((NCH, 1, LANES), F32),
        pltpu.VMEM((NCH, 1, LANES), F32),
        pltpu.VMEM((TOK, LANES), F32),
        pltpu.VMEM((TOK, LANES), BF16),
    ]
    if chained:
        scratch += [
            pltpu.VMEM((NCH, 2, H_A, DH_A, 2 * DH_A), F32),
            pltpu.VMEM((NCH, 2, D_B, D_B), F32),
            pltpu.VMEM((NCH, 2, H_C * DK_C, D_C), F32),
        ]
    outs = pl.pallas_call(
        functools.partial(_layer_kernel, chained=chained, rotary=rot is not None, final=final),
        out_shape=out_shape,
        grid=(steps,),
        in_specs=in_specs,
        out_specs=out_specs,
        scratch_shapes=scratch,
        input_output_aliases=aliases,
        compiler_params=pltpu.CompilerParams(dimension_semantics=("arbitrary",), vmem_limit_bytes=VMEM_LIMIT),
        name=("latent" if chained else "context") + f"_layer{l}",
    )(*args)
    return outs


def _rotary_tables(seqlen):
    rows = seqlen // GRID_W
    r = jnp.repeat(jnp.arange(rows, dtype=F32), GRID_W)
    col = jnp.tile(jnp.arange(GRID_W, dtype=F32), rows)
    n_f = DH_B // 4
    freqs = ROPE_BASE ** (-jnp.arange(n_f, dtype=F32) / n_f)
    ang = jnp.concatenate([r[:, None] * freqs, col[:, None] * freqs], axis=-1)
    cos, sin = jnp.cos(ang), jnp.sin(ang)
    cos_l = jnp.tile(jnp.concatenate([cos, cos], axis=-1), (1, H_B))
    sin_l = jnp.tile(jnp.concatenate([-sin, sin], axis=-1), (1, H_B))
    return cos_l, sin_l


def _gate_lanes(m):
    z = jnp.zeros(m.shape[:-2] + (LANES,), m.dtype)
    z = z.at[..., COL_FF:COL_FF + H_A].set(m[..., 0, :]).at[..., COL_FB:COL_FB + H_A].set(m[..., 1, :])
    return z[..., None, :]


def _pack_segments():
    src = dict(zip("aq ak av ao az ag bq bk bv bz cq ck cv cz clr".split(),
                   zip(np.cumsum((0,) + IN_WIDTHS[:-1]).tolist(), IN_WIDTHS)))
    segs = []
    for names, dst in (("aq ak av ao", OFF_A), ("ag clr", OFF_S), ("bq bk bv", OFF_B), ("cq ck cv", OFF_C),
                       ("az bz cz", OFF_Z)):
        for name in names.split():
            s, w = src[name]
            segs.append((s, dst, w))
            dst += w
    return segs


def _pack_kernel(wt_ref, o_ref):
    tr = o_ref.shape[0]
    small = []
    for s, dst, w in _pack_segments():
        if w % LANES == 0:
            o_ref[:, dst:dst + w] = _bf(wt_ref[s:s + w, :].T)
        else:
            small.append(wt_ref[s:s + w, :])
    small.append(jnp.zeros((W_S - sum(t.shape[0] for t in small), tr), F32))
    o_ref[:, OFF_S:OFF_S + W_S] = _bf(jnp.concatenate(small, axis=0).T)


def _pack_w_in(w_in):
    depth, d, d_in = w_in.shape
    tr = 256
    return pl.pallas_call(
        _pack_kernel,
        out_shape=jax.ShapeDtypeStruct((depth, d, D_INR), BF16),
        grid=(depth, d // tr),
        in_specs=[pl.BlockSpec((None, d_in, tr), lambda l, r: (l, 0, r))],
        out_specs=pl.BlockSpec((None, tr, D_INR), lambda l, r: (l, r, 0)),
        name="pack_w_in",
    )(jnp.swapaxes(w_in, 1, 2))


def kernel(x_prompt, x_sample, state_mlstm_C, state_mlstm_n, state_mlstm_m, state_ret, state_gla, c, c_ctx,
           norm_g, w_ada, b_ada, w_in, mlstm_gate_b, ret_decay_logit, gla_w2, gla_b2, headnorm_g, w_out, final_g):
    depth = w_in.shape[0]
    dec_batch = c.shape[0]

    w_in_r = _pack_w_in(w_in)
    w_out_b = w_out.astype(BF16)
    n_g = mlstm_gate_b.shape[-1]
    gate_b = jnp.pad(mlstm_gate_b, ((0, 0), (0, LANES - n_g)))[:, None, :]
    lg_rows = jnp.repeat(ret_decay_logit, DH_B, axis=-1)
    w2_full = jnp.zeros((depth, LANES, 2 * LANES), F32)
    w2_full = w2_full.at[:, n_g:n_g + GLA_RANK, 0:LANES].set(gla_w2[:, 0])
    w2_full = w2_full.at[:, n_g + GLA_RANK:n_g + 2 * GLA_RANK, LANES:].set(gla_w2[:, 1]).astype(BF16)
    b2_full = gla_b2.reshape(depth, 1, 2 * LANES)

    cstack = jnp.zeros((16, D_MODEL), F32).at[0:dec_batch].set(c).at[dec_batch].set(c_ctx)
    mods = _modulation(cstack, w_ada, b_ada)[:, :, None, :]

    rot = _rotary_tables(x_sample.shape[1])
    cache = (state_mlstm_C, state_mlstm_n, _gate_lanes(state_mlstm_m), state_ret, state_gla)
    new = tuple(jnp.zeros((x_prompt.shape[0],) + s.shape[1:], F32) for s in cache)

    common = (norm_g, w_in_r, gate_b, lg_rows, w2_full, b2_full, headnorm_g, w_out_b, final_g)
    xp, xs = x_prompt, x_sample
    for l in range(depth):
        xp, *new = _layer_call(l, xp, mods, dec_batch, *common, states=new, final=l == depth - 1)
    for l in range(depth):
        xs = _layer_call(l, xs, mods, None, *common, states=cache, rot=rot, chained=True, final=l == depth - 1)[0]

    new_c, new_n, m_l, new_r, new_g = new
    new_m = jnp.stack([m_l[:, :, 0, COL_FF:COL_FF + H_A], m_l[:, :, 0, COL_FB:COL_FB + H_A]], axis=2)
    return (xp, xs, new_c, new_n, new_m, new_r, new_g)
```

```python
import functools

import jax
import jax.numpy as jnp
import numpy as np
from jax import lax
from jax.experimental import pallas as pl
from jax.experimental.pallas import tpu as pltpu

F32 = jnp.float32
BF16 = jnp.bfloat16

D_MODEL = 1024
H_A, DH_A = 4, 128
H_B, DH_B = 4, 64
H_C, DK_C, DV_C = 4, 32, 64
D_A, D_B, D_C = H_A * DH_A, H_B * DH_B, H_C * DV_C
GLA_RANK = 16
GLA_TAU = 16.0
GRID_W = 64
ROPE_BASE = 10000.0
EPS = 1e-6
IN_WIDTHS = (D_A, D_A, D_A, D_A, D_A, 4 * H_A, D_B, D_B, D_B, D_B, H_C * DK_C, H_C * DK_C, D_C, D_C, 2 * GLA_RANK)

LANES = 128
CH = 256
TOK = 1024
NCH = TOK // CH
CHUNK_UNROLL = 4
VMEM_LIMIT = 62 * 1024 * 1024
NEG = -1e30
GLA_SAFE_LOG = -80.0

OFF_A, W_A = 0, 4 * D_A
OFF_S, W_S = OFF_A + W_A, LANES
OFF_B, W_B = OFF_S + W_S, 3 * D_B
OFF_C, W_C = OFF_B + W_B, 2 * H_C * DK_C + D_C
OFF_Z, W_Z = OFF_C + W_C, D_MODEL
D_INR = OFF_Z + W_Z
COL_FF, COL_FB = 4, 12


def _dot(a, b):
    return jnp.dot(a, b, preferred_element_type=F32)


def _dot_nt(a, b):
    return lax.dot_general(a, b, (((1,), (1,)), ((), ())), preferred_element_type=F32)


def _bf(x):
    return x.astype(BF16)


N_PIECES = 2


def _split(x):
    pieces, rest = [], x
    for _ in range(N_PIECES):
        pieces.append(_bf(rest))
        rest = rest - pieces[-1].astype(F32)
    return pieces


def _wide_dot_l(m_bf, x):
    return functools.reduce(jnp.add, [_dot(m_bf, p) for p in reversed(_split(x))])


def _wide_dot_r(x, m_bf):
    return functools.reduce(jnp.add, [_dot(p, m_bf) for p in reversed(_split(x))])


def _log_sigmoid(x):
    return jnp.minimum(x, 0.0) - jnp.log(1.0 + jnp.exp(-jnp.abs(x)))


def _sigmoid(x):
    return 1.0 / (1.0 + jnp.exp(-x))


def _row_to_col(row, n):
    eye = lax.broadcasted_iota(jnp.int32, (n, n), 0) == lax.broadcasted_iota(jnp.int32, (n, n), 1)
    return jnp.sum(jnp.where(eye, row, 0.0), axis=1, keepdims=True)


def _lane_block_mask(width, block, h):
    lane = lax.broadcasted_iota(jnp.int32, (1, width), 1)
    return ((lane >= h * block) & (lane < (h + 1) * block)).astype(F32)


def _log2(n):
    assert n & (n - 1) == 0
    return n.bit_length() - 1


def _block_diag_mask(rows, cols, rblock, cblock):
    r = lax.broadcasted_iota(jnp.int32, (rows, cols), 0) >> _log2(rblock)
    c = lax.broadcasted_iota(jnp.int32, (rows, cols), 1) >> _log2(cblock)
    return (r == c).astype(F32)


def _causal_masks():
    ri = lax.broadcasted_iota(jnp.int32, (CH, CH), 0)
    cj = lax.broadcasted_iota(jnp.int32, (CH, CH), 1)
    return cj <= ri, cj >= ri


def _bwd_gate_lanes():
    lane = lax.broadcasted_iota(jnp.int32, (1, LANES), 1)
    return (lane >= COL_FB) & (lane < COL_FB + H_A)


def _block_diag_value(ref, hn, a, b):
    rows = []
    for h in range(hn):
        wide = jnp.concatenate([ref[h], jnp.zeros((a, (hn - 1) * b), F32)], axis=1)
        rows.append(wide if h == 0 else pltpu.roll(wide, h * b, 1))
    return jnp.concatenate(rows, axis=0)


def _pieces(x):
    return jnp.concatenate(_split(x), axis=1)


def _cum_max(x, reverse):
    n = x.shape[0]
    row = lax.broadcasted_iota(jnp.int32, x.shape, 0)
    s = 1
    while s < n:
        if reverse:
            shifted = jnp.where(row < n - s, pltpu.roll(x, n - s, 0), NEG)
        else:
            shifted = jnp.where(row >= s, pltpu.roll(x, s, 0), NEG)
        x = jnp.maximum(x, shifted)
        s *= 2
    return x


GATE_BLOCK = 16


def _gate_constants():
    half = N_PIECES * GATE_BLOCK
    cols = [COL_FF + h for h in range(H_A)] + [COL_FB + h for h in range(H_A)]
    sp = np.zeros((2 * N_PIECES * LANES, LANES), np.float32)
    for part in range(2):
        for p in range(N_PIECES):
            for k in cols:
                sp[(part * N_PIECES + p) * LANES + k, part * half + p * GATE_BLOCK + k] = 1.0
    lane = np.arange(LANES)
    gl = np.zeros((16, LANES), np.float32)
    for hd, k in enumerate(cols):
        gl[hd] = (lane % GATE_BLOCK == k) & (lane < 2 * half)
    valid = np.isin(lane % GATE_BLOCK, cols)
    gl[2 * H_A] = valid & (lane < half)
    gl[2 * H_A + 1] = valid & (lane >= half) & (lane < 2 * half)
    return jnp.asarray(sp, BF16), jnp.asarray(gl, F32)


def _modulation_kernel(c_ref, w_ref, b_ref, o_ref):
    cv = c_ref[...]
    s = cv * _sigmoid(cv)
    o_ref[...] = _dot(_bf(s), _bf(w_ref[...])) + b_ref[...]


def _modulation(cstack, w_ada, b_ada):
    depth, d, d3 = w_ada.shape
    rows = cstack.shape[0]
    tn = 1024
    return pl.pallas_call(
        _modulation_kernel,
        out_shape=jax.ShapeDtypeStruct((depth, rows, d3), F32),
        grid=(depth, d3 // tn),
        in_specs=[
            pl.BlockSpec((rows, d), lambda l, j: (0, 0)),
            pl.BlockSpec((None, d, tn), lambda l, j: (l, 0, j)),
            pl.BlockSpec((None, 1, tn), lambda l, j: (l, 0, j)),
        ],
        out_specs=pl.BlockSpec((None, rows, tn), lambda l, j: (l, 0, j)),
        name="adaln_modulation",
    )(cstack, w_ada, b_ada.reshape(depth, 1, d3))


def _layer_kernel(*refs, chained, rotary, final):
    it = iter(refs)
    x_ref, mod_ref, ng_ref, win_ref, gb_ref, lg_ref, w2_ref, b2_ref, hng_ref, wout_ref = (next(it) for _ in range(10))
    sp_ref, gl_ref = next(it), next(it)
    fg_ref = next(it) if final else None
    if rotary:
        cos_ref, sin_ref = next(it), next(it)
    sC_ref, sn_ref, sm_ref_in, sR_ref, sG_ref = (next(it) for _ in range(5))
    out_ref = next(it)
    if not chained:
        oC_ref, on_ref, om_ref, oR_ref, oG_ref = (next(it) for _ in range(5))
    (h_ref, pj_ref, sm_ref, fc_ref, bf_ref, bb_ref, tot_ref, y_ref,
     gm_ref, min_ref, cm_ref, bop_ref) = (next(it) for _ in range(12))
    dec_ref = pj_ref.at[:, W_B:W_B + CH]
    if chained:
        uCN_ref, uR_ref, uG_ref = (next(it) for _ in range(3))

    seqs, seqlen, _ = x_ref.shape

    def x_rows(ref, c):
        if isinstance(c, int):
            r0 = c * CH
            return ref.at[r0 // seqlen, r0 % seqlen:r0 % seqlen + CH, :]
        r0 = pl.multiple_of(c * CH, CH)
        return ref.at[r0 >> _log2(seqlen), pl.ds(pl.multiple_of(r0 & (seqlen - 1), CH), CH), :]

    def norm_rows(c):
        x = x_rows(x_ref, c)[...]
        shift, scale = mod_ref[:, 0:D_MODEL], mod_ref[:, D_MODEL:2 * D_MODEL]
        xn = x * lax.rsqrt(jnp.mean(x * x, axis=-1, keepdims=True) + EPS) * ng_ref[...]
        h_ref[c * CH:(c + 1) * CH, :] = _bf(xn * (1.0 + scale) + shift)

    def front_rows(c):
        rows = slice(c * CH, (c + 1) * CH)
        proj = _dot(h_ref[rows, :], win_ref[:, OFF_A:OFF_S + W_S])
        pj_ref[rows, 0:W_A] = proj[:, 0:W_A]
        sm = proj[:, W_A:]
        sm_ref[rows, :] = sm
        la = _log_sigmoid(_dot(_bf(sm), w2_ref[...]) + b2_ref[...]) * (1.0 / GLA_TAU)
        tri = _bf(_causal_masks()[0].astype(F32))
        ls = _log_sigmoid(sm + gb_ref[...])
        ps = _wide_dot_l(tri, jnp.concatenate([ls, la], axis=1))
        tot = ps[CH - 1:CH, :]
        fc_ref[rows, :] = jnp.where(_bwd_gate_lanes(), tot[:, 0:LANES] - ps[:, 0:LANES] + ls, ps[:, 0:LANES])
        bf_ref[rows, :] = ps[:, LANES:2 * LANES]
        bb_ref[rows, :] = tot[:, 2 * LANES:] - ps[:, 2 * LANES:] + la[:, LANES:]
        tot_ref[c] = tot

    norm_rows(0)
    for c in range(NCH):
        if c + 1 < NCH:
            norm_rows(c + 1)
        front_rows(c)

    AQ, AK, AV, AO = 0, D_A, 2 * D_A, 3 * D_A
    k_scale = DH_A ** -0.5

    ones_cols = jnp.ones((CH, DH_A), F32)

    def mlstm_state_body(c, carry):
        rows = pl.ds(pl.multiple_of(c * CH, CH), CH)
        g_pre = sm_ref[rows, :] + gb_ref[...]
        rw = pltpu.roll(g_pre, 4, 1) - fc_ref[rows, :]
        cm_ref[rows, :] = jnp.where(_bwd_gate_lanes(), _cum_max(rw, True), _cum_max(rw, False))
        bop_ref[rows, :] = _bf(_dot(_pieces(rw), sp_ref[N_PIECES * LANES:, :]) + gl_ref[2 * H_A:2 * H_A + 1, :])
        totg = tot_ref[c][:, 0:LANES]
        g = totg + rw
        gm = jnp.max(g, axis=0, keepdims=True)
        gm_ref[c] = gm
        wgt = jnp.exp(g - gm).T
        y_ref[rows, D_A:D_MODEL] = _dot(h_ref[rows, :], win_ref[:, OFF_C:OFF_C + W_C])
        if not chained:
            m_new = jnp.maximum(totg, gm)
            b_row = jnp.exp(gm - m_new)
            om_ref[c] = m_new
            min_ref[c] = jnp.zeros((1, LANES), F32)
        for hh in range(H_A):
            kt = (pj_ref[rows, AK + hh * DH_A:AK + (hh + 1) * DH_A] * k_scale).T
            v_h = pj_ref[rows, AV + hh * DH_A:AV + (hh + 1) * DH_A]
            lhs = jnp.concatenate([kt * wgt[COL_FF + hh:COL_FF + hh + 1, :],
                                   kt * wgt[COL_FB + hh:COL_FB + hh + 1, :]], axis=0)
            u = _dot(_bf(lhs), _bf(jnp.concatenate([v_h, ones_cols], axis=1)))
            for d, col in ((0, COL_FF), (1, COL_FB)):
                u_d = u[d * DH_A:(d + 1) * DH_A, :]
                if chained:
                    uCN_ref[c, d, hh] = u_d
                else:
                    fin = b_row[:, col + hh:col + hh + 1] * u_d
                    oC_ref[c, d, hh] = fin[:, 0:DH_A]
                    on_ref[c, d, hh:hh + 1, :] = fin[:, DH_A:].T[0:1, :]
        return carry

    def chunk_loop(state_body, out_body, unroll=CHUNK_UNROLL):
        if chained:
            return lax.fori_loop(0, NCH, out_body, 0, unroll=unroll)
        return lax.fori_loop(0, NCH, lambda c, carry: out_body(c, state_body(c, carry)), 0, unroll=unroll)

    if chained:
        lax.fori_loop(0, NCH, mlstm_state_body, 0, unroll=CHUNK_UNROLL)

    if chained:
        for d, order, col in ((0, range(NCH), COL_FF), (1, range(NCH - 1, -1, -1), COL_FB)):
            m_run = sm_ref_in[...]
            cn_run = [jnp.concatenate([sC_ref[d, hh],
                                       jnp.broadcast_to(sn_ref[d, hh:hh + 1, :], (DH_A, DH_A)).T], axis=1)
                      for hh in range(H_A)]
            for idx, c in enumerate(order):
                last = idx == NCH - 1
                if not last:
                    totg, gm = tot_ref[c][:, 0:LANES], gm_ref[c]
                    m_new = jnp.maximum(totg + m_run, gm)
                    a_row, b_row = jnp.exp(totg + m_run - m_new), jnp.exp(gm - m_new)
                for hh in range(H_A):
                    u = uCN_ref[c, d, hh]
                    uCN_ref[c, d, hh] = cn_run[hh]
                    if not last:
                        cn_run[hh] = (a_row[:, col + hh:col + hh + 1] * cn_run[hh]
                                      + b_row[:, col + hh:col + hh + 1] * u)
                min_ref[c] = m_run if d == 0 else jnp.where(_bwd_gate_lanes(), m_run, min_ref[c])
                if not last:
                    m_run = m_new

    def mlstm_out_body(c, carry):
        rows = pl.ds(pl.multiple_of(c * CH, CH), CH)
        lower, upper = _causal_masks()
        m_prev = min_ref[c]
        cmx = jnp.maximum(cm_ref[rows, :], m_prev)
        mi = fc_ref[rows, :] + cmx
        a_op = _bf(_dot(_pieces(-cmx), sp_ref[0:N_PIECES * LANES, :]) + gl_ref[2 * H_A + 1:2 * H_A + 2, :])
        b_op = bop_ref[rows, :]
        floor_all = jnp.exp(-mi)
        wa_all = jnp.exp(m_prev - cmx)
        for hh in range(H_A):
            q16 = _bf(pj_ref[rows, AQ + hh * DH_A:AQ + (hh + 1) * DH_A])
            k_h = pj_ref[rows, AK + hh * DH_A:AK + (hh + 1) * DH_A] * k_scale
            v_h = pj_ref[rows, AV + hh * DH_A:AV + (hh + 1) * DH_A]
            o_h = pj_ref[rows, AO + hh * DH_A:AO + (hh + 1) * DH_A]
            s = _dot_nt(q16, _bf(k_h))
            qs = []
            for d, msk in ((0, lower), (1, upper)):
                hd = d * H_A + hh
                dm = _dot_nt(a_op * _bf(gl_ref[hd:hd + 1, :]), b_op)
                qs.append(_bf(s * jnp.exp(jnp.where(msk, dm, NEG))))
            nd = _dot(jnp.concatenate(qs, axis=0), _bf(jnp.concatenate([v_h, ones_cols], axis=1)))
            hs = None
            for d, col in ((0, COL_FF + hh), (1, COL_FB + hh)):
                num, den = nd[d * CH:(d + 1) * CH, 0:DH_A], nd[d * CH:(d + 1) * CH, DH_A:]
                if chained:
                    wa = jnp.broadcast_to(wa_all[:, col:col + 1], (CH, DH_A))
                    inter = _dot(q16, _bf(uCN_ref[c, d, hh]))
                    num = num + wa * inter[:, 0:DH_A]
                    den = den + wa * inter[:, DH_A:]
                floor = jnp.broadcast_to(floor_all[:, col:col + 1], (CH, DH_A))
                part = num / jnp.maximum(jnp.abs(den), floor)
                hs = part if hs is None else hs + part
            y_ref[rows, hh * DH_A:(hh + 1) * DH_A] = _sigmoid(o_h) * hs
        return carry

    chunk_loop(mlstm_state_body, mlstm_out_body)

    BQ, BK, BV = 0, D_B, 2 * D_B

    def ret_proj_body(c, carry):
        rows = pl.ds(pl.multiple_of(c * CH, CH), CH)
        pj_ref[rows, 0:W_B] = _dot(h_ref[rows, :], win_ref[:, OFF_B:OFF_B + W_B])
        return carry

    def rotary_body(c, carry):
        rows = pl.ds(pl.multiple_of(c * CH, CH), CH)
        first_half = (lax.broadcasted_iota(jnp.int32, (1, D_B), 1) & (DH_B - 1)) < DH_B // 2
        for off in (BQ, BK):
            t = pj_ref[rows, off:off + D_B]
            partner = jnp.where(first_half, pltpu.roll(t, D_B - DH_B // 2, 1), pltpu.roll(t, DH_B // 2, 1))
            pj_ref[rows, off:off + D_B] = t * cos_ref[rows, :] + partner * sin_ref[rows, :]
        return carry

    lgam = _log_sigmoid(lg_ref[...])
    lg_f, lg_b = lgam[0:1, :], lgam[1:2, :]
    pos = lax.broadcasted_iota(jnp.int32, (CH, 1), 0).astype(F32)
    lower, upper = _causal_masks()
    rel = (lax.broadcasted_iota(jnp.int32, (CH, CH), 0) - lax.broadcasted_iota(jnp.int32, (CH, CH), 1)).astype(F32)
    for hh in range(H_B):
        lf = lg_f[:, hh * DH_B:hh * DH_B + 1]
        lb = lg_b[:, hh * DH_B:hh * DH_B + 1]
        dec_ref[hh * CH:(hh + 1) * CH, :] = (jnp.where(lower, jnp.exp(jnp.maximum(rel, 0.0) * lf), 0.0)
                                             + jnp.where(upper, jnp.exp(jnp.maximum(-rel, 0.0) * lb), 0.0))
    bd_b = _block_diag_mask(D_B, D_B, DH_B, DH_B)
    hm_b = [_lane_block_mask(D_B, DH_B, hh) for hh in range(H_B)]
    ret_scale = DH_B ** -0.5

    def ret_state_body(c, carry):
        rows = pl.ds(pl.multiple_of(c * CH, CH), CH)
        k = pj_ref[rows, BK:BK + D_B] * ret_scale
        v = pj_ref[rows, BV:BV + D_B]
        kf = k * jnp.exp((CH - 1.0 - pos) * lg_f)
        kb = k * jnp.exp(pos * lg_b)
        u = _dot(_bf(jnp.concatenate([kf, kb], axis=1).T), _bf(v))
        for d in range(2):
            u_d = u[d * D_B:(d + 1) * D_B, :]
            if chained:
                uR_ref[c, d] = u_d * bd_b
            else:
                for hh in range(H_B):
                    oR_ref[c, d, hh] = u_d[hh * DH_B:(hh + 1) * DH_B, hh * DH_B:(hh + 1) * DH_B]
        return carry

    def ret_scan():
        for d, order, lg_row in ((0, range(NCH), lg_f), (1, range(NCH - 1, -1, -1), lg_b)):
            g_col = _row_to_col(jnp.exp(float(CH) * lg_row), D_B)
            s_run = _block_diag_value(sR_ref.at[d], H_B, DH_B, DH_B)
            for idx, c in enumerate(order):
                u = uR_ref[c, d]
                uR_ref[c, d] = s_run
                if idx != NCH - 1:
                    s_run = g_col * s_run + u

    def ret_out_body(c, carry):
        rows = pl.ds(pl.multiple_of(c * CH, CH), CH)
        q = pj_ref[rows, BQ:BQ + D_B]
        k = pj_ref[rows, BK:BK + D_B] * ret_scale
        v = pj_ref[rows, BV:BV + D_B]
        kb16 = _bf(k)
        ps = [_bf(_dot_nt(_bf(q * hm_b[hh]), kb16) * dec_ref[hh * CH:(hh + 1) * CH, :]) for hh in range(H_B)]
        vst = jnp.concatenate([_bf(v * hm_b[hh]) for hh in range(H_B)], axis=0)
        yb = _dot(jnp.concatenate(ps, axis=1), vst)
        if chained:
            qf = q * jnp.exp((pos + 1.0) * lg_f)
            qb = q * jnp.exp((float(CH) - pos) * lg_b)
            s_in = jnp.concatenate([uR_ref[c, 0], uR_ref[c, 1]], axis=0)
            yb = yb + _dot(_bf(jnp.concatenate([qf, qb], axis=1)), _bf(s_in))
        y_ref[rows, D_A:D_A + D_B] = yb
        return carry

    KC = H_C * DK_C
    CQ, CK, CV = D_A, D_A + KC, D_A + 2 * KC
    gla_scale = DK_C ** -0.5
    bd_c = _block_diag_mask(KC, D_C, DK_C, DV_C)
    hm_ck = [_lane_block_mask(KC, DK_C, hh) for hh in range(H_C)]
    hm_cv = [_lane_block_mask(D_C, DV_C, hh) for hh in range(H_C)]

    def gla_state_body(c, carry):
        rows = pl.ds(pl.multiple_of(c * CH, CH), CH)
        k = y_ref[rows, CK:CK + KC] * gla_scale
        v = y_ref[rows, CV:CV + D_C]
        tot = tot_ref[c]
        khf = k * jnp.exp(tot[:, LANES:2 * LANES] - bf_ref[rows, :])
        khb = k * jnp.exp(tot[:, 2 * LANES:] - bb_ref[rows, :])
        u = _dot(_bf(jnp.concatenate([khf, khb], axis=1).T), _bf(v))
        for d in range(2):
            u_d = u[d * KC:(d + 1) * KC, :]
            if chained:
                uG_ref[c, d] = u_d * bd_c
            else:
                for hh in range(H_C):
                    oG_ref[c, d, hh] = u_d[hh * DK_C:(hh + 1) * DK_C, hh * DV_C:(hh + 1) * DV_C]
        return carry

    def gla_state_ret_proj(c, carry):
        return gla_state_body(c, ret_proj_body(c, carry))

    if chained:
        lax.fori_loop(0, NCH, gla_state_ret_proj, 0, unroll=CHUNK_UNROLL)

    if chained:
        for d, order in ((0, range(NCH)), (1, range(NCH - 1, -1, -1))):
            s_run = _block_diag_value(sG_ref.at[d], H_C, DK_C, DV_C)
            for idx, c in enumerate(order):
                u = uG_ref[c, d]
                uG_ref[c, d] = s_run
                if idx != NCH - 1:
                    tot_row = tot_ref[c][:, (1 + d) * LANES:(2 + d) * LANES]
                    s_run = _row_to_col(jnp.exp(tot_row), KC) * s_run + u

    def gla_out_body(c, carry, factorised):
        rows = pl.ds(pl.multiple_of(c * CH, CH), CH)

        def operands():
            q = y_ref[rows, CQ:CQ + KC]
            k = y_ref[rows, CK:CK + KC] * gla_scale
            v = y_ref[rows, CV:CV + D_C]
            b_f, b_b = bf_ref[rows, :], bb_ref[rows, :]
            return q * jnp.exp(b_f), q * jnp.exp(b_b), k, v, b_f, b_b

        def inter_chunk(qf, qb):
            if not chained:
                return jnp.zeros((CH, D_C), F32)
            s_in = jnp.concatenate([uG_ref[c, 0], uG_ref[c, 1]], axis=0)
            return _dot(_bf(jnp.concatenate([qf, qb], axis=1)), _bf(s_in))

        qf, qb, k, v, b_f, b_b = operands()
        if rotary:
            rotary_body(c, 0)
        if factorised:
            lower, upper = _causal_masks()
            kf16, kb16 = _bf(k * jnp.exp(-b_f)), _bf(k * jnp.exp(-b_b))
            ps = []
            for hh in range(H_C):
                s_f = _dot_nt(_bf(qf * hm_ck[hh]), kf16)
                s_b = _dot_nt(_bf(qb * hm_ck[hh]), kb16)
                ps.append(_bf(jnp.where(lower, s_f, 0.0) + jnp.where(upper, s_b, 0.0)))
            vst = jnp.concatenate([_bf(v * hm_cv[hh]) for hh in range(H_C)], axis=0)
            yc = _dot(jnp.concatenate(ps, axis=1), vst)
            y_ref[rows, D_A + D_B:D_MODEL] = yc + inter_chunk(qf, qb) if chained else yc
            return carry

        y_ref[rows, D_A + D_B:D_MODEL] = inter_chunk(qf, qb)
        key = lax.broadcasted_iota(jnp.int32, (CH, 1), 0)
        sub = 8

        def query_body(g, carry2):
            grp = pl.ds(pl.multiple_of(c * CH + g * sub, sub), sub)
            bq_f, bq_b, qg = bf_ref[grp, :], bb_ref[grp, :], y_ref[grp, CQ:CQ + KC]
            outs = []
            for r in range(sub):
                i = g * sub + r
                w_f = jnp.exp(jnp.where(key <= i, bq_f[r:r + 1, :] - b_f, NEG))
                w_b = jnp.exp(jnp.where(key >= i, bq_b[r:r + 1, :] - b_b, NEG))
                t = _dot(_bf(qg[r:r + 1, :] * k * (w_f + w_b)), _bf(bd_c))
                outs.append(jnp.sum(t * v, axis=0, keepdims=True))
            y_ref[grp, D_A + D_B:D_MODEL] += jnp.concatenate(outs, axis=0)
            return carry2

        lax.fori_loop(0, CH // sub, query_body, 0)
        return carry

    all_safe = jnp.min(tot_ref[...][:, :, LANES:]) > GLA_SAFE_LOG

    @pl.when(all_safe)
    def _():
        chunk_loop(gla_state_ret_proj, functools.partial(gla_out_body, factorised=True))

    @pl.when(jnp.logical_not(all_safe))
    def _():
        chunk_loop(gla_state_ret_proj, functools.partial(gla_out_body, factorised=False), unroll=1)

    if chained:
        lax.fori_loop(0, NCH, ret_state_body, 0, unroll=CHUNK_UNROLL)
        ret_scan()
    chunk_loop(ret_state_body, ret_out_body)

    def gate_proj(c):
        rows = slice(c * CH, (c + 1) * CH)
        pj_ref[rows, 0:W_Z] = _dot(h_ref[rows, :], win_ref[:, OFF_Z:OFF_Z + W_Z])

    def head_norm_gate(c):
        rows = slice(c * CH, (c + 1) * CH)
        parts = []
        for hh in range(H_A):
            ya = y_ref[rows, hh * DH_A:(hh + 1) * DH_A]
            parts.append(ya * lax.rsqrt(jnp.mean(ya * ya, axis=1, keepdims=True) + EPS))
        ybc = y_ref[rows, D_A:D_MODEL]
        seg = _bf(_block_diag_mask(D_B + D_C, D_B + D_C, DH_B, DH_B) * (1.0 / DH_B))
        parts.append(ybc * lax.rsqrt(_wide_dot_r(ybc * ybc, seg) + EPS))
        z = pj_ref[rows, 0:W_Z]
        yg = jnp.concatenate(parts, axis=1) * hng_ref[...] * (z * _sigmoid(z))
        h_ref[rows, :] = _bf(yg)

    def out_proj_residual(c):
        rows = slice(c * CH, (c + 1) * CH)
        xo = x_rows(x_ref, c)[...] + mod_ref[:, 2 * D_MODEL:] * _dot(h_ref[rows, :], wout_ref[...])
        if final:
            xo = xo * lax.rsqrt(jnp.mean(xo * xo, axis=-1, keepdims=True) + EPS) * fg_ref[...]
        x_rows(out_ref, c)[...] = xo

    gate_proj(0)
    for c in range(NCH):
        if c + 1 < NCH:
            gate_proj(c + 1)
        head_norm_gate(c)
        if c > 0:
            out_proj_residual(c - 1)
    out_proj_residual(NCH - 1)


def _layer_call(l, x, mods, mod_row, norm_g, w_in_r, gate_b, lg_rows, w2_full, b2_full, hn_g, w_out_b, final_g,
                states, rot=None, chained=False, final=False):
    batch, seqlen, d = x.shape
    seqs = TOK // seqlen
    steps = batch // seqs
    assert seqs * seqlen == TOK and steps * seqs == batch and seqlen % CH == 0
    assert chained == (seqs == 1)
    assert chained or seqlen == CH

    def const(shape):
        nd = len(shape)
        return pl.BlockSpec(shape, lambda i, _nd=nd: (0,) * _nd)

    def layer_block(shape):
        nd = len(shape)
        return pl.BlockSpec((None,) + shape, lambda i, _nd=nd: (l,) + (0,) * _nd)

    in_specs = [
        pl.BlockSpec((seqs, seqlen, d), lambda i: (i, 0, 0)),
        pl.BlockSpec((None, None, 1, 3 * d),
                     (lambda i: (l, i, 0, 0)) if mod_row is None else (lambda i: (l, mod_row, 0, 0))),
        layer_block((1, d)),
        pl.BlockSpec((None, d, D_INR), lambda i: (l, 0, 0), pipeline_mode=pl.Buffered(1)),
        layer_block((1, LANES)),
        layer_block((2, D_B)),
        layer_block((LANES, 2 * LANES)),
        layer_block((1, 2 * LANES)),
        layer_block((1, d)),
        pl.BlockSpec((None, d, d), lambda i: (l, 0, 0), pipeline_mode=pl.Buffered(1)),
    ]
    args = [x, mods, norm_g.reshape(-1, 1, d), w_in_r, gate_b, lg_rows, w2_full, b2_full,
            hn_g.reshape(-1, 1, d), w_out_b]
    for cst in _gate_constants():
        in_specs.append(pl.BlockSpec(cst.shape, lambda i, _nd=cst.ndim: (0,) * _nd, pipeline_mode=pl.Buffered(1)))
        args.append(cst)
    if final:
        in_specs.append(const((1, d)))
        args.append(final_g.reshape(1, d))
    if rot is not None:
        in_specs += [pl.BlockSpec((seqlen, D_B), lambda i: (0, 0), pipeline_mode=pl.Buffered(1))] * 2
        args += list(rot)
    state_blocks = [(2, H_A, DH_A, DH_A), (2, H_A, DH_A), (1, LANES), (2, H_B, DH_B, DH_B), (2, H_C, DK_C, DV_C)]
    out_shape = [jax.ShapeDtypeStruct(x.shape, x.dtype)]
    out_specs = [pl.BlockSpec((seqs, seqlen, d), lambda i: (i, 0, 0))]
    aliases = {}
    for blk, arr in zip(state_blocks, states):
        zeros = (0,) * len(blk)
        if chained:
            in_specs.append(pl.BlockSpec((None, None) + blk, lambda i, _z=zeros: (i, l) + _z))
        else:
            in_specs.append(pl.BlockSpec(memory_space=pl.ANY))
            aliases[len(args)] = len(out_shape)
            out_shape.append(jax.ShapeDtypeStruct(arr.shape, arr.dtype))
            out_specs.append(pl.BlockSpec((seqs, None) + blk, lambda i, _z=zeros: (i, l) + _z))
        args.append(arr)

    scratch = [
        pltpu.VMEM((TOK, d), BF16),
        pltpu.VMEM((TOK, W_A), F32),
        pltpu.VMEM((TOK, LANES), F32),
        pltpu.VMEM((TOK, LANES), F32),
        pltpu.VMEM((TOK, LANES), F32),
        pltpu.VMEM((TOK, LANES), F32),
        pltpu.VMEM((NCH, 1, 3 * LANES), F32),
        pltpu.VMEM((TOK, d), F32),
        pltpu.VMEM((NCH, 1, LANES), F32),
        pltpu.VMEM((NCH, 1, LANES), F32),
        pltpu.VMEM((TOK, LANES), F32),
        pltpu.VMEM((TOK, LANES), BF16),
    ]
    if chained:
        scratch += [
            pltpu.VMEM((NCH, 2, H_A, DH_A, 2 * DH_A), F32),
            pltpu.VMEM((NCH, 2, D_B, D_B), F32),
            pltpu.VMEM((NCH, 2, H_C * DK_C, D_C), F32),
        ]
    outs = pl.pallas_call(
        functools.partial(_layer_kernel, chained=chained, rotary=rot is not None, final=final),
        out_shape=out_shape,
        grid=(steps,),
        in_specs=in_specs,
        out_specs=out_specs,
        scratch_shapes=scratch,
        input_output_aliases=aliases,
        compiler_params=pltpu.CompilerParams(dimension_semantics=("arbitrary",), vmem_limit_bytes=VMEM_LIMIT),
        name=("latent" if chained else "context") + f"_layer{l}",
    )(*args)
    return outs


def _rotary_tables(seqlen):
    rows = seqlen // GRID_W
    r = jnp.repeat(jnp.arange(rows, dtype=F32), GRID_W)
    col = jnp.tile(jnp.arange(GRID_W, dtype=F32), rows)
    n_f = DH_B // 4
    freqs = ROPE_BASE ** (-jnp.arange(n_f, dtype=F32) / n_f)
    ang = jnp.concatenate([r[:, None] * freqs, col[:, None] * freqs], axis=-1)
    cos, sin = jnp.cos(ang), jnp.sin(ang)
    cos_l = jnp.tile(jnp.concatenate([cos, cos], axis=-1), (1, H_B))
    sin_l = jnp.tile(jnp.concatenate([-sin, sin], axis=-1), (1, H_B))
    return cos_l, sin_l


def _gate_lanes(m):
    z = jnp.zeros(m.shape[:-2] + (LANES,), m.dtype)
    z = z.at[..., COL_FF:COL_FF + H_A].set(m[..., 0, :]).at[..., COL_FB:COL_FB + H_A].set(m[..., 1, :])
    return z[..., None, :]


def _pack_segments():
    src = dict(zip("aq ak av ao az ag bq bk bv bz cq ck cv cz clr".split(),
                   zip(np.cumsum((0,) + IN_WIDTHS[:-1]).tolist(), IN_WIDTHS)))
    segs = []
    for names, dst in (("aq ak av ao", OFF_A), ("ag clr", OFF_S), ("bq bk bv", OFF_B), ("cq ck cv", OFF_C),
                       ("az bz cz", OFF_Z)):
        for name in names.split():
            s, w = src[name]
            segs.append((s, dst, w))
            dst += w
    return segs


def _pack_kernel(wt_ref, o_ref):
    tr = o_ref.shape[0]
    small = []
    for s, dst, w in _pack_segments():
        if w % LANES == 0:
            o_ref[:, dst:dst + w] = _bf(wt_ref[s:s + w, :].T)
        else:
            small.append(wt_ref[s:s + w, :])
    small.append(jnp.zeros((W_S - sum(t.shape[0] for t in small), tr), F32))
    o_ref[:, OFF_S:OFF_S + W_S] = _bf(jnp.concatenate(small, axis=0).T)


def _pack_w_in(w_in):
    depth, d, d_in = w_in.shape
    tr = 256
    return pl.pallas_call(
        _pack_kernel,
        out_shape=jax.ShapeDtypeStruct((depth, d, D_INR), BF16),
        grid=(depth, d // tr),
        in_specs=[pl.BlockSpec((None, d_in, tr), lambda l, r: (l, 0, r))],
        out_specs=pl.BlockSpec((None, tr, D_INR), lambda l, r: (l, r, 0)),
        name="pack_w_in",
    )(jnp.swapaxes(w_in, 1, 2))


def kernel(x_prompt, x_sample, state_mlstm_C, state_mlstm_n, state_mlstm_m, state_ret, state_gla, c, c_ctx,
           norm_g, w_ada, b_ada, w_in, mlstm_gate_b, ret_decay_logit, gla_w2, gla_b2, headnorm_g, w_out, final_g):
    depth = w_in.shape[0]
    dec_batch = c.shape[0]

    w_in_r = _pack_w_in(w_in)
    w_out_b = w_out.astype(BF16)
    n_g = mlstm_gate_b.shape[-1]
    gate_b = jnp.pad(mlstm_gate_b, ((0, 0), (0, LANES - n_g)))[:, None, :]
    lg_rows = jnp.repeat(ret_decay_logit, DH_B, axis=-1)
    w2_full = jnp.zeros((depth, LANES, 2 * LANES), F32)
    w2_full = w2_full.at[:, n_g:n_g + GLA_RANK, 0:LANES].set(gla_w2[:, 0])
    w2_full = w2_full.at[:, n_g + GLA_RANK:n_g + 2 * GLA_RANK, LANES:].set(gla_w2[:, 1]).astype(BF16)
    b2_full = gla_b2.reshape(depth, 1, 2 * LANES)

    cstack = jnp.zeros((16, D_MODEL), F32).at[0:dec_batch].set(c).at[dec_batch].set(c_ctx)
    mods = _modulation(cstack, w_ada, b_ada)[:, :, None, :]

    rot = _rotary_tables(x_sample.shape[1])
    cache = (state_mlstm_C, state_mlstm_n, _gate_lanes(state_mlstm_m), state_ret, state_gla)
    new = tuple(jnp.zeros((x_prompt.shape[0],) + s.shape[1:], F32) for s in cache)

    common = (norm_g, w_in_r, gate_b, lg_rows, w2_full, b2_full, headnorm_g, w_out_b, final_g)
    xp, xs = x_prompt, x_sample
    for l in range(depth):
        xp, *new = _layer_call(l, xp, mods, dec_batch, *common, states=new, final=l == depth - 1)
    for l in range(depth):
        xs = _layer_call(l, xs, mods, None, *common, states=cache, rot=rot, chained=True, final=l == depth - 1)[0]

    new_c, new_n, m_l, new_r, new_g = new
    new_m = jnp.stack([m_l[:, :, 0, COL_FF:COL_FF + H_A], m_l[:, :, 0, COL_FB:COL_FB + H_A]], axis=2)
    return (xp, xs, new_c, new_n, new_m, new_r, new_g)
```

```python
import functools

import jax
import jax.numpy as jnp
import numpy as np
from jax import lax
from jax.experimental import pallas as pl
from jax.experimental.pallas import tpu as pltpu

F32 = jnp.float32
BF16 = jnp.bfloat16

D_MODEL = 1024
H_A, DH_A = 4, 128
H_B, DH_B = 4, 64
H_C, DK_C, DV_C = 4, 32, 64
D_A, D_B, D_C = H_A * DH_A, H_B * DH_B, H_C * DV_C
GLA_RANK = 16
GLA_TAU = 16.0
GRID_W = 64
ROPE_BASE = 10000.0
EPS = 1e-6
IN_WIDTHS = (D_A, D_A, D_A, D_A, D_A, 4 * H_A, D_B, D_B, D_B, D_B, H_C * DK_C, H_C * DK_C, D_C, D_C, 2 * GLA_RANK)

LANES = 128
SUBLANES = 8
V7X_VMEM_BYTES = 64 * 1024 * 1024
CH = 256
TOK = 1024
NCH = TOK // CH
CHUNK_UNROLL = NCH
VMEM_LIMIT = V7X_VMEM_BYTES - 2 * 1024 * 1024
NEG = -1e30
GLA_SAFE_LOG = -80.0

OFF_A, W_A = 0, 4 * D_A
OFF_S, W_S = OFF_A + W_A, LANES
OFF_B, W_B = OFF_S + W_S, 3 * D_B
OFF_C, W_C = OFF_B + W_B, 2 * H_C * DK_C + D_C
OFF_Z, W_Z = OFF_C + W_C, D_MODEL
D_INR = OFF_Z + W_Z
COL_FF, COL_FB = H_A, 3 * H_A


def _dot(a, b):
    return jnp.dot(a, b, preferred_element_type=F32)


def _dot_nt(a, b):
    return lax.dot_general(a, b, (((1,), (1,)), ((), ())), preferred_element_type=F32)


def _bf(x):
    return x.astype(BF16)


N_PIECES = 2


def _split(x):
    pieces, rest = [], x
    for _ in range(N_PIECES):
        pieces.append(_bf(rest))
        rest = rest - pieces[-1].astype(F32)
    return pieces


def _wide_dot_l(m_bf, x):
    return functools.reduce(jnp.add, [_dot(m_bf, p) for p in reversed(_split(x))])


def _wide_dot_r(x, m_bf):
    return functools.reduce(jnp.add, [_dot(p, m_bf) for p in reversed(_split(x))])


def _log_sigmoid(x):
    return jnp.minimum(x, 0.0) - jnp.log(1.0 + jnp.exp(-jnp.abs(x)))


def _sigmoid(x):
    return 1.0 / (1.0 + jnp.exp(-x))


def _row_to_col(row, n):
    eye = lax.broadcasted_iota(jnp.int32, (n, n), 0) == lax.broadcasted_iota(jnp.int32, (n, n), 1)
    return jnp.sum(jnp.where(eye, row, 0.0), axis=1, keepdims=True)


def _lane_block_mask(width, block, h):
    lane = lax.broadcasted_iota(jnp.int32, (1, width), 1)
    return ((lane >= h * block) & (lane < (h + 1) * block)).astype(F32)


def _log2(n):
    assert n & (n - 1) == 0
    return n.bit_length() - 1


def _block_diag_mask(rows, cols, rblock, cblock):
    r = lax.broadcasted_iota(jnp.int32, (rows, cols), 0) >> _log2(rblock)
    c = lax.broadcasted_iota(jnp.int32, (rows, cols), 1) >> _log2(cblock)
    return (r == c).astype(F32)


def _causal_masks():
    ri = lax.broadcasted_iota(jnp.int32, (CH, CH), 0)
    cj = lax.broadcasted_iota(jnp.int32, (CH, CH), 1)
    return cj <= ri, cj >= ri


def _bwd_gate_lanes():
    lane = lax.broadcasted_iota(jnp.int32, (1, LANES), 1)
    return (lane >= COL_FB) & (lane < COL_FB + H_A)


def _block_diag_value(ref, hn, a, b):
    rows = []
    for h in range(hn):
        wide = jnp.concatenate([ref[h], jnp.zeros((a, (hn - 1) * b), F32)], axis=1)
        rows.append(wide if h == 0 else pltpu.roll(wide, h * b, 1))
    return jnp.concatenate(rows, axis=0)


def _pieces(x):
    return jnp.concatenate(_split(x), axis=1)


def _cum_max(x, reverse):
    n = x.shape[0]
    row = lax.broadcasted_iota(jnp.int32, x.shape, 0)
    s = 1
    while s < n:
        if reverse:
            shifted = jnp.where(row < n - s, pltpu.roll(x, n - s, 0), NEG)
        else:
            shifted = jnp.where(row >= s, pltpu.roll(x, s, 0), NEG)
        x = jnp.maximum(x, shifted)
        s *= 2
    return x


GATE_BLOCK = 16


def _gate_constants():
    half = N_PIECES * GATE_BLOCK
    cols = [COL_FF + h for h in range(H_A)] + [COL_FB + h for h in range(H_A)]
    sp = np.zeros((2 * N_PIECES * LANES, LANES), np.float32)
    for part in range(2):
        for p in range(N_PIECES):
            for k in cols:
                sp[(part * N_PIECES + p) * LANES + k, part * half + p * GATE_BLOCK + k] = 1.0
    lane = np.arange(LANES)
    gl = np.zeros((16, LANES), np.float32)
    for hd, k in enumerate(cols):
        gl[hd] = (lane % GATE_BLOCK == k) & (lane < 2 * half)
    valid = np.isin(lane % GATE_BLOCK, cols)
    gl[2 * H_A] = valid & (lane < half)
    gl[2 * H_A + 1] = valid & (lane >= half) & (lane < 2 * half)
    return jnp.asarray(sp, BF16), jnp.asarray(gl, F32)


def _modulation_kernel(c_ref, w_ref, b_ref, o_ref):
    cv = c_ref[...]
    s = cv * _sigmoid(cv)
    o_ref[...] = _dot(_bf(s), _bf(w_ref[...])) + b_ref[...]


def _modulation(cstack, w_ada, b_ada):
    depth, d, d3 = w_ada.shape
    rows = cstack.shape[0]
    tn = 1024
    return pl.pallas_call(
        _modulation_kernel,
        out_shape=jax.ShapeDtypeStruct((depth, rows, d3), F32),
        grid=(depth, d3 // tn),
        in_specs=[
            pl.BlockSpec((rows, d), lambda l, j: (0, 0)),
            pl.BlockSpec((None, d, tn), lambda l, j: (l, 0, j)),
            pl.BlockSpec((None, 1, tn), lambda l, j: (l, 0, j)),
        ],
        out_specs=pl.BlockSpec((None, rows, tn), lambda l, j: (l, 0, j)),
        name="adaln_modulation",
    )(cstack, w_ada, b_ada.reshape(depth, 1, d3))


def _layer_kernel(*refs, chained, rotary, final):
    it = iter(refs)
    x_ref, mod_ref, ng_ref, win_ref, gb_ref, lg_ref, w2_ref, b2_ref, hng_ref, wout_ref = (next(it) for _ in range(10))
    sp_ref, gl_ref = next(it), next(it)
    fg_ref = next(it) if final else None
    if rotary:
        cos_ref, sin_ref = next(it), next(it)
    sC_ref, sn_ref, sm_ref_in, sR_ref, sG_ref = (next(it) for _ in range(5))
    out_ref = next(it)
    if not chained:
        oC_ref, on_ref, om_ref, oR_ref, oG_ref = (next(it) for _ in range(5))
    (h_ref, pj_ref, sm_ref, fc_ref, bf_ref, bb_ref, tot_ref, y_ref,
     gm_ref, min_ref, cm_ref, bop_ref) = (next(it) for _ in range(12))
    dec_ref = pj_ref.at[:, W_B:W_B + CH]
    if chained:
        uCN_ref, uR_ref, uG_ref = (next(it) for _ in range(3))

    seqs, seqlen, _ = x_ref.shape

    def x_rows(ref, c):
        if isinstance(c, int):
            r0 = c * CH
            return ref.at[r0 // seqlen, r0 % seqlen:r0 % seqlen + CH, :]
        r0 = pl.multiple_of(c * CH, CH)
        return ref.at[r0 >> _log2(seqlen), pl.ds(pl.multiple_of(r0 & (seqlen - 1), CH), CH), :]

    def norm_rows(c):
        x = x_rows(x_ref, c)[...]
        shift, scale = mod_ref[:, 0:D_MODEL], mod_ref[:, D_MODEL:2 * D_MODEL]
        xn = x * lax.rsqrt(jnp.mean(x * x, axis=-1, keepdims=True) + EPS) * ng_ref[...]
        h_ref[c * CH:(c + 1) * CH, :] = _bf(xn * (1.0 + scale) + shift)

    def front_rows(c):
        rows = slice(c * CH, (c + 1) * CH)
        proj = _dot(h_ref[rows, :], win_ref[:, OFF_A:OFF_S + W_S])
        pj_ref[rows, 0:W_A] = proj[:, 0:W_A]
        sm = proj[:, W_A:]
        sm_ref[rows, :] = sm
        la = _log_sigmoid(_dot(_bf(sm), w2_ref[...]) + b2_ref[...]) * (1.0 / GLA_TAU)
        tri = _bf(_causal_masks()[0].astype(F32))
        ls = _log_sigmoid(sm + gb_ref[...])
        ps = _wide_dot_l(tri, jnp.concatenate([ls, la], axis=1))
        tot = ps[CH - 1:CH, :]
        fc_ref[rows, :] = jnp.where(_bwd_gate_lanes(), tot[:, 0:LANES] - ps[:, 0:LANES] + ls, ps[:, 0:LANES])
        bf_ref[rows, :] = ps[:, LANES:2 * LANES]
        bb_ref[rows, :] = tot[:, 2 * LANES:] - ps[:, 2 * LANES:] + la[:, LANES:]
        tot_ref[c] = tot

    norm_rows(0)
    for c in range(NCH):
        if c + 1 < NCH:
            norm_rows(c + 1)
        front_rows(c)

    AQ, AK, AV, AO = 0, D_A, 2 * D_A, 3 * D_A
    k_scale = DH_A ** -0.5

    ones_cols = jnp.ones((CH, DH_A), F32)

    def mlstm_state_body(c, carry):
        rows = pl.ds(pl.multiple_of(c * CH, CH), CH)
        g_pre = sm_ref[rows, :] + gb_ref[...]
        rw = pltpu.roll(g_pre, H_A, 1) - fc_ref[rows, :]
        cm_ref[rows, :] = jnp.where(_bwd_gate_lanes(), _cum_max(rw, True), _cum_max(rw, False))
        bop_ref[rows, :] = _bf(_dot(_pieces(rw), sp_ref[N_PIECES * LANES:, :]) + gl_ref[2 * H_A:2 * H_A + 1, :])
        totg = tot_ref[c][:, 0:LANES]
        g = totg + rw
        gm = jnp.max(g, axis=0, keepdims=True)
        gm_ref[c] = gm
        wgt = jnp.exp(g - gm).T
        y_ref[rows, D_A:D_MODEL] = _dot(h_ref[rows, :], win_ref[:, OFF_C:OFF_C + W_C])
        if not chained:
            m_new = jnp.maximum(totg, gm)
            b_row = jnp.exp(gm - m_new)
            om_ref[c] = m_new
            min_ref[c] = jnp.zeros((1, LANES), F32)
        for hh in range(H_A):
            kt = (pj_ref[rows, AK + hh * DH_A:AK + (hh + 1) * DH_A] * k_scale).T
            v_h = pj_ref[rows, AV + hh * DH_A:AV + (hh + 1) * DH_A]
            lhs = jnp.concatenate([kt * wgt[COL_FF + hh:COL_FF + hh + 1, :],
                                   kt * wgt[COL_FB + hh:COL_FB + hh + 1, :]], axis=0)
            u = _dot(_bf(lhs), _bf(jnp.concatenate([v_h, ones_cols], axis=1)))
            for d, col in ((0, COL_FF), (1, COL_FB)):
                u_d = u[d * DH_A:(d + 1) * DH_A, :]
                if chained:
                    uCN_ref[c, d, hh] = u_d
                else:
                    fin = b_row[:, col + hh:col + hh + 1] * u_d
                    oC_ref[c, d, hh] = fin[:, 0:DH_A]
                    on_ref[c, d, hh:hh + 1, :] = fin[:, DH_A:].T[0:1, :]
        return carry

    def chunk_loop(state_body, out_body, unroll=CHUNK_UNROLL):
        if chained:
            return lax.fori_loop(0, NCH, out_body, 0, unroll=unroll)
        return lax.fori_loop(0, NCH, lambda c, carry: out_body(c, state_body(c, carry)), 0, unroll=unroll)

    if chained:
        lax.fori_loop(0, NCH, mlstm_state_body, 0, unroll=CHUNK_UNROLL)

    if chained:
        for d, order, col in ((0, range(NCH), COL_FF), (1, range(NCH - 1, -1, -1), COL_FB)):
            m_run = sm_ref_in[...]
            cn_run = [jnp.concatenate([sC_ref[d, hh],
                                       jnp.broadcast_to(sn_ref[d, hh:hh + 1, :], (DH_A, DH_A)).T], axis=1)
                      for hh in range(H_A)]
            for idx, c in enumerate(order):
                last = idx == NCH - 1
                if not last:
                    totg, gm = tot_ref[c][:, 0:LANES], gm_ref[c]
                    m_new = jnp.maximum(totg + m_run, gm)
                    a_row, b_row = jnp.exp(totg + m_run - m_new), jnp.exp(gm - m_new)
                for hh in range(H_A):
                    u = uCN_ref[c, d, hh]
                    uCN_ref[c, d, hh] = cn_run[hh]
                    if not last:
                        cn_run[hh] = (a_row[:, col + hh:col + hh + 1] * cn_run[hh]
                                      + b_row[:, col + hh:col + hh + 1] * u)
                min_ref[c] = m_run if d == 0 else jnp.where(_bwd_gate_lanes(), m_run, min_ref[c])
                if not last:
                    m_run = m_new

    def mlstm_out_body(c, carry):
        rows = pl.ds(pl.multiple_of(c * CH, CH), CH)
        lower, upper = _causal_masks()
        m_prev = min_ref[c]
        cmx = jnp.maximum(cm_ref[rows, :], m_prev)
        mi = fc_ref[rows, :] + cmx
        a_op = _bf(_dot(_pieces(-cmx), sp_ref[0:N_PIECES * LANES, :]) + gl_ref[2 * H_A + 1:2 * H_A + 2, :])
        b_op = bop_ref[rows, :]
        floor_all = jnp.exp(-mi)
        wa_all = jnp.exp(m_prev - cmx)
        for hh in range(H_A):
            q16 = _bf(pj_ref[rows, AQ + hh * DH_A:AQ + (hh + 1) * DH_A])
            k_h = pj_ref[rows, AK + hh * DH_A:AK + (hh + 1) * DH_A] * k_scale
            v_h = pj_ref[rows, AV + hh * DH_A:AV + (hh + 1) * DH_A]
            o_h = pj_ref[rows, AO + hh * DH_A:AO + (hh + 1) * DH_A]
            s = _dot_nt(q16, _bf(k_h))
            qs = []
            for d, msk in ((0, lower), (1, upper)):
                hd = d * H_A + hh
                dm = _dot_nt(a_op * _bf(gl_ref[hd:hd + 1, :]), b_op)
                qs.append(_bf(s * jnp.exp(jnp.where(msk, dm, NEG))))
            nd = _dot(jnp.concatenate(qs, axis=0), _bf(jnp.concatenate([v_h, ones_cols], axis=1)))
            hs = None
            for d, col in ((0, COL_FF + hh), (1, COL_FB + hh)):
                num, den = nd[d * CH:(d + 1) * CH, 0:DH_A], nd[d * CH:(d + 1) * CH, DH_A:]
                if chained:
                    wa = jnp.broadcast_to(wa_all[:, col:col + 1], (CH, DH_A))
                    inter = _dot(q16, _bf(uCN_ref[c, d, hh]))
                    num = num + wa * inter[:, 0:DH_A]
                    den = den + wa * inter[:, DH_A:]
                floor = jnp.broadcast_to(floor_all[:, col:col + 1], (CH, DH_A))
                part = num / jnp.maximum(jnp.abs(den), floor)
                hs = part if hs is None else hs + part
            y_ref[rows, hh * DH_A:(hh + 1) * DH_A] = _sigmoid(o_h) * hs
        return carry

    chunk_loop(mlstm_state_body, mlstm_out_body)

    BQ, BK, BV = 0, D_B, 2 * D_B

    def ret_proj_body(c, carry):
        rows = pl.ds(pl.multiple_of(c * CH, CH), CH)
        pj_ref[rows, 0:W_B] = _dot(h_ref[rows, :], win_ref[:, OFF_B:OFF_B + W_B])
        return carry

    def rotary_body(c, carry):
        rows = pl.ds(pl.multiple_of(c * CH, CH), CH)
        first_half = (lax.broadcasted_iota(jnp.int32, (1, D_B), 1) & (DH_B - 1)) < DH_B // 2
        for off in (BQ, BK):
            t = pj_ref[rows, off:off + D_B]
            partner = jnp.where(first_half, pltpu.roll(t, D_B - DH_B // 2, 1), pltpu.roll(t, DH_B // 2, 1))
            pj_ref[rows, off:off + D_B] = t * cos_ref[rows, :] + partner * sin_ref[rows, :]
        return carry

    lgam = _log_sigmoid(lg_ref[...])
    lg_f, lg_b = lgam[0:1, :], lgam[1:2, :]
    pos = lax.broadcasted_iota(jnp.int32, (CH, 1), 0).astype(F32)
    lower, upper = _causal_masks()
    rel = (lax.broadcasted_iota(jnp.int32, (CH, CH), 0) - lax.broadcasted_iota(jnp.int32, (CH, CH), 1)).astype(F32)
    for hh in range(H_B):
        lf = lg_f[:, hh * DH_B:hh * DH_B + 1]
        lb = lg_b[:, hh * DH_B:hh * DH_B + 1]
        dec_ref[hh * CH:(hh + 1) * CH, :] = (jnp.where(lower, jnp.exp(jnp.maximum(rel, 0.0) * lf), 0.0)
                                             + jnp.where(upper, jnp.exp(jnp.maximum(-rel, 0.0) * lb), 0.0))
    bd_b = _block_diag_mask(D_B, D_B, DH_B, DH_B)
    hm_b = [_lane_block_mask(D_B, DH_B, hh) for hh in range(H_B)]
    ret_scale = DH_B ** -0.5

    def ret_state_body(c, carry):
        rows = pl.ds(pl.multiple_of(c * CH, CH), CH)
        k = pj_ref[rows, BK:BK + D_B] * ret_scale
        v = pj_ref[rows, BV:BV + D_B]
        kf = k * jnp.exp((CH - 1.0 - pos) * lg_f)
        kb = k * jnp.exp(pos * lg_b)
        u = _dot(_bf(jnp.concatenate([kf, kb], axis=1).T), _bf(v))
        for d in range(2):
            u_d = u[d * D_B:(d + 1) * D_B, :]
            if chained:
                uR_ref[c, d] = u_d * bd_b
            else:
                for hh in range(H_B):
                    oR_ref[c, d, hh] = u_d[hh * DH_B:(hh + 1) * DH_B, hh * DH_B:(hh + 1) * DH_B]
        return carry

    def ret_scan():
        for d, order, lg_row in ((0, range(NCH), lg_f), (1, range(NCH - 1, -1, -1), lg_b)):
            g_col = _row_to_col(jnp.exp(float(CH) * lg_row), D_B)
            s_run = _block_diag_value(sR_ref.at[d], H_B, DH_B, DH_B)
            for idx, c in enumerate(order):
                u = uR_ref[c, d]
                uR_ref[c, d] = s_run
                if idx != NCH - 1:
                    s_run = g_col * s_run + u

    def ret_out_body(c, carry):
        rows = pl.ds(pl.multiple_of(c * CH, CH), CH)
        q = pj_ref[rows, BQ:BQ + D_B]
        k = pj_ref[rows, BK:BK + D_B] * ret_scale
        v = pj_ref[rows, BV:BV + D_B]
        kb16 = _bf(k)
        ps = [_bf(_dot_nt(_bf(q * hm_b[hh]), kb16) * dec_ref[hh * CH:(hh + 1) * CH, :]) for hh in range(H_B)]
        vst = jnp.concatenate([_bf(v * hm_b[hh]) for hh in range(H_B)], axis=0)
        yb = _dot(jnp.concatenate(ps, axis=1), vst)
        if chained:
            qf = q * jnp.exp((pos + 1.0) * lg_f)
            qb = q * jnp.exp((float(CH) - pos) * lg_b)
            s_in = jnp.concatenate([uR_ref[c, 0], uR_ref[c, 1]], axis=0)
            yb = yb + _dot(_bf(jnp.concatenate([qf, qb], axis=1)), _bf(s_in))
        y_ref[rows, D_A:D_A + D_B] = yb
        return carry

    KC = H_C * DK_C
    CQ, CK, CV = D_A, D_A + KC, D_A + 2 * KC
    gla_scale = DK_C ** -0.5
    bd_c = _block_diag_mask(KC, D_C, DK_C, DV_C)
    hm_ck = [_lane_block_mask(KC, DK_C, hh) for hh in range(H_C)]
    hm_cv = [_lane_block_mask(D_C, DV_C, hh) for hh in range(H_C)]

    def gla_state_body(c, carry):
        rows = pl.ds(pl.multiple_of(c * CH, CH), CH)
        k = y_ref[rows, CK:CK + KC] * gla_scale
        v = y_ref[rows, CV:CV + D_C]
        tot = tot_ref[c]
        khf = k * jnp.exp(tot[:, LANES:2 * LANES] - bf_ref[rows, :])
        khb = k * jnp.exp(tot[:, 2 * LANES:] - bb_ref[rows, :])
        u = _dot(_bf(jnp.concatenate([khf, khb], axis=1).T), _bf(v))
        for d in range(2):
            u_d = u[d * KC:(d + 1) * KC, :]
            if chained:
                uG_ref[c, d] = u_d * bd_c
            else:
                for hh in range(H_C):
                    oG_ref[c, d, hh] = u_d[hh * DK_C:(hh + 1) * DK_C, hh * DV_C:(hh + 1) * DV_C]
        return carry

    def gla_state_ret_proj(c, carry):
        return gla_state_body(c, ret_proj_body(c, carry))

    if chained:
        lax.fori_loop(0, NCH, gla_state_ret_proj, 0, unroll=CHUNK_UNROLL)

    if chained:
        for d, order in ((0, range(NCH)), (1, range(NCH - 1, -1, -1))):
            s_run = _block_diag_value(sG_ref.at[d], H_C, DK_C, DV_C)
            for idx, c in enumerate(order):
                u = uG_ref[c, d]
                uG_ref[c, d] = s_run
                if idx != NCH - 1:
                    tot_row = tot_ref[c][:, (1 + d) * LANES:(2 + d) * LANES]
                    s_run = _row_to_col(jnp.exp(tot_row), KC) * s_run + u

    def gla_out_body(c, carry, factorised):
        rows = pl.ds(pl.multiple_of(c * CH, CH), CH)

        def operands():
            q = y_ref[rows, CQ:CQ + KC]
            k = y_ref[rows, CK:CK + KC] * gla_scale
            v = y_ref[rows, CV:CV + D_C]
            b_f, b_b = bf_ref[rows, :], bb_ref[rows, :]
            return q * jnp.exp(b_f), q * jnp.exp(b_b), k, v, b_f, b_b

        def inter_chunk(qf, qb):
            if not chained:
                return jnp.zeros((CH, D_C), F32)
            s_in = jnp.concatenate([uG_ref[c, 0], uG_ref[c, 1]], axis=0)
            return _dot(_bf(jnp.concatenate([qf, qb], axis=1)), _bf(s_in))

        qf, qb, k, v, b_f, b_b = operands()
        if rotary:
            rotary_body(c, 0)
        if factorised:
            lower, upper = _causal_masks()
            kf16, kb16 = _bf(k * jnp.exp(-b_f)), _bf(k * jnp.exp(-b_b))
            ps = []
            for hh in range(H_C):
                s_f = _dot_nt(_bf(qf * hm_ck[hh]), kf16)
                s_b = _dot_nt(_bf(qb * hm_ck[hh]), kb16)
                ps.append(_bf(jnp.where(lower, s_f, 0.0) + jnp.where(upper, s_b, 0.0)))
            vst = jnp.concatenate([_bf(v * hm_cv[hh]) for hh in range(H_C)], axis=0)
            yc = _dot(jnp.concatenate(ps, axis=1), vst)
            y_ref[rows, D_A + D_B:D_MODEL] = yc + inter_chunk(qf, qb) if chained else yc
            return carry

        y_ref[rows, D_A + D_B:D_MODEL] = inter_chunk(qf, qb)
        key = lax.broadcasted_iota(jnp.int32, (CH, 1), 0)
        sub = SUBLANES

        def query_body(g, carry2):
            grp = pl.ds(pl.multiple_of(c * CH + g * sub, sub), sub)
            bq_f, bq_b, qg = bf_ref[grp, :], bb_ref[grp, :], y_ref[grp, CQ:CQ + KC]
            outs = []
            for r in range(sub):
                i = g * sub + r
                w_f = jnp.exp(jnp.where(key <= i, bq_f[r:r + 1, :] - b_f, NEG))
                w_b = jnp.exp(jnp.where(key >= i, bq_b[r:r + 1, :] - b_b, NEG))
                t = _dot(_bf(qg[r:r + 1, :] * k * (w_f + w_b)), _bf(bd_c))
                outs.append(jnp.sum(t * v, axis=0, keepdims=True))
            y_ref[grp, D_A + D_B:D_MODEL] += jnp.concatenate(outs, axis=0)
            return carry2

        lax.fori_loop(0, CH // sub, query_body, 0)
        return carry

    all_safe = jnp.min(tot_ref[...][:, :, LANES:]) > GLA_SAFE_LOG

    @pl.when(all_safe)
    def _():
        chunk_loop(gla_state_ret_proj, functools.partial(gla_out_body, factorised=True))

    @pl.when(jnp.logical_not(all_safe))
    def _():
        chunk_loop(gla_state_ret_proj, functools.partial(gla_out_body, factorised=False), unroll=1)

    if chained:
        lax.fori_loop(0, NCH, ret_state_body, 0, unroll=CHUNK_UNROLL)
        ret_scan()
    chunk_loop(ret_state_body, ret_out_body)

    def gate_proj(c):
        rows = slice(c * CH, (c + 1) * CH)
        pj_ref[rows, 0:W_Z] = _dot(h_ref[rows, :], win_ref[:, OFF_Z:OFF_Z + W_Z])

    def head_norm_gate(c):
        rows = slice(c * CH, (c + 1) * CH)
        parts = []
        for hh in range(H_A):
            ya = y_ref[rows, hh * DH_A:(hh + 1) * DH_A]
            parts.append(ya * lax.rsqrt(jnp.mean(ya * ya, axis=1, keepdims=True) + EPS))
        ybc = y_ref[rows, D_A:D_MODEL]
        seg = _bf(_block_diag_mask(D_B + D_C, D_B + D_C, DH_B, DH_B) * (1.0 / DH_B))
        parts.append(ybc * lax.rsqrt(_wide_dot_r(ybc * ybc, seg) + EPS))
        z = pj_ref[rows, 0:W_Z]
        yg = jnp.concatenate(parts, axis=1) * hng_ref[...] * (z * _sigmoid(z))
        h_ref[rows, :] = _bf(yg)

    def out_proj_residual(c):
        rows = slice(c * CH, (c + 1) * CH)
        xo = x_rows(x_ref, c)[...] + mod_ref[:, 2 * D_MODEL:] * _dot(h_ref[rows, :], wout_ref[...])
        if final:
            xo = xo * lax.rsqrt(jnp.mean(xo * xo, axis=-1, keepdims=True) + EPS) * fg_ref[...]
        x_rows(out_ref, c)[...] = xo

    gate_proj(0)
    for c in range(NCH):
        if c + 1 < NCH:
            gate_proj(c + 1)
        head_norm_gate(c)
        if c > 0:
            out_proj_residual(c - 1)
    out_proj_residual(NCH - 1)


def _layer_call(l, x, mods, mod_row, norm_g, w_in_r, gate_b, lg_rows, w2_full, b2_full, hn_g, w_out_b, final_g,
                states, rot=None, chained=False, final=False):
    batch, seqlen, d = x.shape
    seqs = TOK // seqlen
    steps = batch // seqs
    assert seqs * seqlen == TOK and steps * seqs == batch and seqlen % CH == 0
    assert chained == (seqs == 1)
    assert chained or seqlen == CH

    def const(shape):
        nd = len(shape)
        return pl.BlockSpec(shape, lambda i, _nd=nd: (0,) * _nd)

    def layer_block(shape):
        nd = len(shape)
        return pl.BlockSpec((None,) + shape, lambda i, _nd=nd: (l,) + (0,) * _nd)

    in_specs = [
        pl.BlockSpec((seqs, seqlen, d), lambda i: (i, 0, 0)),
        pl.BlockSpec((None, None, 1, 3 * d),
                     (lambda i: (l, i, 0, 0)) if mod_row is None else (lambda i: (l, mod_row, 0, 0))),
        layer_block((1, d)),
        pl.BlockSpec((None, d, D_INR), lambda i: (l, 0, 0), pipeline_mode=pl.Buffered(1)),
        layer_block((1, LANES)),
        layer_block((2, D_B)),
        layer_block((LANES, 2 * LANES)),
        layer_block((1, 2 * LANES)),
        layer_block((1, d)),
        pl.BlockSpec((None, d, d), lambda i: (l, 0, 0), pipeline_mode=pl.Buffered(1)),
    ]
    args = [x, mods, norm_g.reshape(-1, 1, d), w_in_r, gate_b, lg_rows, w2_full, b2_full,
            hn_g.reshape(-1, 1, d), w_out_b]
    for cst in _gate_constants():
        in_specs.append(pl.BlockSpec(cst.shape, lambda i, _nd=cst.ndim: (0,) * _nd, pipeline_mode=pl.Buffered(1)))
        args.append(cst)
    if final:
        in_specs.append(const((1, d)))
        args.append(final_g.reshape(1, d))
    if rot is not None:
        in_specs += [pl.BlockSpec((seqlen, D_B), lambda i: (0, 0), pipeline_mode=pl.Buffered(1))] * 2
        args += list(rot)
    state_blocks = [(2, H_A, DH_A, DH_A), (2, H_A, DH_A), (1, LANES), (2, H_B, DH_B, DH_B), (2, H_C, DK_C, DV_C)]
    out_shape = [jax.ShapeDtypeStruct(x.shape, x.dtype)]
    out_specs = [pl.BlockSpec((seqs, seqlen, d), lambda i: (i, 0, 0))]
    aliases = {}
    for blk, arr in zip(state_blocks, states):
        zeros = (0,) * len(blk)
        if chained:
            in_specs.append(pl.BlockSpec((None, None) + blk, lambda i, _z=zeros: (i, l) + _z))
        else:
            in_specs.append(pl.BlockSpec(memory_space=pl.ANY))
            aliases[len(args)] = len(out_shape)
            out_shape.append(jax.ShapeDtypeStruct(arr.shape, arr.dtype))
            out_specs.append(pl.BlockSpec((seqs, None) + blk, lambda i, _z=zeros: (i, l) + _z))
        args.append(arr)

    scratch = [
        pltpu.VMEM((TOK, d), BF16),
        pltpu.VMEM((TOK, W_A), F32),
        pltpu.VMEM((TOK, LANES), F32),
        pltpu.VMEM((TOK, LANES), F32),
        pltpu.VMEM((TOK, LANES), F32),
        pltpu.VMEM((TOK, LANES), F32),
        pltpu.VMEM((NCH, 1, 3 * LANES), F32),
        pltpu.VMEM((TOK, d), F32),
        pltpu.VMEM((NCH, 1, LANES), F32),
        pltpu.VMEM((NCH, 1, LANES), F32),
        pltpu.VMEM((TOK, LANES), F32),
        pltpu.VMEM((TOK, LANES), BF16),
    ]
    if chained:
        scratch += [
            pltpu.VMEM((NCH, 2, H_A, DH_A, 2 * DH_A), F32),
            pltpu.VMEM((NCH, 2, D_B, D_B), F32),
            pltpu.VMEM((NCH, 2, H_C * DK_C, D_C), F32),
        ]
    outs = pl.pallas_call(
        functools.partial(_layer_kernel, chained=chained, rotary=rot is not None, final=final),
        out_shape=out_shape,
        grid=(steps,),
        in_specs=in_specs,
        out_specs=out_specs,
        scratch_shapes=scratch,
        input_output_aliases=aliases,
        compiler_params=pltpu.CompilerParams(dimension_semantics=("arbitrary",), vmem_limit_bytes=VMEM_LIMIT),
        name=("latent" if chained else "context") + f"_layer{l}",
    )(*args)
    return outs


def _rotary_tables(seqlen):
    rows = seqlen // GRID_W
    r = jnp.repeat(jnp.arange(rows, dtype=F32), GRID_W)
    col = jnp.tile(jnp.arange(GRID_W, dtype=F32), rows)
    n_f = DH_B // 4
    freqs = ROPE_BASE ** (-jnp.arange(n_f, dtype=F32) / n_f)
    ang = jnp.concatenate([r[:, None] * freqs, col[:, None] * freqs], axis=-1)
    cos, sin = jnp.cos(ang), jnp.sin(ang)
    cos_l = jnp.tile(jnp.concatenate([cos, cos], axis=-1), (1, H_B))
    sin_l = jnp.tile(jnp.concatenate([-sin, sin], axis=-1), (1, H_B))
    return cos_l, sin_l


def _gate_lanes(m):
    z = jnp.zeros(m.shape[:-2] + (LANES,), m.dtype)
    z = z.at[..., COL_FF:COL_FF + H_A].set(m[..., 0, :]).at[..., COL_FB:COL_FB + H_A].set(m[..., 1, :])
    return z[..., None, :]


def _pack_segments():
    src = dict(zip("aq ak av ao az ag bq bk bv bz cq ck cv cz clr".split(),
                   zip(np.cumsum((0,) + IN_WIDTHS[:-1]).tolist(), IN_WIDTHS)))
    segs = []
    for names, dst in (("aq ak av ao", OFF_A), ("ag clr", OFF_S), ("bq bk bv", OFF_B), ("cq ck cv", OFF_C),
                       ("az bz cz", OFF_Z)):
        for name in names.split():
            s, w = src[name]
            segs.append((s, dst, w))
            dst += w
    return segs


def _pack_kernel(wt_ref, o_ref):
    tr = o_ref.shape[0]
    small = []
    for s, dst, w in _pack_segments():
        if w % LANES == 0:
            o_ref[:, dst:dst + w] = _bf(wt_ref[s:s + w, :].T)
        else:
            small.append(wt_ref[s:s + w, :])
    small.append(jnp.zeros((W_S - sum(t.shape[0] for t in small), tr), F32))
    o_ref[:, OFF_S:OFF_S + W_S] = _bf(jnp.concatenate(small, axis=0).T)


def _pack_w_in(w_in):
    depth, d, d_in = w_in.shape
    tr = 256
    return pl.pallas_call(
        _pack_kernel,
        out_shape=jax.ShapeDtypeStruct((depth, d, D_INR), BF16),
        grid=(depth, d // tr),
        in_specs=[pl.BlockSpec((None, d_in, tr), lambda l, r: (l, 0, r))],
        out_specs=pl.BlockSpec((None, tr, D_INR), lambda l, r: (l, r, 0)),
        name="pack_w_in",
    )(jnp.swapaxes(w_in, 1, 2))


def kernel(x_prompt, x_sample, state_mlstm_C, state_mlstm_n, state_mlstm_m, state_ret, state_gla, c, c_ctx,
           norm_g, w_ada, b_ada, w_in, mlstm_gate_b, ret_decay_logit, gla_w2, gla_b2, headnorm_g, w_out, final_g):
    depth = w_in.shape[0]
    dec_batch = c.shape[0]

    w_in_r = _pack_w_in(w_in)
    w_out_b = w_out.astype(BF16)
    n_g = mlstm_gate_b.shape[-1]
    gate_b = jnp.pad(mlstm_gate_b, ((0, 0), (0, LANES - n_g)))[:, None, :]
    lg_rows = jnp.repeat(ret_decay_logit, DH_B, axis=-1)
    w2_full = jnp.zeros((depth, LANES, 2 * LANES), F32)
    w2_full = w2_full.at[:, n_g:n_g + GLA_RANK, 0:LANES].set(gla_w2[:, 0])
    w2_full = w2_full.at[:, n_g + GLA_RANK:n_g + 2 * GLA_RANK, LANES:].set(gla_w2[:, 1]).astype(BF16)
    b2_full = gla_b2.reshape(depth, 1, 2 * LANES)

    mod_rows = -(-(dec_batch + 1) // SUBLANES) * SUBLANES
    cstack = jnp.zeros((mod_rows, D_MODEL), F32).at[0:dec_batch].set(c).at[dec_batch].set(c_ctx)
    mods = _modulation(cstack, w_ada, b_ada)[:, :, None, :]

    rot = _rotary_tables(x_sample.shape[1])
    cache = (state_mlstm_C, state_mlstm_n, _gate_lanes(state_mlstm_m), state_ret, state_gla)
    new = tuple(jnp.zeros((x_prompt.shape[0],) + s.shape[1:], F32) for s in cache)

    common = (norm_g, w_in_r, gate_b, lg_rows, w2_full, b2_full, headnorm_g, w_out_b, final_g)
    xp, xs = x_prompt, x_sample
    for l in range(depth):
        xp, *new = _layer_call(l, xp, mods, dec_batch, *common, states=new, final=l == depth - 1)
    for l in range(depth):
        xs = _layer_call(l, xs, mods, None, *common, states=cache, rot=rot, chained=True, final=l == depth - 1)[0]

    new_c, new_n, m_l, new_r, new_g = new
    new_m = jnp.stack([m_l[:, :, 0, COL_FF:COL_FF + H_A], m_l[:, :, 0, COL_FB:COL_FB + H_A]], axis=2)
    return (xp, xs, new_c, new_n, new_m, new_r, new_g)
```

```python
import functools

import jax
import jax.numpy as jnp
import numpy as np
from jax import lax
from jax.experimental import pallas as pl
from jax.experimental.pallas import tpu as pltpu

F32 = jnp.float32
BF16 = jnp.bfloat16

D_MODEL = 1024
H_A, DH_A = 4, 128
H_B, DH_B = 4, 64
H_C, DK_C, DV_C = 4, 32, 64
D_A, D_B, D_C = H_A * DH_A, H_B * DH_B, H_C * DV_C
GLA_RANK = 16
GLA_TAU = 16.0
GRID_W = 64
ROPE_BASE = 10000.0
EPS = 1e-6
IN_WIDTHS = (D_A, D_A, D_A, D_A, D_A, 4 * H_A, D_B, D_B, D_B, D_B, H_C * DK_C, H_C * DK_C, D_C, D_C, 2 * GLA_RANK)

LANES = 128
SUBLANES = 8
V7X_VMEM_BYTES = 64 * 1024 * 1024
CH = 256
TOK = 1024
NCH = TOK // CH
PROJ_ROWS = 2 * CH
CHUNK_UNROLL = NCH
VMEM_LIMIT = V7X_VMEM_BYTES - 2 * 1024 * 1024
NEG = -1e30
GLA_SAFE_LOG = -80.0

OFF_A, W_A = 0, 4 * D_A
OFF_S, W_S = OFF_A + W_A, LANES
OFF_B, W_B = OFF_S + W_S, 3 * D_B
OFF_C, W_C = OFF_B + W_B, 2 * H_C * DK_C + D_C
OFF_Z, W_Z = OFF_C + W_C, D_MODEL
D_INR = OFF_Z + W_Z
COL_FF, COL_FB = H_A, 3 * H_A


def _dot(a, b):
    return jnp.dot(a, b, preferred_element_type=F32)


def _dot_nt(a, b):
    return lax.dot_general(a, b, (((1,), (1,)), ((), ())), preferred_element_type=F32)


def _bf(x):
    return x.astype(BF16)


N_PIECES = 2


def _split(x):
    pieces, rest = [], x
    for _ in range(N_PIECES):
        pieces.append(_bf(rest))
        rest = rest - pieces[-1].astype(F32)
    return pieces


def _wide_dot_l(m_bf, x):
    return functools.reduce(jnp.add, [_dot(m_bf, p) for p in reversed(_split(x))])


def _wide_dot_r(x, m_bf):
    return functools.reduce(jnp.add, [_dot(p, m_bf) for p in reversed(_split(x))])


def _log_sigmoid(x):
    return jnp.minimum(x, 0.0) - jnp.log(1.0 + jnp.exp(-jnp.abs(x)))


def _sigmoid(x):
    return 1.0 / (1.0 + jnp.exp(-x))


def _row_to_col(row, n):
    eye = lax.broadcasted_iota(jnp.int32, (n, n), 0) == lax.broadcasted_iota(jnp.int32, (n, n), 1)
    return jnp.sum(jnp.where(eye, row, 0.0), axis=1, keepdims=True)


def _lane_block_mask(width, block, h):
    lane = lax.broadcasted_iota(jnp.int32, (1, width), 1)
    return ((lane >= h * block) & (lane < (h + 1) * block)).astype(F32)


def _log2(n):
    assert n & (n - 1) == 0
    return n.bit_length() - 1


def _block_diag_mask(rows, cols, rblock, cblock):
    r = lax.broadcasted_iota(jnp.int32, (rows, cols), 0) >> _log2(rblock)
    c = lax.broadcasted_iota(jnp.int32, (rows, cols), 1) >> _log2(cblock)
    return (r == c).astype(F32)


def _causal_masks():
    ri = lax.broadcasted_iota(jnp.int32, (CH, CH), 0)
    cj = lax.broadcasted_iota(jnp.int32, (CH, CH), 1)
    return cj <= ri, cj >= ri


def _bwd_gate_lanes():
    lane = lax.broadcasted_iota(jnp.int32, (1, LANES), 1)
    return (lane >= COL_FB) & (lane < COL_FB + H_A)


def _block_diag_value(ref, hn, a, b):
    rows = []
    for h in range(hn):
        wide = jnp.concatenate([ref[h], jnp.zeros((a, (hn - 1) * b), F32)], axis=1)
        rows.append(wide if h == 0 else pltpu.roll(wide, h * b, 1))
    return jnp.concatenate(rows, axis=0)


def _pieces(x):
    return jnp.concatenate(_split(x), axis=1)


def _cum_max(x, reverse):
    n = x.shape[0]
    row = lax.broadcasted_iota(jnp.int32, x.shape, 0)
    s = 1
    while s < n:
        if reverse:
            shifted = jnp.where(row < n - s, pltpu.roll(x, n - s, 0), NEG)
        else:
            shifted = jnp.where(row >= s, pltpu.roll(x, s, 0), NEG)
        x = jnp.maximum(x, shifted)
        s *= 2
    return x


GATE_BLOCK = 16


def _gate_constants():
    half = N_PIECES * GATE_BLOCK
    cols = [COL_FF + h for h in range(H_A)] + [COL_FB + h for h in range(H_A)]
    sp = np.zeros((2 * N_PIECES * LANES, LANES), np.float32)
    for part in range(2):
        for p in range(N_PIECES):
            for k in cols:
                sp[(part * N_PIECES + p) * LANES + k, part * half + p * GATE_BLOCK + k] = 1.0
    lane = np.arange(LANES)
    gl = np.zeros((16, LANES), np.float32)
    for hd, k in enumerate(cols):
        gl[hd] = (lane % GATE_BLOCK == k) & (lane < 2 * half)
    valid = np.isin(lane % GATE_BLOCK, cols)
    gl[2 * H_A] = valid & (lane < half)
    gl[2 * H_A + 1] = valid & (lane >= half) & (lane < 2 * half)
    return jnp.asarray(sp, BF16), jnp.asarray(gl, F32)


def _modulation_kernel(c_ref, w_ref, b_ref, o_ref):
    cv = c_ref[...]
    s = cv * _sigmoid(cv)
    o_ref[...] = _dot(_bf(s), _bf(w_ref[...])) + b_ref[...]


def _modulation(cstack, w_ada, b_ada):
    depth, d, d3 = w_ada.shape
    rows = cstack.shape[0]
    tn = 1024
    return pl.pallas_call(
        _modulation_kernel,
        out_shape=jax.ShapeDtypeStruct((depth, rows, d3), F32),
        grid=(depth, d3 // tn),
        in_specs=[
            pl.BlockSpec((rows, d), lambda l, j: (0, 0)),
            pl.BlockSpec((None, d, tn), lambda l, j: (l, 0, j)),
            pl.BlockSpec((None, 1, tn), lambda l, j: (l, 0, j)),
        ],
        out_specs=pl.BlockSpec((None, rows, tn), lambda l, j: (l, 0, j)),
        name="adaln_modulation",
    )(cstack, w_ada, b_ada.reshape(depth, 1, d3))


def _layer_kernel(*refs, chained, rotary, final):
    it = iter(refs)
    x_ref, mod_ref, ng_ref, win_ref, gb_ref, lg_ref, w2_ref, b2_ref, hng_ref, wout_ref = (next(it) for _ in range(10))
    sp_ref, gl_ref = next(it), next(it)
    fg_ref = next(it) if final else None
    if rotary:
        cos_ref, sin_ref = next(it), next(it)
    sC_ref, sn_ref, sm_ref_in, sR_ref, sG_ref = (next(it) for _ in range(5))
    out_ref = next(it)
    if not chained:
        oC_ref, on_ref, om_ref, oR_ref, oG_ref = (next(it) for _ in range(5))
    (h_ref, pj_ref, sm_ref, fc_ref, bf_ref, bb_ref, tot_ref, y_ref,
     gm_ref, min_ref, cm_ref, bop_ref) = (next(it) for _ in range(12))
    dec_ref = pj_ref.at[:, W_B:W_B + CH]
    if chained:
        uCN_ref, uR_ref, uG_ref = (next(it) for _ in range(3))

    seqs, seqlen, _ = x_ref.shape

    def x_rows(ref, c):
        if isinstance(c, int):
            r0 = c * CH
            return ref.at[r0 // seqlen, r0 % seqlen:r0 % seqlen + CH, :]
        r0 = pl.multiple_of(c * CH, CH)
        return ref.at[r0 >> _log2(seqlen), pl.ds(pl.multiple_of(r0 & (seqlen - 1), CH), CH), :]

    def norm_rows(c):
        x = x_rows(x_ref, c)[...]
        shift, scale = mod_ref[:, 0:D_MODEL], mod_ref[:, D_MODEL:2 * D_MODEL]
        xn = x * lax.rsqrt(jnp.mean(x * x, axis=-1, keepdims=True) + EPS) * ng_ref[...]
        h_ref[c * CH:(c + 1) * CH, :] = _bf(xn * (1.0 + scale) + shift)

    def front_proj(g):
        rows = slice(g * PROJ_ROWS, (g + 1) * PROJ_ROWS)
        proj = _dot(h_ref[rows, :], win_ref[:, OFF_A:OFF_S + W_S])
        pj_ref[rows, 0:W_A] = proj[:, 0:W_A]
        sm_ref[rows, :] = proj[:, W_A:]

    def front_rows(c):
        rows = slice(c * CH, (c + 1) * CH)
        sm = sm_ref[rows, :]
        la = _log_sigmoid(_dot(_bf(sm), w2_ref[...]) + b2_ref[...]) * (1.0 / GLA_TAU)
        tri = _bf(_causal_masks()[0].astype(F32))
        ls = _log_sigmoid(sm + gb_ref[...])
        ps = _wide_dot_l(tri, jnp.concatenate([ls, la], axis=1))
        tot = ps[CH - 1:CH, :]
        fc_ref[rows, :] = jnp.where(_bwd_gate_lanes(), tot[:, 0:LANES] - ps[:, 0:LANES] + ls, ps[:, 0:LANES])
        bf_ref[rows, :] = ps[:, LANES:2 * LANES]
        bb_ref[rows, :] = tot[:, 2 * LANES:] - ps[:, 2 * LANES:] + la[:, LANES:]
        tot_ref[c] = tot

    per = PROJ_ROWS // CH
    for c in range(per):
        norm_rows(c)
    for g in range(NCH // per):
        for c in range((g + 1) * per, min((g + 2) * per, NCH)):
            norm_rows(c)
        front_proj(g)
        for c in range(g * per, (g + 1) * per):
            front_rows(c)

    AQ, AK, AV, AO = 0, D_A, 2 * D_A, 3 * D_A
    k_scale = DH_A ** -0.5

    ones_cols = jnp.ones((CH, DH_A), F32)

    def mlstm_state_body(c, carry):
        rows = pl.ds(pl.multiple_of(c * CH, CH), CH)
        g_pre = sm_ref[rows, :] + gb_ref[...]
        rw = pltpu.roll(g_pre, H_A, 1) - fc_ref[rows, :]
        cm_ref[rows, :] = jnp.where(_bwd_gate_lanes(), _cum_max(rw, True), _cum_max(rw, False))
        bop_ref[rows, :] = _bf(_dot(_pieces(rw), sp_ref[N_PIECES * LANES:, :]) + gl_ref[2 * H_A:2 * H_A + 1, :])
        totg = tot_ref[c][:, 0:LANES]
        g = totg + rw
        gm = jnp.max(g, axis=0, keepdims=True)
        gm_ref[c] = gm
        wgt = jnp.exp(g - gm).T
        y_ref[rows, D_A:D_MODEL] = _dot(h_ref[rows, :], win_ref[:, OFF_C:OFF_C + W_C])
        if not chained:
            m_new = jnp.maximum(totg, gm)
            b_row = jnp.exp(gm - m_new)
            om_ref[c] = m_new
            min_ref[c] = jnp.zeros((1, LANES), F32)
        for hh in range(H_A):
            kt = (pj_ref[rows, AK + hh * DH_A:AK + (hh + 1) * DH_A] * k_scale).T
            v_h = pj_ref[rows, AV + hh * DH_A:AV + (hh + 1) * DH_A]
            lhs = jnp.concatenate([kt * wgt[COL_FF + hh:COL_FF + hh + 1, :],
                                   kt * wgt[COL_FB + hh:COL_FB + hh + 1, :]], axis=0)
            u = _dot(_bf(lhs), _bf(jnp.concatenate([v_h, ones_cols], axis=1)))
            for d, col in ((0, COL_FF), (1, COL_FB)):
                u_d = u[d * DH_A:(d + 1) * DH_A, :]
                if chained:
                    uCN_ref[c, d, hh] = u_d
                else:
                    fin = b_row[:, col + hh:col + hh + 1] * u_d
                    oC_ref[c, d, hh] = fin[:, 0:DH_A]
                    on_ref[c, d, hh:hh + 1, :] = fin[:, DH_A:].T[0:1, :]
        return carry

    def chunk_loop(state_body, out_body, unroll=CHUNK_UNROLL):
        if chained:
            return lax.fori_loop(0, NCH, out_body, 0, unroll=unroll)
        return lax.fori_loop(0, NCH, lambda c, carry: out_body(c, state_body(c, carry)), 0, unroll=unroll)

    if chained:
        lax.fori_loop(0, NCH, mlstm_state_body, 0, unroll=CHUNK_UNROLL)

    if chained:
        for d, order, col in ((0, range(NCH), COL_FF), (1, range(NCH - 1, -1, -1), COL_FB)):
            m_run = sm_ref_in[...]
            cn_run = [jnp.concatenate([sC_ref[d, hh],
                                       jnp.broadcast_to(sn_ref[d, hh:hh + 1, :], (DH_A, DH_A)).T], axis=1)
                      for hh in range(H_A)]
            for idx, c in enumerate(order):
                last = idx == NCH - 1
                if not last:
                    totg, gm = tot_ref[c][:, 0:LANES], gm_ref[c]
                    m_new = jnp.maximum(totg + m_run, gm)
                    a_row, b_row = jnp.exp(totg + m_run - m_new), jnp.exp(gm - m_new)
                for hh in range(H_A):
                    u = uCN_ref[c, d, hh]
                    uCN_ref[c, d, hh] = cn_run[hh]
                    if not last:
                        cn_run[hh] = (a_row[:, col + hh:col + hh + 1] * cn_run[hh]
                                      + b_row[:, col + hh:col + hh + 1] * u)
                min_ref[c] = m_run if d == 0 else jnp.where(_bwd_gate_lanes(), m_run, min_ref[c])
                if not last:
                    m_run = m_new

    def mlstm_out_body(c, carry):
        rows = pl.ds(pl.multiple_of(c * CH, CH), CH)
        lower, upper = _causal_masks()
        m_prev = min_ref[c]
        cmx = jnp.maximum(cm_ref[rows, :], m_prev)
        mi = fc_ref[rows, :] + cmx
        a_op = _bf(_dot(_pieces(-cmx), sp_ref[0:N_PIECES * LANES, :]) + gl_ref[2 * H_A + 1:2 * H_A + 2, :])
        b_op = bop_ref[rows, :]
        floor_all = jnp.exp(-mi)
        wa_all = jnp.exp(m_prev - cmx)
        for hh in range(H_A):
            q16 = _bf(pj_ref[rows, AQ + hh * DH_A:AQ + (hh + 1) * DH_A])
            k_h = pj_ref[rows, AK + hh * DH_A:AK + (hh + 1) * DH_A] * k_scale
            v_h = pj_ref[rows, AV + hh * DH_A:AV + (hh + 1) * DH_A]
            o_h = pj_ref[rows, AO + hh * DH_A:AO + (hh + 1) * DH_A]
            s = _dot_nt(q16, _bf(k_h))
            qs = []
            for d, msk in ((0, lower), (1, upper)):
                hd = d * H_A + hh
                dm = _dot_nt(a_op * _bf(gl_ref[hd:hd + 1, :]), b_op)
                qs.append(_bf(s * jnp.exp(jnp.where(msk, dm, NEG))))
            nd = _dot(jnp.concatenate(qs, axis=0), _bf(jnp.concatenate([v_h, ones_cols], axis=1)))
            hs = None
            for d, col in ((0, COL_FF + hh), (1, COL_FB + hh)):
                num, den = nd[d * CH:(d + 1) * CH, 0:DH_A], nd[d * CH:(d + 1) * CH, DH_A:]
                if chained:
                    wa = jnp.broadcast_to(wa_all[:, col:col + 1], (CH, DH_A))
                    inter = _dot(q16, _bf(uCN_ref[c, d, hh]))
                    num = num + wa * inter[:, 0:DH_A]
                    den = den + wa * inter[:, DH_A:]
                floor = jnp.broadcast_to(floor_all[:, col:col + 1], (CH, DH_A))
                part = num / jnp.maximum(jnp.abs(den), floor)
                hs = part if hs is None else hs + part
            y_ref[rows, hh * DH_A:(hh + 1) * DH_A] = _sigmoid(o_h) * hs
        return carry

    chunk_loop(mlstm_state_body, mlstm_out_body)

    BQ, BK, BV = 0, D_B, 2 * D_B

    def ret_proj_body(c, carry):
        rows = pl.ds(pl.multiple_of(c * CH, CH), CH)
        pj_ref[rows, 0:W_B] = _dot(h_ref[rows, :], win_ref[:, OFF_B:OFF_B + W_B])
        return carry

    def rotary_body(c, carry):
        rows = pl.ds(pl.multiple_of(c * CH, CH), CH)
        first_half = (lax.broadcasted_iota(jnp.int32, (1, D_B), 1) & (DH_B - 1)) < DH_B // 2
        for off in (BQ, BK):
            t = pj_ref[rows, off:off + D_B]
            partner = jnp.where(first_half, pltpu.roll(t, D_B - DH_B // 2, 1), pltpu.roll(t, DH_B // 2, 1))
            pj_ref[rows, off:off + D_B] = t * cos_ref[rows, :] + partner * sin_ref[rows, :]
        return carry

    lgam = _log_sigmoid(lg_ref[...])
    lg_f, lg_b = lgam[0:1, :], lgam[1:2, :]
    pos = lax.broadcasted_iota(jnp.int32, (CH, 1), 0).astype(F32)
    lower, upper = _causal_masks()
    rel = (lax.broadcasted_iota(jnp.int32, (CH, CH), 0) - lax.broadcasted_iota(jnp.int32, (CH, CH), 1)).astype(F32)
    for hh in range(H_B):
        lf = lg_f[:, hh * DH_B:hh * DH_B + 1]
        lb = lg_b[:, hh * DH_B:hh * DH_B + 1]
        dec_ref[hh * CH:(hh + 1) * CH, :] = (jnp.where(lower, jnp.exp(jnp.maximum(rel, 0.0) * lf), 0.0)
                                             + jnp.where(upper, jnp.exp(jnp.maximum(-rel, 0.0) * lb), 0.0))
    bd_b = _block_diag_mask(D_B, D_B, DH_B, DH_B)
    hm_b = [_lane_block_mask(D_B, DH_B, hh) for hh in range(H_B)]
    ret_scale = DH_B ** -0.5

    def ret_state_body(c, carry):
        rows = pl.ds(pl.multiple_of(c * CH, CH), CH)
        k = pj_ref[rows, BK:BK + D_B] * ret_scale
        v = pj_ref[rows, BV:BV + D_B]
        kf = k * jnp.exp((CH - 1.0 - pos) * lg_f)
        kb = k * jnp.exp(pos * lg_b)
        u = _dot(_bf(jnp.concatenate([kf, kb], axis=1).T), _bf(v))
        for d in range(2):
            u_d = u[d * D_B:(d + 1) * D_B, :]
            if chained:
                uR_ref[c, d] = u_d * bd_b
            else:
                for hh in range(H_B):
                    oR_ref[c, d, hh] = u_d[hh * DH_B:(hh + 1) * DH_B, hh * DH_B:(hh + 1) * DH_B]
        return carry

    def ret_scan():
        for d, order, lg_row in ((0, range(NCH), lg_f), (1, range(NCH - 1, -1, -1), lg_b)):
            g_col = _row_to_col(jnp.exp(float(CH) * lg_row), D_B)
            s_run = _block_diag_value(sR_ref.at[d], H_B, DH_B, DH_B)
            for idx, c in enumerate(order):
                u = uR_ref[c, d]
                uR_ref[c, d] = s_run
                if idx != NCH - 1:
                    s_run = g_col * s_run + u

    def ret_out_body(c, carry):
        rows = pl.ds(pl.multiple_of(c * CH, CH), CH)
        q = pj_ref[rows, BQ:BQ + D_B]
        k = pj_ref[rows, BK:BK + D_B] * ret_scale
        v = pj_ref[rows, BV:BV + D_B]
        kb16 = _bf(k)
        ps = [_bf(_dot_nt(_bf(q * hm_b[hh]), kb16) * dec_ref[hh * CH:(hh + 1) * CH, :]) for hh in range(H_B)]
        vst = jnp.concatenate([_bf(v * hm_b[hh]) for hh in range(H_B)], axis=0)
        yb = _dot(jnp.concatenate(ps, axis=1), vst)
        if chained:
            qf = q * jnp.exp((pos + 1.0) * lg_f)
            qb = q * jnp.exp((float(CH) - pos) * lg_b)
            s_in = jnp.concatenate([uR_ref[c, 0], uR_ref[c, 1]], axis=0)
            yb = yb + _dot(_bf(jnp.concatenate([qf, qb], axis=1)), _bf(s_in))
        y_ref[rows, D_A:D_A + D_B] = yb
        return carry

    KC = H_C * DK_C
    CQ, CK, CV = D_A, D_A + KC, D_A + 2 * KC
    gla_scale = DK_C ** -0.5
    bd_c = _block_diag_mask(KC, D_C, DK_C, DV_C)
    hm_ck = [_lane_block_mask(KC, DK_C, hh) for hh in range(H_C)]
    hm_cv = [_lane_block_mask(D_C, DV_C, hh) for hh in range(H_C)]

    def gla_state_body(c, carry):
        rows = pl.ds(pl.multiple_of(c * CH, CH), CH)
        k = y_ref[rows, CK:CK + KC] * gla_scale
        v = y_ref[rows, CV:CV + D_C]
        tot = tot_ref[c]
        khf = k * jnp.exp(tot[:, LANES:2 * LANES] - bf_ref[rows, :])
        khb = k * jnp.exp(tot[:, 2 * LANES:] - bb_ref[rows, :])
        u = _dot(_bf(jnp.concatenate([khf, khb], axis=1).T), _bf(v))
        for d in range(2):
            u_d = u[d * KC:(d + 1) * KC, :]
            if chained:
                uG_ref[c, d] = u_d * bd_c
            else:
                for hh in range(H_C):
                    oG_ref[c, d, hh] = u_d[hh * DK_C:(hh + 1) * DK_C, hh * DV_C:(hh + 1) * DV_C]
        return carry

    def gla_state_ret_proj(c, carry):
        return gla_state_body(c, ret_proj_body(c, carry))

    if chained:
        lax.fori_loop(0, NCH, gla_state_ret_proj, 0, unroll=CHUNK_UNROLL)

    if chained:
        for d, order in ((0, range(NCH)), (1, range(NCH - 1, -1, -1))):
            s_run = _block_diag_value(sG_ref.at[d], H_C, DK_C, DV_C)
            for idx, c in enumerate(order):
                u = uG_ref[c, d]
                uG_ref[c, d] = s_run
                if idx != NCH - 1:
                    tot_row = tot_ref[c][:, (1 + d) * LANES:(2 + d) * LANES]
                    s_run = _row_to_col(jnp.exp(tot_row), KC) * s_run + u

    def gla_out_body(c, carry, factorised):
        rows = pl.ds(pl.multiple_of(c * CH, CH), CH)

        def operands():
            q = y_ref[rows, CQ:CQ + KC]
            k = y_ref[rows, CK:CK + KC] * gla_scale
            v = y_ref[rows, CV:CV + D_C]
            b_f, b_b = bf_ref[rows, :], bb_ref[rows, :]
            return q * jnp.exp(b_f), q * jnp.exp(b_b), k, v, b_f, b_b

        def inter_chunk(qf, qb):
            if not chained:
                return jnp.zeros((CH, D_C), F32)
            s_in = jnp.concatenate([uG_ref[c, 0], uG_ref[c, 1]], axis=0)
            return _dot(_bf(jnp.concatenate([qf, qb], axis=1)), _bf(s_in))

        qf, qb, k, v, b_f, b_b = operands()
        if rotary:
            rotary_body(c, 0)
        if factorised:
            lower, upper = _causal_masks()
            kf16, kb16 = _bf(k * jnp.exp(-b_f)), _bf(k * jnp.exp(-b_b))
            ps = []
            for hh in range(H_C):
                s_f = _dot_nt(_bf(qf * hm_ck[hh]), kf16)
                s_b = _dot_nt(_bf(qb * hm_ck[hh]), kb16)
                ps.append(_bf(jnp.where(lower, s_f, 0.0) + jnp.where(upper, s_b, 0.0)))
            vst = jnp.concatenate([_bf(v * hm_cv[hh]) for hh in range(H_C)], axis=0)
            yc = _dot(jnp.concatenate(ps, axis=1), vst)
            y_ref[rows, D_A + D_B:D_MODEL] = yc + inter_chunk(qf, qb) if chained else yc
            return carry

        y_ref[rows, D_A + D_B:D_MODEL] = inter_chunk(qf, qb)
        key = lax.broadcasted_iota(jnp.int32, (CH, 1), 0)
        sub = SUBLANES

        def query_body(g, carry2):
            grp = pl.ds(pl.multiple_of(c * CH + g * sub, sub), sub)
            bq_f, bq_b, qg = bf_ref[grp, :], bb_ref[grp, :], y_ref[grp, CQ:CQ + KC]
            outs = []
            for r in range(sub):
                i = g * sub + r
                w_f = jnp.exp(jnp.where(key <= i, bq_f[r:r + 1, :] - b_f, NEG))
                w_b = jnp.exp(jnp.where(key >= i, bq_b[r:r + 1, :] - b_b, NEG))
                t = _dot(_bf(qg[r:r + 1, :] * k * (w_f + w_b)), _bf(bd_c))
                outs.append(jnp.sum(t * v, axis=0, keepdims=True))
            y_ref[grp, D_A + D_B:D_MODEL] += jnp.concatenate(outs, axis=0)
            return carry2

        lax.fori_loop(0, CH // sub, query_body, 0)
        return carry

    all_safe = jnp.min(tot_ref[...][:, :, LANES:]) > GLA_SAFE_LOG

    @pl.when(all_safe)
    def _():
        chunk_loop(gla_state_ret_proj, functools.partial(gla_out_body, factorised=True))

    @pl.when(jnp.logical_not(all_safe))
    def _():
        chunk_loop(gla_state_ret_proj, functools.partial(gla_out_body, factorised=False), unroll=1)

    if chained:
        lax.fori_loop(0, NCH, ret_state_body, 0, unroll=CHUNK_UNROLL)
        ret_scan()
    chunk_loop(ret_state_body, ret_out_body)

    def gate_proj(g):
        rows = slice(g * PROJ_ROWS, (g + 1) * PROJ_ROWS)
        pj_ref[rows, 0:W_Z] = _dot(h_ref[rows, :], win_ref[:, OFF_Z:OFF_Z + W_Z])

    def head_norm_gate(c):
        rows = slice(c * CH, (c + 1) * CH)
        parts = []
        for hh in range(H_A):
            ya = y_ref[rows, hh * DH_A:(hh + 1) * DH_A]
            parts.append(ya * lax.rsqrt(jnp.mean(ya * ya, axis=1, keepdims=True) + EPS))
        ybc = y_ref[rows, D_A:D_MODEL]
        seg = _bf(_block_diag_mask(D_B + D_C, D_B + D_C, DH_B, DH_B) * (1.0 / DH_B))
        parts.append(ybc * lax.rsqrt(_wide_dot_r(ybc * ybc, seg) + EPS))
        z = pj_ref[rows, 0:W_Z]
        yg = jnp.concatenate(parts, axis=1) * hng_ref[...] * (z * _sigmoid(z))
        h_ref[rows, :] = _bf(yg)

    def out_proj_residual(g):
        mix = _dot(h_ref[g * PROJ_ROWS:(g + 1) * PROJ_ROWS, :], wout_ref[...])
        for j in range(per):
            c = g * per + j
            xo = x_rows(x_ref, c)[...] + mod_ref[:, 2 * D_MODEL:] * mix[j * CH:(j + 1) * CH, :]
            if final:
                xo = xo * lax.rsqrt(jnp.mean(xo * xo, axis=-1, keepdims=True) + EPS) * fg_ref[...]
            x_rows(out_ref, c)[...] = xo

    groups = NCH // per
    gate_proj(0)
    for g in range(groups):
        if g + 1 < groups:
            gate_proj(g + 1)
        for c in range(g * per, (g + 1) * per):
            head_norm_gate(c)
        if g > 0:
            out_proj_residual(g - 1)
    out_proj_residual(groups - 1)


def _layer_call(l, x, mods, mod_row, norm_g, w_in_r, gate_b, lg_rows, w2_full, b2_full, hn_g, w_out_b, final_g,
                states, rot=None, chained=False, final=False):
    batch, seqlen, d = x.shape
    seqs = TOK // seqlen
    steps = batch // seqs
    assert seqs * seqlen == TOK and steps * seqs == batch and seqlen % CH == 0
    assert chained == (seqs == 1)
    assert chained or seqlen == CH

    def const(shape):
        nd = len(shape)
        return pl.BlockSpec(shape, lambda i, _nd=nd: (0,) * _nd)

    def layer_block(shape):
        nd = len(shape)
        return pl.BlockSpec((None,) + shape, lambda i, _nd=nd: (l,) + (0,) * _nd)

    in_specs = [
        pl.BlockSpec((seqs, seqlen, d), lambda i: (i, 0, 0)),
        pl.BlockSpec((None, None, 1, 3 * d),
                     (lambda i: (l, i, 0, 0)) if mod_row is None else (lambda i: (l, mod_row, 0, 0))),
        layer_block((1, d)),
        pl.BlockSpec((None, d, D_INR), lambda i: (l, 0, 0), pipeline_mode=pl.Buffered(1)),
        layer_block((1, LANES)),
        layer_block((2, D_B)),
        layer_block((LANES, 2 * LANES)),
        layer_block((1, 2 * LANES)),
        layer_block((1, d)),
        pl.BlockSpec((None, d, d), lambda i: (l, 0, 0), pipeline_mode=pl.Buffered(1)),
    ]
    args = [x, mods, norm_g.reshape(-1, 1, d), w_in_r, gate_b, lg_rows, w2_full, b2_full,
            hn_g.reshape(-1, 1, d), w_out_b]
    for cst in _gate_constants():
        in_specs.append(pl.BlockSpec(cst.shape, lambda i, _nd=cst.ndim: (0,) * _nd, pipeline_mode=pl.Buffered(1)))
        args.append(cst)
    if final:
        in_specs.append(const((1, d)))
        args.append(final_g.reshape(1, d))
    if rot is not None:
        in_specs += [pl.BlockSpec((seqlen, D_B), lambda i: (0, 0), pipeline_mode=pl.Buffered(1))] * 2
        args += list(rot)
    state_blocks = [(2, H_A, DH_A, DH_A), (2, H_A, DH_A), (1, LANES), (2, H_B, DH_B, DH_B), (2, H_C, DK_C, DV_C)]
    out_shape = [jax.ShapeDtypeStruct(x.shape, x.dtype)]
    out_specs = [pl.BlockSpec((seqs, seqlen, d), lambda i: (i, 0, 0))]
    aliases = {}
    for blk, arr in zip(state_blocks, states):
        zeros = (0,) * len(blk)
        if chained:
            in_specs.append(pl.BlockSpec((None, None) + blk, lambda i, _z=zeros: (i, l) + _z))
        else:
            in_specs.append(pl.BlockSpec(memory_space=pl.ANY))
            aliases[len(args)] = len(out_shape)
            out_shape.append(jax.ShapeDtypeStruct(arr.shape, arr.dtype))
            out_specs.append(pl.BlockSpec((seqs, None) + blk, lambda i, _z=zeros: (i, l) + _z))
        args.append(arr)

    scratch = [
        pltpu.VMEM((TOK, d), BF16),
        pltpu.VMEM((TOK, W_A), F32),
        pltpu.VMEM((TOK, LANES), F32),
        pltpu.VMEM((TOK, LANES), F32),
        pltpu.VMEM((TOK, LANES), F32),
        pltpu.VMEM((TOK, LANES), F32),
        pltpu.VMEM((NCH, 1, 3 * LANES), F32),
        pltpu.VMEM((TOK, d), F32),
        pltpu.VMEM((NCH, 1, LANES), F32),
        pltpu.VMEM((NCH, 1, LANES), F32),
        pltpu.VMEM((TOK, LANES), F32),
        pltpu.VMEM((TOK, LANES), BF16),
    ]
    if chained:
        scratch += [
            pltpu.VMEM((NCH, 2, H_A, DH_A, 2 * DH_A), F32),
            pltpu.VMEM((NCH, 2, D_B, D_B), F32),
            pltpu.VMEM((NCH, 2, H_C * DK_C, D_C), F32),
        ]
    outs = pl.pallas_call(
        functools.partial(_layer_kernel, chained=chained, rotary=rot is not None, final=final),
        out_shape=out_shape,
        grid=(steps,),
        in_specs=in_specs,
        out_specs=out_specs,
        scratch_shapes=scratch,
        input_output_aliases=aliases,
        compiler_params=pltpu.CompilerParams(dimension_semantics=("arbitrary",), vmem_limit_bytes=VMEM_LIMIT),
        name=("latent" if chained else "context") + f"_layer{l}",
    )(*args)
    return outs


def _rotary_tables(seqlen):
    rows = seqlen // GRID_W
    r = jnp.repeat(jnp.arange(rows, dtype=F32), GRID_W)
    col = jnp.tile(jnp.arange(GRID_W, dtype=F32), rows)
    n_f = DH_B // 4
    freqs = ROPE_BASE ** (-jnp.arange(n_f, dtype=F32) / n_f)
    ang = jnp.concatenate([r[:, None] * freqs, col[:, None] * freqs], axis=-1)
    cos, sin = jnp.cos(ang), jnp.sin(ang)
    cos_l = jnp.tile(jnp.concatenate([cos, cos], axis=-1), (1, H_B))
    sin_l = jnp.tile(jnp.concatenate([-sin, sin], axis=-1), (1, H_B))
    return cos_l, sin_l


def _gate_lanes(m):
    z = jnp.zeros(m.shape[:-2] + (LANES,), m.dtype)
    z = z.at[..., COL_FF:COL_FF + H_A].set(m[..., 0, :]).at[..., COL_FB:COL_FB + H_A].set(m[..., 1, :])
    return z[..., None, :]


def _pack_segments():
    src = dict(zip("aq ak av ao az ag bq bk bv bz cq ck cv cz clr".split(),
                   zip(np.cumsum((0,) + IN_WIDTHS[:-1]).tolist(), IN_WIDTHS)))
    segs = []
    for names, dst in (("aq ak av ao", OFF_A), ("ag clr", OFF_S), ("bq bk bv", OFF_B), ("cq ck cv", OFF_C),
                       ("az bz cz", OFF_Z)):
        for name in names.split():
            s, w = src[name]
            segs.append((s, dst, w))
            dst += w
    return segs


def _pack_kernel(wt_ref, o_ref):
    tr = o_ref.shape[0]
    small = []
    for s, dst, w in _pack_segments():
        if w % LANES == 0:
            o_ref[:, dst:dst + w] = _bf(wt_ref[s:s + w, :].T)
        else:
            small.append(wt_ref[s:s + w, :])
    small.append(jnp.zeros((W_S - sum(t.shape[0] for t in small), tr), F32))
    o_ref[:, OFF_S:OFF_S + W_S] = _bf(jnp.concatenate(small, axis=0).T)


def _pack_w_in(w_in):
    depth, d, d_in = w_in.shape
    tr = 256
    return pl.pallas_call(
        _pack_kernel,
        out_shape=jax.ShapeDtypeStruct((depth, d, D_INR), BF16),
        grid=(depth, d // tr),
        in_specs=[pl.BlockSpec((None, d_in, tr), lambda l, r: (l, 0, r))],
        out_specs=pl.BlockSpec((None, tr, D_INR), lambda l, r: (l, r, 0)),
        name="pack_w_in",
    )(jnp.swapaxes(w_in, 1, 2))


def kernel(x_prompt, x_sample, state_mlstm_C, state_mlstm_n, state_mlstm_m, state_ret, state_gla, c, c_ctx,
           norm_g, w_ada, b_ada, w_in, mlstm_gate_b, ret_decay_logit, gla_w2, gla_b2, headnorm_g, w_out, final_g):
    depth = w_in.shape[0]
    dec_batch = c.shape[0]

    w_in_r = _pack_w_in(w_in)
    w_out_b = w_out.astype(BF16)
    n_g = mlstm_gate_b.shape[-1]
    gate_b = jnp.pad(mlstm_gate_b, ((0, 0), (0, LANES - n_g)))[:, None, :]
    lg_rows = jnp.repeat(ret_decay_logit, DH_B, axis=-1)
    w2_full = jnp.zeros((depth, LANES, 2 * LANES), F32)
    w2_full = w2_full.at[:, n_g:n_g + GLA_RANK, 0:LANES].set(gla_w2[:, 0])
    w2_full = w2_full.at[:, n_g + GLA_RANK:n_g + 2 * GLA_RANK, LANES:].set(gla_w2[:, 1]).astype(BF16)
    b2_full = gla_b2.reshape(depth, 1, 2 * LANES)

    mod_rows = -(-(dec_batch + 1) // SUBLANES) * SUBLANES
    cstack = jnp.zeros((mod_rows, D_MODEL), F32).at[0:dec_batch].set(c).at[dec_batch].set(c_ctx)
    mods = _modulation(cstack, w_ada, b_ada)[:, :, None, :]

    rot = _rotary_tables(x_sample.shape[1])
    cache = (state_mlstm_C, state_mlstm_n, _gate_lanes(state_mlstm_m), state_ret, state_gla)
    new = tuple(jnp.zeros((x_prompt.shape[0],) + s.shape[1:], F32) for s in cache)

    common = (norm_g, w_in_r, gate_b, lg_rows, w2_full, b2_full, headnorm_g, w_out_b, final_g)
    xp, xs = x_prompt, x_sample
    for l in range(depth):
        xp, *new = _layer_call(l, xp, mods, dec_batch, *common, states=new, final=l == depth - 1)
    for l in range(depth):
        xs = _layer_call(l, xs, mods, None, *common, states=cache, rot=rot, chained=True, final=l == depth - 1)[0]

    new_c, new_n, m_l, new_r, new_g = new
    new_m = jnp.stack([m_l[:, :, 0, COL_FF:COL_FF + H_A], m_l[:, :, 0, COL_FB:COL_FB + H_A]], axis=2)
    return (xp, xs, new_c, new_n, new_m, new_r, new_g)
```

```python
import functools

import jax
import jax.numpy as jnp
import numpy as np
from jax import lax
from jax.experimental import pallas as pl
from jax.experimental.pallas import tpu as pltpu

F32 = jnp.float32
BF16 = jnp.bfloat16

D_MODEL = 1024
H_A, DH_A = 4, 128
H_B, DH_B = 4, 64
H_C, DK_C, DV_C = 4, 32, 64
D_A, D_B, D_C = H_A * DH_A, H_B * DH_B, H_C * DV_C
GLA_RANK = 16
GLA_TAU = 16.0
GRID_W = 64
ROPE_BASE = 10000.0
EPS = 1e-6
IN_WIDTHS = (D_A, D_A, D_A, D_A, D_A, 4 * H_A, D_B, D_B, D_B, D_B, H_C * DK_C, H_C * DK_C, D_C, D_C, 2 * GLA_RANK)

LANES = 128
SUBLANES = 8
V7X_VMEM_BYTES = 64 * 1024 * 1024
CH = 256
TOK = 1024
NCH = TOK // CH
CHUNK_UNROLL = NCH
VMEM_LIMIT = V7X_VMEM_BYTES - 2 * 1024 * 1024
NEG = -1e30
GLA_SAFE_LOG = -80.0
MLSTM_SAFE_RANGE = 60.0

OFF_A, W_A = 0, 4 * D_A
OFF_S, W_S = OFF_A + W_A, LANES
OFF_B, W_B = OFF_S + W_S, 3 * D_B
OFF_C, W_C = OFF_B + W_B, 2 * H_C * DK_C + D_C
OFF_Z, W_Z = OFF_C + W_C, D_MODEL
D_INR = OFF_Z + W_Z
COL_FF, COL_FB = H_A, 3 * H_A


def _dot(a, b):
    return jnp.dot(a, b, preferred_element_type=F32)


def _dot_nt(a, b):
    return lax.dot_general(a, b, (((1,), (1,)), ((), ())), preferred_element_type=F32)


def _bf(x):
    return x.astype(BF16)


N_PIECES = 2


def _split(x):
    pieces, rest = [], x
    for _ in range(N_PIECES):
        pieces.append(_bf(rest))
        rest = rest - pieces[-1].astype(F32)
    return pieces


def _wide_dot_l(m_bf, x):
    return functools.reduce(jnp.add, [_dot(m_bf, p) for p in reversed(_split(x))])


def _wide_dot_r(x, m_bf):
    return functools.reduce(jnp.add, [_dot(p, m_bf) for p in reversed(_split(x))])


def _log_sigmoid(x):
    return jnp.minimum(x, 0.0) - jnp.log(1.0 + jnp.exp(-jnp.abs(x)))


def _sigmoid(x):
    return 1.0 / (1.0 + jnp.exp(-x))


def _row_to_col(row, n):
    eye = lax.broadcasted_iota(jnp.int32, (n, n), 0) == lax.broadcasted_iota(jnp.int32, (n, n), 1)
    return jnp.sum(jnp.where(eye, row, 0.0), axis=1, keepdims=True)


def _lane_block_mask(width, block, h):
    lane = lax.broadcasted_iota(jnp.int32, (1, width), 1)
    return ((lane >= h * block) & (lane < (h + 1) * block)).astype(F32)


def _log2(n):
    assert n & (n - 1) == 0
    return n.bit_length() - 1


def _block_diag_mask(rows, cols, rblock, cblock):
    r = lax.broadcasted_iota(jnp.int32, (rows, cols), 0) >> _log2(rblock)
    c = lax.broadcasted_iota(jnp.int32, (rows, cols), 1) >> _log2(cblock)
    return (r == c).astype(F32)


def _causal_masks():
    ri = lax.broadcasted_iota(jnp.int32, (CH, CH), 0)
    cj = lax.broadcasted_iota(jnp.int32, (CH, CH), 1)
    return cj <= ri, cj >= ri


def _bwd_gate_lanes():
    lane = lax.broadcasted_iota(jnp.int32, (1, LANES), 1)
    return (lane >= COL_FB) & (lane < COL_FB + H_A)


def _block_diag_value(ref, hn, a, b):
    rows = []
    for h in range(hn):
        wide = jnp.concatenate([ref[h], jnp.zeros((a, (hn - 1) * b), F32)], axis=1)
        rows.append(wide if h == 0 else pltpu.roll(wide, h * b, 1))
    return jnp.concatenate(rows, axis=0)


def _pieces(x):
    return jnp.concatenate(_split(x), axis=1)


def _cum_max(x, reverse):
    n = x.shape[0]
    row = lax.broadcasted_iota(jnp.int32, x.shape, 0)
    s = 1
    while s < n:
        if reverse:
            shifted = jnp.where(row < n - s, pltpu.roll(x, n - s, 0), NEG)
        else:
            shifted = jnp.where(row >= s, pltpu.roll(x, s, 0), NEG)
        x = jnp.maximum(x, shifted)
        s *= 2
    return x


GATE_BLOCK = 16


def _gate_constants():
    half = N_PIECES * GATE_BLOCK
    cols = [COL_FF + h for h in range(H_A)] + [COL_FB + h for h in range(H_A)]
    sp = np.zeros((2 * N_PIECES * LANES, LANES), np.float32)
    for part in range(2):
        for p in range(N_PIECES):
            for k in cols:
                sp[(part * N_PIECES + p) * LANES + k, part * half + p * GATE_BLOCK + k] = 1.0
    lane = np.arange(LANES)
    gl = np.zeros((16, LANES), np.float32)
    for hd, k in enumerate(cols):
        gl[hd] = (lane % GATE_BLOCK == k) & (lane < 2 * half)
    valid = np.isin(lane % GATE_BLOCK, cols)
    gl[2 * H_A] = valid & (lane < half)
    gl[2 * H_A + 1] = valid & (lane >= half) & (lane < 2 * half)
    return jnp.asarray(sp, BF16), jnp.asarray(gl, F32)


def _modulation_kernel(c_ref, w_ref, b_ref, o_ref):
    cv = c_ref[...]
    s = cv * _sigmoid(cv)
    o_ref[...] = _dot(_bf(s), _bf(w_ref[...])) + b_ref[...]


def _modulation(cstack, w_ada, b_ada):
    depth, d, d3 = w_ada.shape
    rows = cstack.shape[0]
    tn = 1024
    return pl.pallas_call(
        _modulation_kernel,
        out_shape=jax.ShapeDtypeStruct((depth, rows, d3), F32),
        grid=(depth, d3 // tn),
        in_specs=[
            pl.BlockSpec((rows, d), lambda l, j: (0, 0)),
            pl.BlockSpec((None, d, tn), lambda l, j: (l, 0, j)),
            pl.BlockSpec((None, 1, tn), lambda l, j: (l, 0, j)),
        ],
        out_specs=pl.BlockSpec((None, rows, tn), lambda l, j: (l, 0, j)),
        name="adaln_modulation",
    )(cstack, w_ada, b_ada.reshape(depth, 1, d3))


def _layer_kernel(*refs, chained, rotary, final):
    it = iter(refs)
    x_ref, mod_ref, ng_ref, win_ref, gb_ref, lg_ref, w2_ref, b2_ref, hng_ref, wout_ref = (next(it) for _ in range(10))
    sp_ref, gl_ref = next(it), next(it)
    fg_ref = next(it) if final else None
    if rotary:
        cos_ref, sin_ref = next(it), next(it)
    sC_ref, sn_ref, sm_ref_in, sR_ref, sG_ref = (next(it) for _ in range(5))
    out_ref = next(it)
    if not chained:
        oC_ref, on_ref, om_ref, oR_ref, oG_ref = (next(it) for _ in range(5))
    (h_ref, pj_ref, sm_ref, fc_ref, bf_ref, bb_ref, tot_ref, y_ref,
     gm_ref, min_ref, cm_ref) = (next(it) for _ in range(11))
    dec_ref = pj_ref.at[:, W_B:W_B + CH]
    if chained:
        uCN_ref, uR_ref, uG_ref = (next(it) for _ in range(3))

    seqs, seqlen, _ = x_ref.shape

    def x_rows(ref, c):
        if isinstance(c, int):
            r0 = c * CH
            return ref.at[r0 // seqlen, r0 % seqlen:r0 % seqlen + CH, :]
        r0 = pl.multiple_of(c * CH, CH)
        return ref.at[r0 >> _log2(seqlen), pl.ds(pl.multiple_of(r0 & (seqlen - 1), CH), CH), :]

    def norm_rows(c):
        x = x_rows(x_ref, c)[...]
        shift, scale = mod_ref[:, 0:D_MODEL], mod_ref[:, D_MODEL:2 * D_MODEL]
        xn = x * lax.rsqrt(jnp.mean(x * x, axis=-1, keepdims=True) + EPS) * ng_ref[...]
        h_ref[c * CH:(c + 1) * CH, :] = _bf(xn * (1.0 + scale) + shift)

    def front_rows(c):
        rows = slice(c * CH, (c + 1) * CH)
        proj = _dot(h_ref[rows, :], win_ref[:, OFF_A:OFF_S + W_S])
        pj_ref[rows, 0:W_A] = proj[:, 0:W_A]
        sm = proj[:, W_A:]
        sm_ref[rows, :] = sm
        la = _log_sigmoid(_dot(_bf(sm), w2_ref[...]) + b2_ref[...]) * (1.0 / GLA_TAU)
        tri = _bf(_causal_masks()[0].astype(F32))
        ls = _log_sigmoid(sm + gb_ref[...])
        ps = _wide_dot_l(tri, jnp.concatenate([ls, la], axis=1))
        tot = ps[CH - 1:CH, :]
        fc_ref[rows, :] = jnp.where(_bwd_gate_lanes(), tot[:, 0:LANES] - ps[:, 0:LANES] + ls, ps[:, 0:LANES])
        bf_ref[rows, :] = ps[:, LANES:2 * LANES]
        bb_ref[rows, :] = tot[:, 2 * LANES:] - ps[:, 2 * LANES:] + la[:, LANES:]
        tot_ref[c] = tot

    norm_rows(0)
    for c in range(NCH):
        if c + 1 < NCH:
            norm_rows(c + 1)
        front_rows(c)

    AQ, AK, AV, AO = 0, D_A, 2 * D_A, 3 * D_A
    k_scale = DH_A ** -0.5

    ones_cols = jnp.ones((CH, DH_A), F32)

    def mlstm_state_body(c, carry):
        rows = pl.ds(pl.multiple_of(c * CH, CH), CH)
        g_pre = sm_ref[rows, :] + gb_ref[...]
        rw = pltpu.roll(g_pre, H_A, 1) - fc_ref[rows, :]
        cm_ref[rows, :] = jnp.where(_bwd_gate_lanes(), _cum_max(rw, True), _cum_max(rw, False))
        totg = tot_ref[c][:, 0:LANES]
        g = totg + rw
        gm = jnp.max(g, axis=0, keepdims=True)
        gm_ref[c] = gm
        wgt = jnp.exp(g - gm).T
        y_ref[rows, D_A:D_MODEL] = _dot(h_ref[rows, :], win_ref[:, OFF_C:OFF_C + W_C])
        if not chained:
            m_new = jnp.maximum(totg, gm)
            b_row = jnp.exp(gm - m_new)
            om_ref[c] = m_new
            min_ref[c] = jnp.zeros((1, LANES), F32)
        for hh in range(H_A):
            kt = (pj_ref[rows, AK + hh * DH_A:AK + (hh + 1) * DH_A] * k_scale).T
            v_h = pj_ref[rows, AV + hh * DH_A:AV + (hh + 1) * DH_A]
            lhs = jnp.concatenate([kt * wgt[COL_FF + hh:COL_FF + hh + 1, :],
                                   kt * wgt[COL_FB + hh:COL_FB + hh + 1, :]], axis=0)
            u = _dot(_bf(lhs), _bf(jnp.concatenate([v_h, ones_cols], axis=1)))
            for d, col in ((0, COL_FF), (1, COL_FB)):
                u_d = u[d * DH_A:(d + 1) * DH_A, :]
                if chained:
                    uCN_ref[c, d, hh] = u_d
                else:
                    fin = b_row[:, col + hh:col + hh + 1] * u_d
                    oC_ref[c, d, hh] = fin[:, 0:DH_A]
                    on_ref[c, d, hh:hh + 1, :] = fin[:, DH_A:].T[0:1, :]
        return carry

    def chunk_loop(state_body, out_body, unroll=CHUNK_UNROLL):
        if chained:
            return lax.fori_loop(0, NCH, out_body, 0, unroll=unroll)
        return lax.fori_loop(0, NCH, lambda c, carry: out_body(c, state_body(c, carry)), 0, unroll=unroll)

    lax.fori_loop(0, NCH, mlstm_state_body, 0, unroll=CHUNK_UNROLL)

    if chained:
        for d, order, col in ((0, range(NCH), COL_FF), (1, range(NCH - 1, -1, -1), COL_FB)):
            m_run = sm_ref_in[...]
            cn_run = [jnp.concatenate([sC_ref[d, hh],
                                       jnp.broadcast_to(sn_ref[d, hh:hh + 1, :], (DH_A, DH_A)).T], axis=1)
                      for hh in range(H_A)]
            for idx, c in enumerate(order):
                last = idx == NCH - 1
                if not last:
                    totg, gm = tot_ref[c][:, 0:LANES], gm_ref[c]
                    m_new = jnp.maximum(totg + m_run, gm)
                    a_row, b_row = jnp.exp(totg + m_run - m_new), jnp.exp(gm - m_new)
                for hh in range(H_A):
                    u = uCN_ref[c, d, hh]
                    uCN_ref[c, d, hh] = cn_run[hh]
                    if not last:
                        cn_run[hh] = (a_row[:, col + hh:col + hh + 1] * cn_run[hh]
                                      + b_row[:, col + hh:col + hh + 1] * u)
                min_ref[c] = m_run if d == 0 else jnp.where(_bwd_gate_lanes(), m_run, min_ref[c])
                if not last:
                    m_run = m_new

    def mlstm_out_body(c, carry, factorised):
        rows = pl.ds(pl.multiple_of(c * CH, CH), CH)
        lower, upper = _causal_masks()
        m_prev = min_ref[c]
        cmx = jnp.maximum(cm_ref[rows, :], m_prev)
        mi = fc_ref[rows, :] + cmx
        rw = pltpu.roll(sm_ref[rows, :] + gb_ref[...], H_A, 1) - fc_ref[rows, :]
        floor_all = jnp.exp(-mi)
        wa_all = jnp.exp(m_prev - cmx)
        if factorised:
            top = jnp.max(cmx, axis=0, keepdims=True)
            key_w, query_w = jnp.exp(rw - top), jnp.exp(top - cmx)
        else:
            a_op = _bf(_dot(_pieces(-cmx), sp_ref[0:N_PIECES * LANES, :]) + gl_ref[2 * H_A + 1:2 * H_A + 2, :])
            b_op = _bf(_dot(_pieces(rw), sp_ref[N_PIECES * LANES:, :]) + gl_ref[2 * H_A:2 * H_A + 1, :])
        for hh in range(H_A):
            q_h = pj_ref[rows, AQ + hh * DH_A:AQ + (hh + 1) * DH_A]
            q16 = _bf(q_h)
            k_h = pj_ref[rows, AK + hh * DH_A:AK + (hh + 1) * DH_A] * k_scale
            v_h = pj_ref[rows, AV + hh * DH_A:AV + (hh + 1) * DH_A]
            o_h = pj_ref[rows, AO + hh * DH_A:AO + (hh + 1) * DH_A]
            if not factorised:
                s = _dot_nt(q16, _bf(k_h))
            qs = []
            for d, msk, col in ((0, lower, COL_FF + hh), (1, upper, COL_FB + hh)):
                if factorised:
                    q_w = q_h * jnp.broadcast_to(query_w[:, col:col + 1], (CH, DH_A))
                    k_w = k_h * jnp.broadcast_to(key_w[:, col:col + 1], (CH, DH_A))
                    qs.append(_bf(jnp.where(msk, _dot_nt(_bf(q_w), _bf(k_w)), 0.0)))
                    continue
                hd = d * H_A + hh
                dm = _dot_nt(a_op * _bf(gl_ref[hd:hd + 1, :]), b_op)
                qs.append(_bf(s * jnp.exp(jnp.where(msk, dm, NEG))))
            nd = _dot(jnp.concatenate(qs, axis=0), _bf(jnp.concatenate([v_h, ones_cols], axis=1)))
            hs = None
            for d, col in ((0, COL_FF + hh), (1, COL_FB + hh)):
                num, den = nd[d * CH:(d + 1) * CH, 0:DH_A], nd[d * CH:(d + 1) * CH, DH_A:]
                if chained:
                    wa = jnp.broadcast_to(wa_all[:, col:col + 1], (CH, DH_A))
                    inter = _dot(q16, _bf(uCN_ref[c, d, hh]))
                    num = num + wa * inter[:, 0:DH_A]
                    den = den + wa * inter[:, DH_A:]
                floor = jnp.broadcast_to(floor_all[:, col:col + 1], (CH, DH_A))
                part = num / jnp.maximum(jnp.abs(den), floor)
                hs = part if hs is None else hs + part
            y_ref[rows, hh * DH_A:(hh + 1) * DH_A] = _sigmoid(o_h) * hs
        return carry

    lane = lax.broadcasted_iota(jnp.int32, (1, LANES), 1)
    gate_lane = ((lane >= COL_FF) & (lane < COL_FF + H_A)) | _bwd_gate_lanes()
    spread = jnp.float32(0.0)
    for c in range(NCH):
        cmx_c = jnp.maximum(cm_ref[c * CH:(c + 1) * CH, :], min_ref[c])
        rng = jnp.max(cmx_c, axis=0, keepdims=True) - jnp.min(cmx_c, axis=0, keepdims=True)
        spread = jnp.maximum(spread, jnp.max(jnp.where(gate_lane, rng, 0.0)))
    mlstm_safe = spread < MLSTM_SAFE_RANGE

    @pl.when(mlstm_safe)
    def _():
        lax.fori_loop(0, NCH, functools.partial(mlstm_out_body, factorised=True), 0, unroll=CHUNK_UNROLL)

    @pl.when(jnp.logical_not(mlstm_safe))
    def _():
        lax.fori_loop(0, NCH, functools.partial(mlstm_out_body, factorised=False), 0)

    BQ, BK, BV = 0, D_B, 2 * D_B

    def ret_proj_body(c, carry):
        rows = pl.ds(pl.multiple_of(c * CH, CH), CH)
        pj_ref[rows, 0:W_B] = _dot(h_ref[rows, :], win_ref[:, OFF_B:OFF_B + W_B])
        return carry

    def rotary_body(c, carry):
        rows = pl.ds(pl.multiple_of(c * CH, CH), CH)
        first_half = (lax.broadcasted_iota(jnp.int32, (1, D_B), 1) & (DH_B - 1)) < DH_B // 2
        for off in (BQ, BK):
            t = pj_ref[rows, off:off + D_B]
            partner = jnp.where(first_half, pltpu.roll(t, D_B - DH_B // 2, 1), pltpu.roll(t, DH_B // 2, 1))
            pj_ref[rows, off:off + D_B] = t * cos_ref[rows, :] + partner * sin_ref[rows, :]
        return carry

    lgam = _log_sigmoid(lg_ref[...])
    lg_f, lg_b = lgam[0:1, :], lgam[1:2, :]
    pos = lax.broadcasted_iota(jnp.int32, (CH, 1), 0).astype(F32)
    lower, upper = _causal_masks()
    rel = (lax.broadcasted_iota(jnp.int32, (CH, CH), 0) - lax.broadcasted_iota(jnp.int32, (CH, CH), 1)).astype(F32)
    for hh in range(H_B):
        lf = lg_f[:, hh * DH_B:hh * DH_B + 1]
        lb = lg_b[:, hh * DH_B:hh * DH_B + 1]
        dec_ref[hh * CH:(hh + 1) * CH, :] = (jnp.where(lower, jnp.exp(jnp.maximum(rel, 0.0) * lf), 0.0)
                                             + jnp.where(upper, jnp.exp(jnp.maximum(-rel, 0.0) * lb), 0.0))
    bd_b = _block_diag_mask(D_B, D_B, DH_B, DH_B)
    hm_b = [_lane_block_mask(D_B, DH_B, hh) for hh in range(H_B)]
    ret_scale = DH_B ** -0.5

    def ret_state_body(c, carry):
        rows = pl.ds(pl.multiple_of(c * CH, CH), CH)
        k = pj_ref[rows, BK:BK + D_B] * ret_scale
        v = pj_ref[rows, BV:BV + D_B]
        kf = k * jnp.exp((CH - 1.0 - pos) * lg_f)
        kb = k * jnp.exp(pos * lg_b)
        u = _dot(_bf(jnp.concatenate([kf, kb], axis=1).T), _bf(v))
        for d in range(2):
            u_d = u[d * D_B:(d + 1) * D_B, :]
            if chained:
                uR_ref[c, d] = u_d * bd_b
            else:
                for hh in range(H_B):
                    oR_ref[c, d, hh] = u_d[hh * DH_B:(hh + 1) * DH_B, hh * DH_B:(hh + 1) * DH_B]
        return carry

    def ret_scan():
        for d, order, lg_row in ((0, range(NCH), lg_f), (1, range(NCH - 1, -1, -1), lg_b)):
            g_col = _row_to_col(jnp.exp(float(CH) * lg_row), D_B)
            s_run = _block_diag_value(sR_ref.at[d], H_B, DH_B, DH_B)
            for idx, c in enumerate(order):
                u = uR_ref[c, d]
                uR_ref[c, d] = s_run
                if idx != NCH - 1:
                    s_run = g_col * s_run + u

    def ret_out_body(c, carry):
        rows = pl.ds(pl.multiple_of(c * CH, CH), CH)
        q = pj_ref[rows, BQ:BQ + D_B]
        k = pj_ref[rows, BK:BK + D_B] * ret_scale
        v = pj_ref[rows, BV:BV + D_B]
        kb16 = _bf(k)
        ps = [_bf(_dot_nt(_bf(q * hm_b[hh]), kb16) * dec_ref[hh * CH:(hh + 1) * CH, :]) for hh in range(H_B)]
        vst = jnp.concatenate([_bf(v * hm_b[hh]) for hh in range(H_B)], axis=0)
        yb = _dot(jnp.concatenate(ps, axis=1), vst)
        if chained:
            qf = q * jnp.exp((pos + 1.0) * lg_f)
            qb = q * jnp.exp((float(CH) - pos) * lg_b)
            s_in = jnp.concatenate([uR_ref[c, 0], uR_ref[c, 1]], axis=0)
            yb = yb + _dot(_bf(jnp.concatenate([qf, qb], axis=1)), _bf(s_in))
        y_ref[rows, D_A:D_A + D_B] = yb
        return carry

    KC = H_C * DK_C
    CQ, CK, CV = D_A, D_A + KC, D_A + 2 * KC
    gla_scale = DK_C ** -0.5
    bd_c = _block_diag_mask(KC, D_C, DK_C, DV_C)
    hm_ck = [_lane_block_mask(KC, DK_C, hh) for hh in range(H_C)]
    hm_cv = [_lane_block_mask(D_C, DV_C, hh) for hh in range(H_C)]

    def gla_state_body(c, carry):
        rows = pl.ds(pl.multiple_of(c * CH, CH), CH)
        k = y_ref[rows, CK:CK + KC] * gla_scale
        v = y_ref[rows, CV:CV + D_C]
        tot = tot_ref[c]
        khf = k * jnp.exp(tot[:, LANES:2 * LANES] - bf_ref[rows, :])
        khb = k * jnp.exp(tot[:, 2 * LANES:] - bb_ref[rows, :])
        u = _dot(_bf(jnp.concatenate([khf, khb], axis=1).T), _bf(v))
        for d in range(2):
            u_d = u[d * KC:(d + 1) * KC, :]
            if chained:
                uG_ref[c, d] = u_d * bd_c
            else:
                for hh in range(H_C):
                    oG_ref[c, d, hh] = u_d[hh * DK_C:(hh + 1) * DK_C, hh * DV_C:(hh + 1) * DV_C]
        return carry

    def gla_state_ret_proj(c, carry):
        return gla_state_body(c, ret_proj_body(c, carry))

    if chained:
        lax.fori_loop(0, NCH, gla_state_ret_proj, 0, unroll=CHUNK_UNROLL)

    if chained:
        for d, order in ((0, range(NCH)), (1, range(NCH - 1, -1, -1))):
            s_run = _block_diag_value(sG_ref.at[d], H_C, DK_C, DV_C)
            for idx, c in enumerate(order):
                u = uG_ref[c, d]
                uG_ref[c, d] = s_run
                if idx != NCH - 1:
                    tot_row = tot_ref[c][:, (1 + d) * LANES:(2 + d) * LANES]
                    s_run = _row_to_col(jnp.exp(tot_row), KC) * s_run + u

    def gla_out_body(c, carry, factorised):
        rows = pl.ds(pl.multiple_of(c * CH, CH), CH)

        def operands():
            q = y_ref[rows, CQ:CQ + KC]
            k = y_ref[rows, CK:CK + KC] * gla_scale
            v = y_ref[rows, CV:CV + D_C]
            b_f, b_b = bf_ref[rows, :], bb_ref[rows, :]
            return q * jnp.exp(b_f), q * jnp.exp(b_b), k, v, b_f, b_b

        def inter_chunk(qf, qb):
            if not chained:
                return jnp.zeros((CH, D_C), F32)
            s_in = jnp.concatenate([uG_ref[c, 0], uG_ref[c, 1]], axis=0)
            return _dot(_bf(jnp.concatenate([qf, qb], axis=1)), _bf(s_in))

        qf, qb, k, v, b_f, b_b = operands()
        if rotary:
            rotary_body(c, 0)
        if factorised:
            lower, upper = _causal_masks()
            kf16, kb16 = _bf(k * jnp.exp(-b_f)), _bf(k * jnp.exp(-b_b))
            ps = []
            for hh in range(H_C):
                s_f = _dot_nt(_bf(qf * hm_ck[hh]), kf16)
                s_b = _dot_nt(_bf(qb * hm_ck[hh]), kb16)
                ps.append(_bf(jnp.where(lower, s_f, 0.0) + jnp.where(upper, s_b, 0.0)))
            vst = jnp.concatenate([_bf(v * hm_cv[hh]) for hh in range(H_C)], axis=0)
            yc = _dot(jnp.concatenate(ps, axis=1), vst)
            y_ref[rows, D_A + D_B:D_MODEL] = yc + inter_chunk(qf, qb) if chained else yc
            return carry

        y_ref[rows, D_A + D_B:D_MODEL] = inter_chunk(qf, qb)
        key = lax.broadcasted_iota(jnp.int32, (CH, 1), 0)
        sub = SUBLANES

        def query_body(g, carry2):
            grp = pl.ds(pl.multiple_of(c * CH + g * sub, sub), sub)
            bq_f, bq_b, qg = bf_ref[grp, :], bb_ref[grp, :], y_ref[grp, CQ:CQ + KC]
            outs = []
            for r in range(sub):
                i = g * sub + r
                w_f = jnp.exp(jnp.where(key <= i, bq_f[r:r + 1, :] - b_f, NEG))
                w_b = jnp.exp(jnp.where(key >= i, bq_b[r:r + 1, :] - b_b, NEG))
                t = _dot(_bf(qg[r:r + 1, :] * k * (w_f + w_b)), _bf(bd_c))
                outs.append(jnp.sum(t * v, axis=0, keepdims=True))
            y_ref[grp, D_A + D_B:D_MODEL] += jnp.concatenate(outs, axis=0)
            return carry2

        lax.fori_loop(0, CH // sub, query_body, 0)
        return carry

    all_safe = jnp.min(tot_ref[...][:, :, LANES:]) > GLA_SAFE_LOG

    @pl.when(all_safe)
    def _():
        chunk_loop(gla_state_ret_proj, functools.partial(gla_out_body, factorised=True))

    @pl.when(jnp.logical_not(all_safe))
    def _():
        chunk_loop(gla_state_ret_proj, functools.partial(gla_out_body, factorised=False), unroll=1)

    if chained:
        lax.fori_loop(0, NCH, ret_state_body, 0, unroll=CHUNK_UNROLL)
        ret_scan()
    chunk_loop(ret_state_body, ret_out_body)

    def gate_proj(c):
        rows = slice(c * CH, (c + 1) * CH)
        pj_ref[rows, 0:W_Z] = _dot(h_ref[rows, :], win_ref[:, OFF_Z:OFF_Z + W_Z])

    def head_norm_gate(c):
        rows = slice(c * CH, (c + 1) * CH)
        parts = []
        for hh in range(H_A):
            ya = y_ref[rows, hh * DH_A:(hh + 1) * DH_A]
            parts.append(ya * lax.rsqrt(jnp.mean(ya * ya, axis=1, keepdims=True) + EPS))
        ybc = y_ref[rows, D_A:D_MODEL]
        seg = _bf(_block_diag_mask(D_B + D_C, D_B + D_C, DH_B, DH_B) * (1.0 / DH_B))
        parts.append(ybc * lax.rsqrt(_wide_dot_r(ybc * ybc, seg) + EPS))
        z = pj_ref[rows, 0:W_Z]
        yg = jnp.concatenate(parts, axis=1) * hng_ref[...] * (z * _sigmoid(z))
        h_ref[rows, :] = _bf(yg)

    def out_proj_residual(c):
        rows = slice(c * CH, (c + 1) * CH)
        xo = x_rows(x_ref, c)[...] + mod_ref[:, 2 * D_MODEL:] * _dot(h_ref[rows, :], wout_ref[...])
        if final:
            xo = xo * lax.rsqrt(jnp.mean(xo * xo, axis=-1, keepdims=True) + EPS) * fg_ref[...]
        x_rows(out_ref, c)[...] = xo

    gate_proj(0)
    for c in range(NCH):
        if c + 1 < NCH:
            gate_proj(c + 1)
        head_norm_gate(c)
        if c > 0:
            out_proj_residual(c - 1)
    out_proj_residual(NCH - 1)


def _layer_call(l, x, mods, mod_row, norm_g, w_in_r, gate_b, lg_rows, w2_full, b2_full, hn_g, w_out_b, final_g,
                states, rot=None, chained=False, final=False):
    batch, seqlen, d = x.shape
    seqs = TOK // seqlen
    steps = batch // seqs
    assert seqs * seqlen == TOK and steps * seqs == batch and seqlen % CH == 0
    assert chained == (seqs == 1)
    assert chained or seqlen == CH

    def const(shape):
        nd = len(shape)
        return pl.BlockSpec(shape, lambda i, _nd=nd: (0,) * _nd)

    def layer_block(shape):
        nd = len(shape)
        return pl.BlockSpec((None,) + shape, lambda i, _nd=nd: (l,) + (0,) * _nd)

    in_specs = [
        pl.BlockSpec((seqs, seqlen, d), lambda i: (i, 0, 0)),
        pl.BlockSpec((None, None, 1, 3 * d),
                     (lambda i: (l, i, 0, 0)) if mod_row is None else (lambda i: (l, mod_row, 0, 0))),
        layer_block((1, d)),
        pl.BlockSpec((None, d, D_INR), lambda i: (l, 0, 0), pipeline_mode=pl.Buffered(1)),
        layer_block((1, LANES)),
        layer_block((2, D_B)),
        layer_block((LANES, 2 * LANES)),
        layer_block((1, 2 * LANES)),
        layer_block((1, d)),
        pl.BlockSpec((None, d, d), lambda i: (l, 0, 0), pipeline_mode=pl.Buffered(1)),
    ]
    args = [x, mods, norm_g.reshape(-1, 1, d), w_in_r, gate_b, lg_rows, w2_full, b2_full,
            hn_g.reshape(-1, 1, d), w_out_b]
    for cst in _gate_constants():
        in_specs.append(pl.BlockSpec(cst.shape, lambda i, _nd=cst.ndim: (0,) * _nd, pipeline_mode=pl.Buffered(1)))
        args.append(cst)
    if final:
        in_specs.append(const((1, d)))
        args.append(final_g.reshape(1, d))
    if rot is not None:
        in_specs += [pl.BlockSpec((seqlen, D_B), lambda i: (0, 0), pipeline_mode=pl.Buffered(1))] * 2
        args += list(rot)
    state_blocks = [(2, H_A, DH_A, DH_A), (2, H_A, DH_A), (1, LANES), (2, H_B, DH_B, DH_B), (2, H_C, DK_C, DV_C)]
    out_shape = [jax.ShapeDtypeStruct(x.shape, x.dtype)]
    out_specs = [pl.BlockSpec((seqs, seqlen, d), lambda i: (i, 0, 0))]
    aliases = {}
    for blk, arr in zip(state_blocks, states):
        zeros = (0,) * len(blk)
        if chained:
            in_specs.append(pl.BlockSpec((None, None) + blk, lambda i, _z=zeros: (i, l) + _z))
        else:
            in_specs.append(pl.BlockSpec(memory_space=pl.ANY))
            aliases[len(args)] = len(out_shape)
            out_shape.append(jax.ShapeDtypeStruct(arr.shape, arr.dtype))
            out_specs.append(pl.BlockSpec((seqs, None) + blk, lambda i, _z=zeros: (i, l) + _z))
        args.append(arr)

    scratch = [
        pltpu.VMEM((TOK, d), BF16),
        pltpu.VMEM((TOK, W_A), F32),
        pltpu.VMEM((TOK, LANES), F32),
        pltpu.VMEM((TOK, LANES), F32),
        pltpu.VMEM((TOK, LANES), F32),
        pltpu.VMEM((TOK, LANES), F32),
        pltpu.VMEM((NCH, 1, 3 * LANES), F32),
        pltpu.VMEM((TOK, d), F32),
        pltpu.VMEM((NCH, 1, LANES), F32),
        pltpu.VMEM((NCH, 1, LANES), F32),
        pltpu.VMEM((TOK, LANES), F32),
    ]
    if chained:
        scratch += [
            pltpu.VMEM((NCH, 2, H_A, DH_A, 2 * DH_A), F32),
            pltpu.VMEM((NCH, 2, D_B, D_B), F32),
            pltpu.VMEM((NCH, 2, H_C * DK_C, D_C), F32),
        ]
    outs = pl.pallas_call(
        functools.partial(_layer_kernel, chained=chained, rotary=rot is not None, final=final),
        out_shape=out_shape,
        grid=(steps,),
        in_specs=in_specs,
        out_specs=out_specs,
        scratch_shapes=scratch,
        input_output_aliases=aliases,
        compiler_params=pltpu.CompilerParams(dimension_semantics=("arbitrary",), vmem_limit_bytes=VMEM_LIMIT),
        name=("latent" if chained else "context") + f"_layer{l}",
    )(*args)
    return outs


def _rotary_tables(seqlen):
    rows = seqlen // GRID_W
    r = jnp.repeat(jnp.arange(rows, dtype=F32), GRID_W)
    col = jnp.tile(jnp.arange(GRID_W, dtype=F32), rows)
    n_f = DH_B // 4
    freqs = ROPE_BASE ** (-jnp.arange(n_f, dtype=F32) / n_f)
    ang = jnp.concatenate([r[:, None] * freqs, col[:, None] * freqs], axis=-1)
    cos, sin = jnp.cos(ang), jnp.sin(ang)
    cos_l = jnp.tile(jnp.concatenate([cos, cos], axis=-1), (1, H_B))
    sin_l = jnp.tile(jnp.concatenate([-sin, sin], axis=-1), (1, H_B))
    return cos_l, sin_l


def _gate_lanes(m):
    z = jnp.zeros(m.shape[:-2] + (LANES,), m.dtype)
    z = z.at[..., COL_FF:COL_FF + H_A].set(m[..., 0, :]).at[..., COL_FB:COL_FB + H_A].set(m[..., 1, :])
    return z[..., None, :]


def _pack_segments():
    src = dict(zip("aq ak av ao az ag bq bk bv bz cq ck cv cz clr".split(),
                   zip(np.cumsum((0,) + IN_WIDTHS[:-1]).tolist(), IN_WIDTHS)))
    segs = []
    for names, dst in (("aq ak av ao", OFF_A), ("ag clr", OFF_S), ("bq bk bv", OFF_B), ("cq ck cv", OFF_C),
                       ("az bz cz", OFF_Z)):
        for name in names.split():
            s, w = src[name]
            segs.append((s, dst, w))
            dst += w
    return segs


def _pack_kernel(wt_ref, o_ref):
    tr = o_ref.shape[0]
    small = []
    for s, dst, w in _pack_segments():
        if w % LANES == 0:
            o_ref[:, dst:dst + w] = _bf(wt_ref[s:s + w, :].T)
        else:
            small.append(wt_ref[s:s + w, :])
    small.append(jnp.zeros((W_S - sum(t.shape[0] for t in small), tr), F32))
    o_ref[:, OFF_S:OFF_S + W_S] = _bf(jnp.concatenate(small, axis=0).T)


def _pack_w_in(w_in):
    depth, d, d_in = w_in.shape
    tr = 256
    return pl.pallas_call(
        _pack_kernel,
        out_shape=jax.ShapeDtypeStruct((depth, d, D_INR), BF16),
        grid=(depth, d // tr),
        in_specs=[pl.BlockSpec((None, d_in, tr), lambda l, r: (l, 0, r))],
        out_specs=pl.BlockSpec((None, tr, D_INR), lambda l, r: (l, r, 0)),
        name="pack_w_in",
    )(jnp.swapaxes(w_in, 1, 2))


def kernel(x_prompt, x_sample, state_mlstm_C, state_mlstm_n, state_mlstm_m, state_ret, state_gla, c, c_ctx,
           norm_g, w_ada, b_ada, w_in, mlstm_gate_b, ret_decay_logit, gla_w2, gla_b2, headnorm_g, w_out, final_g):
    depth = w_in.shape[0]
    dec_batch = c.shape[0]

    w_in_r = _pack_w_in(w_in)
    w_out_b = w_out.astype(BF16)
    n_g = mlstm_gate_b.shape[-1]
    gate_b = jnp.pad(mlstm_gate_b, ((0, 0), (0, LANES - n_g)))[:, None, :]
    lg_rows = jnp.repeat(ret_decay_logit, DH_B, axis=-1)
    w2_full = jnp.zeros((depth, LANES, 2 * LANES), F32)
    w2_full = w2_full.at[:, n_g:n_g + GLA_RANK, 0:LANES].set(gla_w2[:, 0])
    w2_full = w2_full.at[:, n_g + GLA_RANK:n_g + 2 * GLA_RANK, LANES:].set(gla_w2[:, 1]).astype(BF16)
    b2_full = gla_b2.reshape(depth, 1, 2 * LANES)

    mod_rows = -(-(dec_batch + 1) // SUBLANES) * SUBLANES
    cstack = jnp.zeros((mod_rows, D_MODEL), F32).at[0:dec_batch].set(c).at[dec_batch].set(c_ctx)
    mods = _modulation(cstack, w_ada, b_ada)[:, :, None, :]

    rot = _rotary_tables(x_sample.shape[1])
    cache = (state_mlstm_C, state_mlstm_n, _gate_lanes(state_mlstm_m), state_ret, state_gla)
    new = tuple(jnp.zeros((x_prompt.shape[0],) + s.shape[1:], F32) for s in cache)

    common = (norm_g, w_in_r, gate_b, lg_rows, w2_full, b2_full, headnorm_g, w_out_b, final_g)
    xp, xs = x_prompt, x_sample
    for l in range(depth):
        xp, *new = _layer_call(l, xp, mods, dec_batch, *common, states=new, final=l == depth - 1)
    for l in range(depth):
        xs = _layer_call(l, xs, mods, None, *common, states=cache, rot=rot, chained=True, final=l == depth - 1)[0]

    new_c, new_n, m_l, new_r, new_g = new
    new_m = jnp.stack([m_l[:, :, 0, COL_FF:COL_FF + H_A], m_l[:, :, 0, COL_FB:COL_FB + H_A]], axis=2)
    return (xp, xs, new_c, new_n, new_m, new_r, new_g)
```
